```python
import math
import jax, jax.numpy as jnp
from jax import lax
import numpy as np

D_MODEL = 2048
BATCH = 8
SEQ = 4096
DEPTH = 1

CHUNK = 64
Q_BLOCK = 128
HEAD_DIM = 128
N_HEADS_DIFF = D_MODEL // (2 * HEAD_DIM)
DIFF_QK_DIM = HEAD_DIM // 2
N_HEADS_SB = D_MODEL // (2 * HEAD_DIM)
DIFF_WIDTH = N_HEADS_DIFF * HEAD_DIM
SB_WIDTH = N_HEADS_SB * HEAD_DIM
MIX_WIDTH = DIFF_WIDTH + SB_WIDTH
IN_COLS = 3 * DIFF_WIDTH + 3 * SB_WIDTH
N_MEM = 256
N_HEADS_MEM = 4
MEM_HEAD_DIM = D_MODEL // N_HEADS_MEM
D_FF = ((8 * D_MODEL // 3 + 255) // 256) * 256
N_BUCKETS = 32
MAX_DISTANCE = 128
EPS = 1e-6

kernel_name = "hybrid_diff_stickbreak_chunk_encoder"


def rms_norm(x, g):
    xf = x.astype(jnp.float32)
    y = xf * lax.rsqrt(jnp.mean(xf * xf, axis=-1, keepdims=True) + EPS)
    return (y * g.astype(jnp.float32)).astype(x.dtype)


def lambda_init(layer_idx):
    return 0.8 - 0.6 * math.exp(-0.3 * layer_idx)


def t5_bucket(rel):
    nb = N_BUCKETS // 2
    ret = jnp.where(rel > 0, nb, 0)
    n = jnp.abs(rel)
    max_exact = nb // 2
    nf = jnp.maximum(n, 1).astype(jnp.float32)
    large = max_exact + (jnp.log(nf / max_exact) / math.log(MAX_DISTANCE / max_exact)
                         * (nb - max_exact)).astype(jnp.int32)
    large = jnp.minimum(large, nb - 1)
    return ret + jnp.where(n < max_exact, n, large)


def to_blocks(q):
    b, s, h, d = q.shape
    return q.reshape(b, s // Q_BLOCK, Q_BLOCK, h, d).transpose(1, 0, 2, 3, 4)


def from_blocks(o):
    nb, b, qb, h, d = o.shape
    return o.transpose(1, 0, 2, 3, 4).reshape(b, nb * qb, h, d)


def diff_attention(q, k, v, bias_table, lam, sub_gain, lam_init):
    seq = q.shape[1]
    key_pos = jnp.arange(seq, dtype=jnp.int32)
    k1, k2 = k[..., :DIFF_QK_DIM], k[..., DIFF_QK_DIM:]
    scale = DIFF_QK_DIM ** -0.5

    def block(args):
        i, qi = args
        q_pos = i * Q_BLOCK + jnp.arange(Q_BLOCK, dtype=jnp.int32)
        allowed = (key_pos[None, :] // CHUNK) <= (q_pos[:, None] // CHUNK)
        bias = bias_table[t5_bucket(key_pos[None, :] - q_pos[:, None])]
        bias = jnp.transpose(bias, (2, 0, 1)).astype(jnp.float32)[None]

        def softmax_map(qh, kh):
            s = jnp.einsum('bqhd,bkhd->bhqk', qh, kh).astype(jnp.float32) * scale + bias
            s = jnp.where(allowed, s, -jnp.inf)
            return jax.nn.softmax(s, axis=-1)

        p = softmax_map(qi[..., :DIFF_QK_DIM], k1) - lam * softmax_map(qi[..., DIFF_QK_DIM:], k2)
        return jnp.einsum('bhqk,bkhd->bqhd', p.astype(v.dtype), v)

    nb = seq // Q_BLOCK
    out = from_blocks(lax.map(block, (jnp.arange(nb, dtype=jnp.int32), to_blocks(q))))
    out = rms_norm(out, sub_gain) * (1.0 - lam_init)
    return out.reshape(out.shape[0], seq, -1)


def stick_breaking(q, k, v, out_gain):
    seq = q.shape[1]
    key_pos = jnp.arange(seq, dtype=jnp.int32)
    scale = HEAD_DIM ** -0.5

    def block(args):
        i, qi = args
        q_pos = i * Q_BLOCK + jnp.arange(Q_BLOCK, dtype=jnp.int32)
        strict = key_pos[None, :] < q_pos[:, None]
        z = jnp.einsum('bqhd,bkhd->bhqk', qi, k).astype(jnp.float32) * scale
        log_beta = jax.nn.log_sigmoid(z)
        log_keep = jnp.where(strict, jax.nn.log_sigmoid(-z), 0.0)
        after = lax.cumsum(log_keep, axis=log_keep.ndim - 1, reverse=True) - log_keep
        w = jnp.where(strict, jnp.exp(log_beta + after), 0.0)
        return jnp.einsum('bhqk,bkhd->bqhd', w.astype(v.dtype), v)

    nb = seq // Q_BLOCK
    out = from_blocks(lax.map(block, (jnp.arange(nb, dtype=jnp.int32), to_blocks(q))))
    out = rms_norm(out, out_gain)
    return out.reshape(out.shape[0], seq, -1)


def memory_attention(h, mem_n, w_q, w_kv, w_o):
    b, s, _ = h.shape
    q = (h @ w_q).reshape(b, s, N_HEADS_MEM, MEM_HEAD_DIM)
    kv = mem_n @ w_kv
    k = kv[..., :D_MODEL].reshape(b, N_MEM, N_HEADS_MEM, MEM_HEAD_DIM)
    v = kv[..., D_MODEL:].reshape(b, N_MEM, N_HEADS_MEM, MEM_HEAD_DIM)
    s_ = jnp.einsum('bqhd,bmhd->bhqm', q, k).astype(jnp.float32) * (MEM_HEAD_DIM ** -0.5)
    p = jax.nn.softmax(s_, axis=-1).astype(v.dtype)
    o = jnp.einsum('bhqm,bmhd->bqhd', p, v).reshape(b, s, D_MODEL)
    return o @ w_o


def setup_inputs(seed: int = 0) -> dict:
    key = jax.random.key(seed)
    ks = jax.random.split(key, 24)
    f32 = jnp.float32

    def w(k, shape, fan_in):
        return jax.random.normal(k, shape, f32) * (fan_in ** -0.5)

    def gain(k, shape):
        return 1.0 + 0.05 * jax.random.normal(k, shape, f32)

    return {
        "x": jax.random.normal(ks[0], (BATCH, SEQ, D_MODEL), f32),
        "mem": jax.random.normal(ks[1], (BATCH, N_MEM, D_MODEL), f32),
        "w_in": w(ks[2], (DEPTH, D_MODEL, IN_COLS), D_MODEL),
        "w_out": w(ks[3], (DEPTH, MIX_WIDTH, D_MODEL), MIX_WIDTH),
        "rel_bias": 0.5 * jax.random.normal(ks[4], (N_BUCKETS, N_HEADS_DIFF), f32),
        "lambda_q1": 0.1 * jax.random.normal(ks[5], (DEPTH, DIFF_QK_DIM), f32),
        "lambda_k1": 0.1 * jax.random.normal(ks[6], (DEPTH, DIFF_QK_DIM), f32),
        "lambda_q2": 0.1 * jax.random.normal(ks[7], (DEPTH, DIFF_QK_DIM), f32),
        "lambda_k2": 0.1 * jax.random.normal(ks[8], (DEPTH, DIFF_QK_DIM), f32),
        "diff_sub_gain": gain(ks[9], (DEPTH, HEAD_DIM)),
        "sb_gain": gain(ks[10], (DEPTH, HEAD_DIM)),
        "g_mix_pre": gain(ks[11], (DEPTH, D_MODEL)),
        "g_mix_post": gain(ks[12], (DEPTH, D_MODEL)),
        "w_mq": w(ks[13], (DEPTH, D_MODEL, D_MODEL), D_MODEL),
        "w_mkv": w(ks[14], (DEPTH, D_MODEL, 2 * D_MODEL), D_MODEL),
        "w_mo": w(ks[15], (DEPTH, D_MODEL, D_MODEL), D_MODEL),
        "g_mem_kv": gain(ks[16], (DEPTH, D_MODEL)),
        "g_mem_pre": gain(ks[17], (DEPTH, D_MODEL)),
        "g_mem_post": gain(ks[18], (DEPTH, D_MODEL)),
        "w_gate_up": w(ks[19], (DEPTH, D_MODEL, 2 * D_FF), D_MODEL),
        "w_down": w(ks[20], (DEPTH, D_FF, D_MODEL), D_FF),
        "g_ffn_pre": gain(ks[21], (DEPTH, D_MODEL)),
        "g_ffn_post": gain(ks[22], (DEPTH, D_MODEL)),
    }


def reference(x, mem, w_in, w_out, rel_bias, lambda_q1, lambda_k1, lambda_q2, lambda_k2,
              diff_sub_gain, sb_gain, g_mix_pre, g_mix_post, w_mq, w_mkv, w_mo,
              g_mem_kv, g_mem_pre, g_mem_post, w_gate_up, w_down, g_ffn_pre, g_ffn_post):
    b, s, _ = x.shape
    for l in range(DEPTH):
        lam_init = lambda_init(l)
        h = rms_norm(x, g_mix_pre[l])
        proj = h @ w_in[l]
        a_q, a_k, a_v, b_q, b_k, b_v = jnp.split(
            proj, [DIFF_WIDTH, 2 * DIFF_WIDTH, 3 * DIFF_WIDTH,
                   3 * DIFF_WIDTH + SB_WIDTH, 3 * DIFF_WIDTH + 2 * SB_WIDTH], axis=-1)
        a_q = a_q.reshape(b, s, N_HEADS_DIFF, HEAD_DIM)
        a_k = a_k.reshape(b, s, N_HEADS_DIFF, HEAD_DIM)
        a_v = a_v.reshape(b, s, N_HEADS_DIFF, HEAD_DIM)
        b_q = b_q.reshape(b, s, N_HEADS_SB, HEAD_DIM)
        b_k = b_k.reshape(b, s, N_HEADS_SB, HEAD_DIM)
        b_v = b_v.reshape(b, s, N_HEADS_SB, HEAD_DIM)
        lam = (jnp.exp(jnp.sum(lambda_q1[l].astype(jnp.float32) * lambda_k1[l].astype(jnp.float32)))
               - jnp.exp(jnp.sum(lambda_q2[l].astype(jnp.float32) * lambda_k2[l].astype(jnp.float32)))
               + lam_init)
        out_a = diff_attention(a_q, a_k, a_v, rel_bias, lam, diff_sub_gain[l], lam_init)
        out_b = stick_breaking(b_q, b_k, b_v, sb_gain[l])
        mix = jnp.concatenate([out_a, out_b], axis=-1) @ w_out[l]
        x = x + rms_norm(mix, g_mix_post[l])
        h = rms_norm(x, g_mem_pre[l])
        mem_n = rms_norm(mem, g_mem_kv[l])
        o = memory_attention(h, mem_n, w_mq[l], w_mkv[l], w_mo[l])
        x = x + rms_norm(o, g_mem_post[l])
        h = rms_norm(x, g_ffn_pre[l])
        gu = h @ w_gate_up[l]
        f = (jax.nn.silu(gu[..., :D_FF]) * gu[..., D_FF:]) @ w_down[l]
        x = x + rms_norm(f, g_ffn_post[l])
    return x
```

```python
import functools
import math

import numpy as np
import jax
import jax.numpy as jnp
from jax import lax
from jax.experimental import pallas as pl
from jax.experimental.pallas import tpu as pltpu

CHUNK = 64
HEAD_DIM = 128
DIFF_QK_DIM = HEAD_DIM // 2
N_HEADS_MEM = 4
N_BUCKETS = 32
MAX_DISTANCE = 128
EPS = 1e-6

V7X_VMEM_BYTES = 64 * 1024 * 1024
VMEM_CAP_BYTES = V7X_VMEM_BYTES - 8 * 1024 * 1024

F32 = jnp.float32
BF16 = jnp.bfloat16


def _vmem_limit(estimate_bytes):
    return int(min(VMEM_CAP_BYTES, max(estimate_bytes, 16 * 1024 * 1024)))


def _params(n_axes, vmem_bytes):
    return pltpu.CompilerParams(
        dimension_semantics=("arbitrary",) * n_axes,
        vmem_limit_bytes=_vmem_limit(vmem_bytes))


def _rms_scale(v):
    return lax.rsqrt(jnp.mean(v * v, axis=-1, keepdims=True) + EPS)


def _dot(a, b):
    return jnp.dot(a, b, preferred_element_type=F32)


def _dot_nt(a, b):
    return lax.dot_general(a, b, (((1,), (1,)), ((), ())), preferred_element_type=F32)


def _norm_matmul_kernel(x_ref, g_ref, w_ref, cs_ref, o_ref, h_ref):
    @pl.when(pl.program_id(1) == 0)
    def _():
        x = x_ref[...]
        h_ref[...] = (x * _rms_scale(x) * g_ref[...]).astype(BF16)

    acc = _dot(h_ref[...], w_ref[...])
    o_ref[...] = (acc * cs_ref[...]).astype(o_ref.dtype)


def _norm_matmul(x, gain, w, col_scale, *, tm, tn):
    m, d = x.shape
    n = w.shape[1]
    assert m % tm == 0 and n % tn == 0
    vmem = 2 * tm * d * 4 + tm * d * 2 + 2 * d * tn * 2 + 2 * tm * tn * 2 + 2 * tm * tn * 4
    return pl.pallas_call(
        _norm_matmul_kernel,
        grid=(m // tm, n // tn),
        in_specs=[
            pl.BlockSpec((tm, d), lambda i, j: (i, 0)),
            pl.BlockSpec((1, d), lambda i, j: (0, 0)),
            pl.BlockSpec((d, tn), lambda i, j: (0, j)),
            pl.BlockSpec((1, tn), lambda i, j: (0, j)),
        ],
        out_specs=pl.BlockSpec((tm, tn), lambda i, j: (i, j)),
        out_shape=jax.ShapeDtypeStruct((m, n), BF16),
        scratch_shapes=[pltpu.VMEM((tm, d), BF16)],
        compiler_params=_params(2, vmem),
        name="norm_matmul",
    )(x, gain.reshape(1, d), w, col_scale.reshape(1, n))


def _norm_swiglu_kernel(x_ref, g_ref, wg_ref, wu_ref, o_ref, h_ref):
    @pl.when(pl.program_id(1) == 0)
    def _():
        x = x_ref[...]
        h_ref[...] = (x * _rms_scale(x) * g_ref[...]).astype(BF16)

    h = h_ref[...]
    gate = _dot(h, wg_ref[...])
    up = _dot(h, wu_ref[...])
    o_ref[...] = (gate * (1.0 / (1.0 + jnp.exp(-gate))) * up).astype(o_ref.dtype)


def _norm_swiglu(x, gain, w_gate_up, *, tm, tn):
    m, d = x.shape
    f = w_gate_up.shape[1] // 2
    assert m % tm == 0 and f % tn == 0
    nf = f // tn
    vmem = 2 * tm * d * 4 + tm * d * 2 + 4 * d * tn * 2 + 2 * tm * tn * 2 + 4 * tm * tn * 4
    return pl.pallas_call(
        _norm_swiglu_kernel,
        grid=(m // tm, nf),
        in_specs=[
            pl.BlockSpec((tm, d), lambda i, j: (i, 0)),
            pl.BlockSpec((1, d), lambda i, j: (0, 0)),
            pl.BlockSpec((d, tn), lambda i, j: (0, j)),
            pl.BlockSpec((d, tn), lambda i, j: (0, j + nf)),
        ],
        out_specs=pl.BlockSpec((tm, tn), lambda i, j: (i, j)),
        out_shape=jax.ShapeDtypeStruct((m, f), BF16),
        scratch_shapes=[pltpu.VMEM((tm, d), BF16)],
        compiler_params=_params(2, vmem),
        name="norm_swiglu",
    )(x, gain.reshape(1, d), w_gate_up, w_gate_up)


def _matmul_norm_resid_kernel(l_ref, w_ref, r_ref, g_ref, o_ref, acc_ref):
    k = pl.program_id(1)

    @pl.when(k == 0)
    def _():
        acc_ref[...] = jnp.zeros_like(acc_ref)

    acc_ref[...] += _dot(l_ref[...], w_ref[...])

    @pl.when(k == pl.num_programs(1) - 1)
    def _():
        y = acc_ref[...]
        o_ref[...] = r_ref[...] + y * _rms_scale(y) * g_ref[...]


def _matmul_norm_resid(lhs, w, resid, gain, *, tm, tk):
    m, kdim = lhs.shape
    d = w.shape[1]
    assert m % tm == 0 and kdim % tk == 0
    vmem = 2 * tm * tk * 2 + 2 * tk * d * 2 + 4 * tm * d * 4 + 2 * tm * d * 4
    return pl.pallas_call(
        _matmul_norm_resid_kernel,
        grid=(m // tm, kdim // tk),
        in_specs=[
            pl.BlockSpec((tm, tk), lambda i, k: (i, k)),
            pl.BlockSpec((tk, d), lambda i, k: (k, 0)),
            pl.BlockSpec((tm, d), lambda i, k: (i, 0)),
            pl.BlockSpec((1, d), lambda i, k: (0, 0)),
        ],
        out_specs=pl.BlockSpec((tm, d), lambda i, k: (i, 0)),
        out_shape=jax.ShapeDtypeStruct((m, d), F32),
        scratch_shapes=[pltpu.VMEM((tm, d), F32)],
        compiler_params=_params(2, vmem),
        name="matmul_norm_resid",
    )(lhs, w, resid, gain.reshape(1, d))


def _t5_bucket(rel):
    nb = N_BUCKETS // 2
    ret = jnp.where(rel > 0, nb, 0)
    n = jnp.abs(rel)
    max_exact = nb // 2
    nf = jnp.maximum(n, 1).astype(F32)
    large = max_exact + (jnp.log(nf / max_exact) / math.log(MAX_DISTANCE / max_exact)
                         * (nb - max_exact)).astype(jnp.int32)
    large = jnp.minimum(large, nb - 1)
    return ret + jnp.where(n < max_exact, n, large)


def _far_bias_is_constant(t, seq):
    n = np.arange(t + 1, max(seq, t + 2), dtype=np.float64)
    nb, me = N_BUCKETS // 2, N_BUCKETS // 4
    b = np.minimum(me + (np.log(n / me) / math.log(MAX_DISTANCE / me) * (nb - me)).astype(np.int64), nb - 1)
    return bool(np.all(b == nb - 1)) and (t + 1) >= 2 * MAX_DISTANCE


def _diff_bias_tiles(rel_bias, t):
    r = jnp.arange(t, dtype=jnp.int32)[:, None]
    c = jnp.arange(t, dtype=jnp.int32)[None, :]
    tiles = []
    for d in range(3):
        rel = c - r - d * t
        b = rel_bias[_t5_bucket(rel)].astype(F32)
        if d == 0:
            allowed = (c // CHUNK) <= (r // CHUNK)
            b = jnp.where(allowed[:, :, None], b, -jnp.inf)
        tiles.append(jnp.transpose(b, (2, 0, 1)))
    tiles = jnp.stack(tiles, axis=1)
    return jnp.concatenate([tiles, tiles], axis=2)


def _diff_attn_kernel(q_ref, k_ref, v_ref, bias_ref, lamv_ref, gain_ref, o_ref,
                      m_ref, l_ref, acc_ref, *, t, lam_init):
    qi = pl.program_id(2)
    q = q_ref[...].astype(F32)
    lane = lax.broadcasted_iota(jnp.int32, q.shape, 1)
    first = lane < DIFF_QK_DIM
    qq = jnp.concatenate([jnp.where(first, q, 0.0), jnp.where(first, 0.0, q)], axis=0).astype(BF16)

    m_ref[...] = jnp.full_like(m_ref, -jnp.inf)
    l_ref[...] = jnp.zeros_like(l_ref)
    acc_ref[...] = jnp.zeros_like(acc_ref)

    def body(ki, carry):
        off = pl.multiple_of(ki * t, t)
        k = k_ref[pl.ds(off, t), :]
        v = v_ref[pl.ds(off, t), :]
        s = _dot_nt(qq, k) + bias_ref[0, jnp.minimum(qi - ki, 2)]
        m_prev = m_ref[...]
        m_new = jnp.maximum(m_prev, jnp.max(s, axis=-1, keepdims=True))
        alpha = jnp.exp(m_prev - m_new)
        p = jnp.exp(s - m_new)
        l_ref[...] = alpha * l_ref[...] + jnp.sum(p, axis=-1, keepdims=True)
        acc_ref[...] = alpha * acc_ref[...] + _dot(p.astype(BF16), v)
        m_ref[...] = m_new
        return carry

    lax.fori_loop(0, qi + 1, body, 0)

    lv = lamv_ref[...]
    lam = (jnp.exp(jnp.sum(lv[0:1] * lv[1:2], axis=-1, keepdims=True))
           - jnp.exp(jnp.sum(lv[2:3] * lv[3:4], axis=-1, keepdims=True)) + lam_init)
    o = acc_ref[...] / l_ref[...]
    out = o[:t] - lam * o[t:]
    y = out * _rms_scale(out) * gain_ref[...] * (1.0 - lam_init)
    o_ref[...] = y.astype(o_ref.dtype)


def _diff_attn(proj, bias_tiles, lam_vecs, sub_gain, *, batch, seq, n_heads, col0, t, lam_init):
    nq = seq // t
    hd = HEAD_DIM
    vmem = (4 * t * hd * 2 + 4 * seq * hd * 2 + 2 * 3 * 2 * t * t * 4
            + 2 * t * hd * 4 + 2 * 2 * t * 128 * 4 + 6 * 2 * t * t * 4)
    kern = functools.partial(_diff_attn_kernel, t=t, lam_init=lam_init)
    return pl.pallas_call(
        kern,
        grid=(batch, n_heads, nq),
        in_specs=[
            pl.BlockSpec((t, hd), lambda b, h, i: (b * nq + i, col0 + h)),
            pl.BlockSpec((seq, hd), lambda b, h, i: (b, col0 + n_heads + h)),
            pl.BlockSpec((seq, hd), lambda b, h, i: (b, col0 + 2 * n_heads + h)),
            pl.BlockSpec((1, 3, 2 * t, t), lambda b, h, i: (h, 0, 0, 0)),
            pl.BlockSpec((4, DIFF_QK_DIM), lambda b, h, i: (0, 0)),
            pl.BlockSpec((1, hd), lambda b, h, i: (0, 0)),
        ],
        out_specs=pl.BlockSpec((t, hd), lambda b, h, i: (b * nq + i, h)),
        out_shape=jax.ShapeDtypeStruct((batch * seq, n_heads * hd), BF16),
        scratch_shapes=[
            pltpu.VMEM((2 * t, 1), F32),
            pltpu.VMEM((2 * t, 1), F32),
            pltpu.VMEM((2 * t, hd), F32),
        ],
        compiler_params=_params(3, vmem),
        name="diff_attn",
    )(proj, proj, proj, bias_tiles, lam_vecs, sub_gain.reshape(1, hd))


def _sb_attn_kernel(q_ref, k_ref, v_ref, tri_ref, gain_ref, o_ref, carry_ref, acc_ref, *, t):
    qi = pl.program_id(2)
    q = q_ref[...]

    def tile(ki, diag):
        off = pl.multiple_of(ki * t, t)
        k = k_ref[pl.ds(off, t), :]
        v = v_ref[pl.ds(off, t), :]
        z = _dot_nt(q, k)
        soft = jnp.log(1.0 + jnp.exp(-jnp.abs(z)))
        log_beta = jnp.minimum(z, 0.0) - soft
        log_keep = log_beta - z
        if diag:
            row = lax.broadcasted_iota(jnp.int32, z.shape, 0)
            col = lax.broadcasted_iota(jnp.int32, z.shape, 1)
            strict = col < row
            log_keep = jnp.where(strict, log_keep, 0.0)
        hi = log_keep.astype(BF16)
        lo = (log_keep - hi.astype(F32)).astype(BF16)
        tri = tri_ref[...]
        after = _dot(hi, tri) + _dot(lo, tri)
        row_sum = jnp.sum(log_keep, axis=-1, keepdims=True)
        if diag:
            w = jnp.where(strict, jnp.exp(log_beta + after), 0.0)
            acc_ref[...] = _dot(w.astype(BF16), v)
            carry_ref[...] = row_sum
        else:
            w = jnp.exp(log_beta + after + carry_ref[...])
            acc_ref[...] += _dot(w.astype(BF16), v)
            carry_ref[...] += row_sum

    tile(qi, True)

    def body(j, carry):
        tile(qi - 1 - j, False)
        return carry

    lax.fori_loop(0, qi, body, 0)

    out = acc_ref[...]
    o_ref[...] = (out * _rms_scale(out) * gain_ref[...]).astype(o_ref.dtype)


def _sb_attn(proj, out_gain, *, batch, seq, n_heads, col0, t):
    nq = seq // t
    hd = HEAD_DIM
    tri = jnp.asarray(np.tril(np.ones((t, t), np.float32), -1), BF16)
    vmem = (4 * t * hd * 2 + 4 * seq * hd * 2 + 2 * t * t * 2
            + t * hd * 4 + t * 128 * 4 + 10 * t * t * 4)
    kern = functools.partial(_sb_attn_kernel, t=t)
    return pl.pallas_call(
        kern,
        grid=(batch, n_heads, nq),
        in_specs=[
            pl.BlockSpec((t, hd), lambda b, h, i: (b * nq + i, col0 + h)),
            pl.BlockSpec((seq, hd), lambda b, h, i: (b, col0 + n_heads + h)),
            pl.BlockSpec((seq, hd), lambda b, h, i: (b, col0 + 2 * n_heads + h)),
            pl.BlockSpec((t, t), lambda b, h, i: (0, 0)),
            pl.BlockSpec((1, hd), lambda b, h, i: (0, 0)),
        ],
        out_specs=pl.BlockSpec((t, hd), lambda b, h, i: (b * nq + i, h)),
        out_shape=jax.ShapeDtypeStruct((batch * seq, n_heads * hd), BF16),
        scratch_shapes=[
            pltpu.VMEM((t, 1), F32),
            pltpu.VMEM((t, hd), F32),
        ],
        compiler_params=_params(3, vmem),
        name="sb_attn",
    )(proj, proj, proj, tri, out_gain.reshape(1, hd))


def _mem_attn_kernel(q_ref, k_ref, v_ref, o_ref, *, n_heads, dh):
    scale = dh ** -0.5
    for h in range(n_heads):
        cols = slice(h * dh, (h + 1) * dh)
        s = _dot_nt(q_ref[:, cols], k_ref[:, cols]) * scale
        p = jnp.exp(s - jnp.max(s, axis=-1, keepdims=True))
        denom = jnp.sum(p, axis=-1, keepdims=True)
        o = _dot(p.astype(BF16), v_ref[:, cols]) / denom
        o_ref[:, cols] = o.astype(o_ref.dtype)


def _mem_attn(q, kv, *, batch, seq, n_mem, tq):
    d = q.shape[1]
    nq = seq // tq
    vmem = 4 * tq * d * 2 + 4 * n_mem * d * 2 + 8 * tq * n_mem * 4 + 2 * tq * d * 4
    kern = functools.partial(_mem_attn_kernel, n_heads=N_HEADS_MEM, dh=d // N_HEADS_MEM)
    return pl.pallas_call(
        kern,
        grid=(batch, nq),
        in_specs=[
            pl.BlockSpec((tq, d), lambda b, i: (b * nq + i, 0)),
            pl.BlockSpec((n_mem, d), lambda b, i: (b, 0)),
            pl.BlockSpec((n_mem, d), lambda b, i: (b, 1)),
        ],
        out_specs=pl.BlockSpec((tq, d), lambda b, i: (b * nq + i, 0)),
        out_shape=jax.ShapeDtypeStruct((batch * seq, d), BF16),
        compiler_params=_params(2, vmem),
        name="mem_attn",
    )(q, kv, kv)


def _lambda_init(layer_idx):
    return 0.8 - 0.6 * math.exp(-0.3 * layer_idx)


def _pick(n, prefs):
    for p in prefs:
        if n % p == 0:
            return p
    return n


def kernel(x, mem, w_in, w_out, rel_bias, lambda_q1, lambda_k1, lambda_q2, lambda_k2, diff_sub_gain, sb_gain, g_mix_pre, g_mix_post, w_mq, w_mkv, w_mo, g_mem_kv, g_mem_pre, g_mem_post, w_gate_up, w_down, g_ffn_pre, g_ffn_post):
    batch, seq, d_model = x.shape
    n_mem = mem.shape[1]
    depth = w_in.shape[0]
    width = w_out.shape[1] // 2
    n_heads = width // HEAD_DIM
    d_ff = w_down.shape[1]
    t = _pick(seq, (256, 128))
    assert w_in.shape[2] == 6 * width and t % CHUNK == 0
    assert _far_bias_is_constant(t, seq)

    rows = batch * seq
    tm = _pick(rows, (1024, 512, 256, 128))
    xf = x.reshape(rows, d_model)
    memf = mem.reshape(batch * n_mem, d_model)
    ones = lambda n: jnp.ones((n,), F32)

    col_scale = jnp.concatenate([
        jnp.full((width,), DIFF_QK_DIM ** -0.5, F32), ones(2 * width),
        jnp.full((width,), HEAD_DIM ** -0.5, F32), ones(2 * width)])

    for l in range(depth):
        lam_init = _lambda_init(l)
        proj = _norm_matmul(xf, g_mix_pre[l], w_in[l].astype(BF16), col_scale,
                            tm=tm, tn=_pick(6 * width, (1024, 512, 256, 128)))
        lam_vecs = jnp.stack([lambda_q1[l], lambda_k1[l], lambda_q2[l], lambda_k2[l]]).astype(F32)
        out_a = _diff_attn(proj, _diff_bias_tiles(rel_bias, t), lam_vecs, diff_sub_gain[l],
                           batch=batch, seq=seq, n_heads=n_heads, col0=0, t=t, lam_init=lam_init)
        out_b = _sb_attn(proj, sb_gain[l], batch=batch, seq=seq, n_heads=n_heads,
                         col0=3 * n_heads, t=t)
        mix_in = jnp.concatenate([out_a, out_b], axis=-1)
        tm_r = _pick(rows, (512, 256, 128))
        xf = _matmul_norm_resid(mix_in, w_out[l].astype(BF16), xf, g_mix_post[l],
                                tm=tm_r, tk=_pick(2 * width, (1024, 512, 256, 128)))
        q_mem = _norm_matmul(xf, g_mem_pre[l], w_mq[l].astype(BF16), ones(d_model),
                             tm=tm, tn=_pick(d_model, (1024, 512, 256, 128)))
        kv = _norm_matmul(memf, g_mem_kv[l], w_mkv[l].astype(BF16), ones(2 * d_model),
                          tm=_pick(batch * n_mem, (1024, 512, 256, 128)),
                          tn=_pick(2 * d_model, (1024, 512, 256, 128)))
        o_mem = _mem_attn(q_mem, kv, batch=batch, seq=seq, n_mem=n_mem,
                          tq=_pick(seq, (512, 256, 128)))
        xf = _matmul_norm_resid(o_mem, w_mo[l].astype(BF16), xf, g_mem_post[l],
                                tm=tm_r, tk=_pick(d_model, (1024, 512, 256, 128)))
        act = _norm_swiglu(xf, g_ffn_pre[l], w_gate_up[l].astype(BF16),
                           tm=tm, tn=_pick(d_ff, (512, 256, 128)))
        xf = _matmul_norm_resid(act, w_down[l].astype(BF16), xf, g_ffn_post[l],
                                tm=tm_r, tk=_pick(d_ff, (1408, 512, 256, 128)))
    return xf.reshape(batch, seq, d_model)
```

```python
import functools
import math

import numpy as np
import jax
import jax.numpy as jnp
from jax import lax
from jax.experimental import pallas as pl
from jax.experimental.pallas import tpu as pltpu

CHUNK = 64
HEAD_DIM = 128
DIFF_QK_DIM = HEAD_DIM // 2
N_HEADS_MEM = 4
N_BUCKETS = 32
MAX_DISTANCE = 128
EPS = 1e-6

V7X_VMEM_BYTES = 64 * 1024 * 1024
LANES = 128
MXU_DIM = 256
VMEM_CAP_BYTES = V7X_VMEM_BYTES - 8 * 1024 * 1024

F32 = jnp.float32
BF16 = jnp.bfloat16


def _vmem_limit(estimate_bytes):
    return int(min(VMEM_CAP_BYTES, max(estimate_bytes, 16 * 1024 * 1024)))


def _params(n_axes, vmem_bytes):
    return pltpu.CompilerParams(
        dimension_semantics=("arbitrary",) * n_axes,
        vmem_limit_bytes=_vmem_limit(vmem_bytes))


def _rms_scale(v):
    return lax.rsqrt(jnp.mean(v * v, axis=-1, keepdims=True) + EPS)


def _dot(a, b):
    return jnp.dot(a, b, preferred_element_type=F32)


def _dot_nt(a, b):
    return lax.dot_general(a, b, (((1,), (1,)), ((), ())), preferred_element_type=F32)


def _norm_matmul_kernel(x_ref, g_ref, w_ref, cs_ref, o_ref, h_ref):
    @pl.when(pl.program_id(1) == 0)
    def _():
        x = x_ref[...]
        h_ref[...] = (x * _rms_scale(x) * g_ref[...]).astype(BF16)

    acc = _dot(h_ref[...], w_ref[...])
    o_ref[...] = (acc * cs_ref[...]).astype(o_ref.dtype)


def _norm_matmul(x, gain, w, col_scale, *, tm, tn):
    m, d = x.shape
    n = w.shape[1]
    assert m % tm == 0 and n % tn == 0
    vmem = 2 * tm * d * 4 + tm * d * 2 + 2 * d * tn * 2 + 2 * tm * tn * 2 + 2 * tm * tn * 4
    return pl.pallas_call(
        _norm_matmul_kernel,
        grid=(m // tm, n // tn),
        in_specs=[
            pl.BlockSpec((tm, d), lambda i, j: (i, 0)),
            pl.BlockSpec((1, d), lambda i, j: (0, 0)),
            pl.BlockSpec((d, tn), lambda i, j: (0, j)),
            pl.BlockSpec((1, tn), lambda i, j: (0, j)),
        ],
        out_specs=pl.BlockSpec((tm, tn), lambda i, j: (i, j)),
        out_shape=jax.ShapeDtypeStruct((m, n), BF16),
        scratch_shapes=[pltpu.VMEM((tm, d), BF16)],
        compiler_params=_params(2, vmem),
        name="norm_matmul",
    )(x, gain.reshape(1, d), w, col_scale.reshape(1, n))


def _norm_swiglu_kernel(x_ref, g_ref, wg_ref, wu_ref, o_ref, h_ref):
    @pl.when(pl.program_id(1) == 0)
    def _():
        x = x_ref[...]
        h_ref[...] = (x * _rms_scale(x) * g_ref[...]).astype(BF16)

    h = h_ref[...]
    gate = _dot(h, wg_ref[...])
    up = _dot(h, wu_ref[...])
    o_ref[...] = (gate * (1.0 / (1.0 + jnp.exp(-gate))) * up).astype(o_ref.dtype)


def _norm_swiglu(x, gain, w_gate_up, *, tm, tn):
    m, d = x.shape
    f = w_gate_up.shape[1] // 2
    assert m % tm == 0 and f % tn == 0
    nf = f // tn
    vmem = 2 * tm * d * 4 + tm * d * 2 + 4 * d * tn * 2 + 2 * tm * tn * 2 + 4 * tm * tn * 4
    return pl.pallas_call(
        _norm_swiglu_kernel,
        grid=(m // tm, nf),
        in_specs=[
            pl.BlockSpec((tm, d), lambda i, j: (i, 0)),
            pl.BlockSpec((1, d), lambda i, j: (0, 0)),
            pl.BlockSpec((d, tn), lambda i, j: (0, j)),
            pl.BlockSpec((d, tn), lambda i, j: (0, j + nf)),
        ],
        out_specs=pl.BlockSpec((tm, tn), lambda i, j: (i, j)),
        out_shape=jax.ShapeDtypeStruct((m, f), BF16),
        scratch_shapes=[pltpu.VMEM((tm, d), BF16)],
        compiler_params=_params(2, vmem),
        name="norm_swiglu",
    )(x, gain.reshape(1, d), w_gate_up, w_gate_up)


def _matmul_norm_resid_kernel(l_ref, w_ref, r_ref, g_ref, o_ref, acc_ref):
    k = pl.program_id(1)

    @pl.when(k == 0)
    def _():
        acc_ref[...] = jnp.zeros_like(acc_ref)

    acc_ref[...] += _dot(l_ref[...], w_ref[...])

    @pl.when(k == pl.num_programs(1) - 1)
    def _():
        y = acc_ref[...]
        o_ref[...] = r_ref[...] + y * _rms_scale(y) * g_ref[...]


def _matmul_norm_resid(lhs, w, resid, gain, *, tm, tk):
    m, kdim = lhs.shape
    d = w.shape[1]
    assert m % tm == 0 and kdim % tk == 0
    vmem = 2 * tm * tk * 2 + 2 * tk * d * 2 + 4 * tm * d * 4 + 2 * tm * d * 4
    return pl.pallas_call(
        _matmul_norm_resid_kernel,
        grid=(m // tm, kdim // tk),
        in_specs=[
            pl.BlockSpec((tm, tk), lambda i, k: (i, k)),
            pl.BlockSpec((tk, d), lambda i, k: (k, 0)),
            pl.BlockSpec((tm, d), lambda i, k: (i, 0)),
            pl.BlockSpec((1, d), lambda i, k: (0, 0)),
        ],
        out_specs=pl.BlockSpec((tm, d), lambda i, k: (i, 0)),
        out_shape=jax.ShapeDtypeStruct((m, d), F32),
        scratch_shapes=[pltpu.VMEM((tm, d), F32)],
        compiler_params=_params(2, vmem),
        name="matmul_norm_resid",
    )(lhs, w, resid, gain.reshape(1, d))


def _t5_bucket(rel):
    nb = N_BUCKETS // 2
    ret = jnp.where(rel > 0, nb, 0)
    n = jnp.abs(rel)
    max_exact = nb // 2
    nf = jnp.maximum(n, 1).astype(F32)
    large = max_exact + (jnp.log(nf / max_exact) / math.log(MAX_DISTANCE / max_exact)
                         * (nb - max_exact)).astype(jnp.int32)
    large = jnp.minimum(large, nb - 1)
    return ret + jnp.where(n < max_exact, n, large)


def _far_bias_is_constant(t, seq):
    n = np.arange(t + 1, max(seq, t + 2), dtype=np.float64)
    nb, me = N_BUCKETS // 2, N_BUCKETS // 4
    b = np.minimum(me + (np.log(n / me) / math.log(MAX_DISTANCE / me) * (nb - me)).astype(np.int64), nb - 1)
    return bool(np.all(b == nb - 1)) and (t + 1) >= 2 * MAX_DISTANCE


def _diff_bias_tiles(rel_bias, t):
    r = jnp.arange(t, dtype=jnp.int32)[:, None]
    c = jnp.arange(t, dtype=jnp.int32)[None, :]
    tiles = []
    for d in range(3):
        rel = c - r - d * t
        b = rel_bias[_t5_bucket(rel)].astype(F32)
        if d == 0:
            allowed = (c // CHUNK) <= (r // CHUNK)
            b = jnp.where(allowed[:, :, None], b, -jnp.inf)
        tiles.append(jnp.transpose(b, (2, 0, 1)))
    return jnp.stack(tiles, axis=1)


def _widen(col, width):
    return jnp.concatenate([col] * (width // LANES), axis=1)


def _diff_attn_kernel(q_ref, k_ref, v_ref, bias_ref, lamv_ref, gain_ref, o_ref,
                      m_ref, l_ref, acc_ref, *, t, lam_init):
    qi = pl.program_id(2)
    nc = t // LANES
    q = q_ref[...].astype(F32)
    lane = lax.broadcasted_iota(jnp.int32, q.shape, 1)
    first = lane < DIFF_QK_DIM
    qq = jnp.concatenate([jnp.where(first, q, 0.0), jnp.where(first, 0.0, q)], axis=0).astype(BF16)

    m_ref[...] = jnp.full_like(m_ref, -jnp.inf)
    l_ref[...] = jnp.zeros_like(l_ref)
    acc_ref[...] = jnp.zeros_like(acc_ref)

    def body(ki, carry):
        off = pl.multiple_of(ki * t, t)
        k = k_ref[pl.ds(off, t), :]
        v = v_ref[pl.ds(off, t), :]
        bias = bias_ref[0, jnp.minimum(qi - ki, 2)]
        s_all = _dot_nt(qq, k)
        for half in range(2):
            rows = slice(half * t, (half + 1) * t)
            s = s_all[rows] + bias
            mx = s[:, :LANES]
            for c in range(1, nc):
                mx = jnp.maximum(mx, s[:, c * LANES:(c + 1) * LANES])
            m_prev = m_ref[rows]
            m_new = jnp.maximum(m_prev, jnp.max(mx, axis=-1, keepdims=True))
            alpha = jnp.exp(m_prev - m_new)
            l_part = None
            p = []
            for c in range(nc):
                pc = jnp.exp(s[:, c * LANES:(c + 1) * LANES] - m_new)
                l_part = pc if l_part is None else l_part + pc
                p.append(pc.astype(BF16))
            l_ref[rows] = alpha * l_ref[rows] + l_part
            acc_ref[rows] = alpha * acc_ref[rows] + _dot(jnp.concatenate(p, axis=1), v)
            m_ref[rows] = m_new
        return carry

    lax.fori_loop(0, qi + 1, body, 0)

    lv = lamv_ref[...]
    lam = (jnp.exp(jnp.sum(lv[0:1] * lv[1:2], axis=-1, keepdims=True))
           - jnp.exp(jnp.sum(lv[2:3] * lv[3:4], axis=-1, keepdims=True)) + lam_init)
    o = acc_ref[...] / jnp.sum(l_ref[...], axis=-1, keepdims=True)
    out = o[:t] - lam * o[t:]
    y = out * _rms_scale(out) * gain_ref[...] * (1.0 - lam_init)
    o_ref[...] = y.astype(o_ref.dtype)


def _diff_attn(proj, bias_tiles, lam_vecs, sub_gain, *, batch, seq, n_heads, col0, t, lam_init):
    nq = seq // t
    hd = HEAD_DIM
    vmem = (4 * t * hd * 2 + 4 * seq * hd * 2 + 2 * 3 * t * t * 4
            + 2 * t * hd * 4 + 3 * 2 * t * LANES * 4 + 8 * 2 * t * t * 4)
    kern = functools.partial(_diff_attn_kernel, t=t, lam_init=lam_init)
    return pl.pallas_call(
        kern,
        grid=(batch, n_heads, nq),
        in_specs=[
            pl.BlockSpec((t, hd), lambda b, h, i: (b * nq + i, col0 + h)),
            pl.BlockSpec((seq, hd), lambda b, h, i: (b, col0 + n_heads + h)),
            pl.BlockSpec((seq, hd), lambda b, h, i: (b, col0 + 2 * n_heads + h)),
            pl.BlockSpec((1, 3, t, t), lambda b, h, i: (h, 0, 0, 0)),
            pl.BlockSpec((4, DIFF_QK_DIM), lambda b, h, i: (0, 0)),
            pl.BlockSpec((1, hd), lambda b, h, i: (0, 0)),
        ],
        out_specs=pl.BlockSpec((t, hd), lambda b, h, i: (b * nq + i, h)),
        out_shape=jax.ShapeDtypeStruct((batch * seq, n_heads * hd), BF16),
        scratch_shapes=[
            pltpu.VMEM((2 * t, LANES), F32),
            pltpu.VMEM((2 * t, LANES), F32),
            pltpu.VMEM((2 * t, hd), F32),
        ],
        compiler_params=_params(3, vmem),
        name="diff_attn",
    )(proj, proj, proj, bias_tiles, lam_vecs, sub_gain.reshape(1, hd))


def _sb_attn_kernel(q_ref, k_ref, v_ref, tri_ref, gain_ref, o_ref, carry_ref, acc_ref, *, t, cw):
    qi = pl.program_id(2)
    q = q_ref[...]

    def tile(ki, diag):
        off = pl.multiple_of(ki * t, t)
        k = k_ref[pl.ds(off, t), :]
        v = v_ref[pl.ds(off, t), :]
        z = _dot_nt(q, k)
        soft = jnp.log(1.0 + jnp.exp(-jnp.abs(z)))
        log_beta = jnp.minimum(z, 0.0) - soft
        log_keep = log_beta - z
        if diag:
            row = lax.broadcasted_iota(jnp.int32, z.shape, 0)
            col = lax.broadcasted_iota(jnp.int32, z.shape, 1)
            strict = col < row
            log_keep = jnp.where(strict, log_keep, 0.0)
        tri = tri_ref[...]
        run = None if diag else carry_ref[...]
        after = [None] * (t // cw)
        for c in reversed(range(t // cw)):
            lk = log_keep[:, c * cw:(c + 1) * cw]
            hi = lk.astype(BF16)
            lo = (lk - hi.astype(F32)).astype(BF16)
            cs = _dot(hi, tri) + _dot(lo, tri)
            after[c] = cs if run is None else cs + _widen(run, cw)
            row_sum = jnp.sum(lk, axis=-1, keepdims=True)
            run = jnp.broadcast_to(row_sum, (t, LANES)) if run is None else run + row_sum
        w = jnp.exp(log_beta + jnp.concatenate(after, axis=1))
        if diag:
            w = jnp.where(strict, w, 0.0)
            acc_ref[...] = _dot(w.astype(BF16), v)
        else:
            acc_ref[...] += _dot(w.astype(BF16), v)
        carry_ref[...] = run

    tile(qi, True)

    def body(j, carry):
        tile(qi - 1 - j, False)
        return carry

    lax.fori_loop(0, qi, body, 0)

    out = acc_ref[...]
    o_ref[...] = (out * _rms_scale(out) * gain_ref[...]).astype(o_ref.dtype)


def _sb_attn(proj, out_gain, *, batch, seq, n_heads, col0, t):
    nq = seq // t
    hd = HEAD_DIM
    cw = min(t, MXU_DIM)
    tri = jnp.asarray(np.tril(np.ones((cw, cw), np.float32), -1), BF16)
    vmem = (4 * t * hd * 2 + 4 * seq * hd * 2 + 2 * cw * cw * 2
            + t * hd * 4 + t * LANES * 4 + 10 * t * t * 4)
    kern = functools.partial(_sb_attn_kernel, t=t, cw=cw)
    return pl.pallas_call(
        kern,
        grid=(batch, n_heads, nq),
        in_specs=[
            pl.BlockSpec((t, hd), lambda b, h, i: (b * nq + i, col0 + h)),
            pl.BlockSpec((seq, hd), lambda b, h, i: (b, col0 + n_heads + h)),
            pl.BlockSpec((seq, hd), lambda b, h, i: (b, col0 + 2 * n_heads + h)),
            pl.BlockSpec((cw, cw), lambda b, h, i: (0, 0)),
            pl.BlockSpec((1, hd), lambda b, h, i: (0, 0)),
        ],
        out_specs=pl.BlockSpec((t, hd), lambda b, h, i: (b * nq + i, h)),
        out_shape=jax.ShapeDtypeStruct((batch * seq, n_heads * hd), BF16),
        scratch_shapes=[
            pltpu.VMEM((t, LANES), F32),
            pltpu.VMEM((t, hd), F32),
        ],
        compiler_params=_params(3, vmem),
        name="sb_attn",
    )(proj, proj, proj, tri, out_gain.reshape(1, hd))


def _mem_attn_kernel(q_ref, k_ref, v_ref, o_ref, *, n_heads, dh):
    scale = dh ** -0.5
    for h in range(n_heads):
        cols = slice(h * dh, (h + 1) * dh)
        s = _dot_nt(q_ref[:, cols], k_ref[:, cols]) * scale
        p = jnp.exp(s - jnp.max(s, axis=-1, keepdims=True))
        denom = jnp.sum(p, axis=-1, keepdims=True)
        o = _dot(p.astype(BF16), v_ref[:, cols]) / denom
        o_ref[:, cols] = o.astype(o_ref.dtype)


def _mem_attn(q, kv, *, batch, seq, n_mem, tq):
    d = q.shape[1]
    nq = seq // tq
    vmem = 4 * tq * d * 2 + 4 * n_mem * d * 2 + 8 * tq * n_mem * 4 + 2 * tq * d * 4
    kern = functools.partial(_mem_attn_kernel, n_heads=N_HEADS_MEM, dh=d // N_HEADS_MEM)
    return pl.pallas_call(
        kern,
        grid=(batch, nq),
        in_specs=[
            pl.BlockSpec((tq, d), lambda b, i: (b * nq + i, 0)),
            pl.BlockSpec((n_mem, d), lambda b, i: (b, 0)),
            pl.BlockSpec((n_mem, d), lambda b, i: (b, 1)),
        ],
        out_specs=pl.BlockSpec((tq, d), lambda b, i: (b * nq + i, 0)),
        out_shape=jax.ShapeDtypeStruct((batch * seq, d), BF16),
        compiler_params=_params(2, vmem),
        name="mem_attn",
    )(q, kv, kv)


def _lambda_init(layer_idx):
    return 0.8 - 0.6 * math.exp(-0.3 * layer_idx)


def _pick(n, prefs):
    for p in prefs:
        if n % p == 0:
            return p
    return n


def kernel(x, mem, w_in, w_out, rel_bias, lambda_q1, lambda_k1, lambda_q2, lambda_k2, diff_sub_gain, sb_gain, g_mix_pre, g_mix_post, w_mq, w_mkv, w_mo, g_mem_kv, g_mem_pre, g_mem_post, w_gate_up, w_down, g_ffn_pre, g_ffn_post):
    batch, seq, d_model = x.shape
    n_mem = mem.shape[1]
    depth = w_in.shape[0]
    width = w_out.shape[1] // 2
    n_heads = width // HEAD_DIM
    d_ff = w_down.shape[1]
    t = _pick(seq, (512, 256))
    assert w_in.shape[2] == 6 * width and t % CHUNK == 0
    assert _far_bias_is_constant(t, seq)

    rows = batch * seq
    tm = _pick(rows, (1024, 512, 256, 128))
    xf = x.reshape(rows, d_model)
    memf = mem.reshape(batch * n_mem, d_model)
    ones = lambda n: jnp.ones((n,), F32)

    col_scale = jnp.concatenate([
        jnp.full((width,), DIFF_QK_DIM ** -0.5, F32), ones(2 * width),
        jnp.full((width,), HEAD_DIM ** -0.5, F32), ones(2 * width)])

    for l in range(depth):
        lam_init = _lambda_init(l)
        proj = _norm_matmul(xf, g_mix_pre[l], w_in[l].astype(BF16), col_scale,
                            tm=tm, tn=_pick(6 * width, (1024, 512, 256, 128)))
        lam_vecs = jnp.stack([lambda_q1[l], lambda_k1[l], lambda_q2[l], lambda_k2[l]]).astype(F32)
        out_a = _diff_attn(proj, _diff_bias_tiles(rel_bias, t), lam_vecs, diff_sub_gain[l],
                           batch=batch, seq=seq, n_heads=n_heads, col0=0, t=t, lam_init=lam_init)
        out_b = _sb_attn(proj, sb_gain[l], batch=batch, seq=seq, n_heads=n_heads,
                         col0=3 * n_heads, t=t)
        mix_in = jnp.concatenate([out_a, out_b], axis=-1)
        tm_r = _pick(rows, (512, 256, 128))
        xf = _matmul_norm_resid(mix_in, w_out[l].astype(BF16), xf, g_mix_post[l],
                                tm=tm_r, tk=_pick(2 * width, (1024, 512, 256, 128)))
        q_mem = _norm_matmul(xf, g_mem_pre[l], w_mq[l].astype(BF16), ones(d_model),
                             tm=tm, tn=_pick(d_model, (1024, 512, 256, 128)))
        kv = _norm_matmul(memf, g_mem_kv[l], w_mkv[l].astype(BF16), ones(2 * d_model),
                          tm=_pick(batch * n_mem, (1024, 512, 256, 128)),
                          tn=_pick(2 * d_model, (1024, 512, 256, 128)))
        o_mem = _mem_attn(q_mem, kv, batch=batch, seq=seq, n_mem=n_mem,
                          tq=_pick(seq, (512, 256, 128)))
        xf = _matmul_norm_resid(o_mem, w_mo[l].astype(BF16), xf, g_mem_post[l],
                                tm=tm_r, tk=_pick(d_model, (1024, 512, 256, 128)))
        act = _norm_swiglu(xf, g_ffn_pre[l], w_gate_up[l].astype(BF16),
                           tm=tm, tn=_pick(d_ff, (512, 256, 128)))
        xf = _matmul_norm_resid(act, w_down[l].astype(BF16), xf, g_ffn_post[l],
                                tm=tm_r, tk=_pick(d_ff, (1408, 512, 256, 128)))
    return xf.reshape(batch, seq, d_model)
```

```python
import functools
import math

import numpy as np
import jax
import jax.numpy as jnp
from jax import lax
from jax.experimental import pallas as pl
from jax.experimental.pallas import tpu as pltpu

CHUNK = 64
HEAD_DIM = 128
DIFF_QK_DIM = HEAD_DIM // 2
N_HEADS_MEM = 4
N_BUCKETS = 32
MAX_DISTANCE = 128
EPS = 1e-6
LOG2E = math.log2(math.e)

V7X_VMEM_BYTES = 64 * 1024 * 1024
LANES = 128
MXU_DIM = 256
HEADS_PER_STEP = 2
VMEM_CAP_BYTES = V7X_VMEM_BYTES - 8 * 1024 * 1024

F32 = jnp.float32
BF16 = jnp.bfloat16


def _vmem_limit(estimate_bytes):
    return int(min(VMEM_CAP_BYTES, max(estimate_bytes, 16 * 1024 * 1024)))


def _params(n_axes, vmem_bytes):
    return pltpu.CompilerParams(
        dimension_semantics=("arbitrary",) * n_axes,
        vmem_limit_bytes=_vmem_limit(vmem_bytes))


def _rms_scale(v):
    return lax.rsqrt(jnp.mean(v * v, axis=-1, keepdims=True) + EPS)


def _dot(a, b):
    return jnp.dot(a, b, preferred_element_type=F32)


def _dot_nt(a, b):
    return lax.dot_general(a, b, (((1,), (1,)), ((), ())), preferred_element_type=F32)


def _norm_matmul_kernel(x_ref, g_ref, w_ref, cs_ref, o_ref, h_ref):
    @pl.when(pl.program_id(1) == 0)
    def _():
        x = x_ref[...]
        h_ref[...] = (x * _rms_scale(x) * g_ref[...]).astype(BF16)

    acc = _dot(h_ref[...], w_ref[...])
    o_ref[...] = (acc * cs_ref[...]).astype(o_ref.dtype)


def _norm_matmul(x, gain, w, col_scale, *, tm, tn):
    m, d = x.shape
    n = w.shape[1]
    assert m % tm == 0 and n % tn == 0
    vmem = 2 * tm * d * 4 + tm * d * 2 + 2 * d * tn * 2 + 2 * tm * tn * 2 + 2 * tm * tn * 4
    return pl.pallas_call(
        _norm_matmul_kernel,
        grid=(m // tm, n // tn),
        in_specs=[
            pl.BlockSpec((tm, d), lambda i, j: (i, 0)),
            pl.BlockSpec((1, d), lambda i, j: (0, 0)),
            pl.BlockSpec((d, tn), lambda i, j: (0, j)),
            pl.BlockSpec((1, tn), lambda i, j: (0, j)),
        ],
        out_specs=pl.BlockSpec((tm, tn), lambda i, j: (i, j)),
        out_shape=jax.ShapeDtypeStruct((m, n), BF16),
        scratch_shapes=[pltpu.VMEM((tm, d), BF16)],
        compiler_params=_params(2, vmem),
        name="norm_matmul",
    )(x, gain.reshape(1, d), w, col_scale.reshape(1, n))


def _norm_swiglu_kernel(x_ref, g_ref, wg_ref, wu_ref, o_ref, h_ref):
    @pl.when(pl.program_id(1) == 0)
    def _():
        x = x_ref[...]
        h_ref[...] = (x * _rms_scale(x) * g_ref[...]).astype(BF16)

    h = h_ref[...]
    gate = _dot(h, wg_ref[...])
    up = _dot(h, wu_ref[...])
    o_ref[...] = (gate * (1.0 / (1.0 + jnp.exp(-gate))) * up).astype(o_ref.dtype)


def _norm_swiglu(x, gain, w_gate_up, *, tm, tn):
    m, d = x.shape
    f = w_gate_up.shape[1] // 2
    assert m % tm == 0 and f % tn == 0
    nf = f // tn
    vmem = 2 * tm * d * 4 + tm * d * 2 + 4 * d * tn * 2 + 2 * tm * tn * 2 + 4 * tm * tn * 4
    return pl.pallas_call(
        _norm_swiglu_kernel,
        grid=(m // tm, nf),
        in_specs=[
            pl.BlockSpec((tm, d), lambda i, j: (i, 0)),
            pl.BlockSpec((1, d), lambda i, j: (0, 0)),
            pl.BlockSpec((d, tn), lambda i, j: (0, j)),
            pl.BlockSpec((d, tn), lambda i, j: (0, j + nf)),
        ],
        out_specs=pl.BlockSpec((tm, tn), lambda i, j: (i, j)),
        out_shape=jax.ShapeDtypeStruct((m, f), BF16),
        scratch_shapes=[pltpu.VMEM((tm, d), BF16)],
        compiler_params=_params(2, vmem),
        name="norm_swiglu",
    )(x, gain.reshape(1, d), w_gate_up, w_gate_up)


def _matmul_norm_resid_kernel(l_ref, w_ref, r_ref, g_ref, o_ref, acc_ref):
    k = pl.program_id(1)

    @pl.when(k == 0)
    def _():
        acc_ref[...] = jnp.zeros_like(acc_ref)

    acc_ref[...] += _dot(l_ref[...], w_ref[...])

    @pl.when(k == pl.num_programs(1) - 1)
    def _():
        y = acc_ref[...]
        o_ref[...] = r_ref[...] + y * _rms_scale(y) * g_ref[...]


def _matmul_norm_resid(lhs, w, resid, gain, *, tm, tk):
    m, kdim = lhs.shape
    d = w.shape[1]
    assert m % tm == 0 and kdim % tk == 0
    vmem = 2 * tm * tk * 2 + 2 * tk * d * 2 + 4 * tm * d * 4 + 2 * tm * d * 4
    return pl.pallas_call(
        _matmul_norm_resid_kernel,
        grid=(m // tm, kdim // tk),
        in_specs=[
            pl.BlockSpec((tm, tk), lambda i, k: (i, k)),
            pl.BlockSpec((tk, d), lambda i, k: (k, 0)),
            pl.BlockSpec((tm, d), lambda i, k: (i, 0)),
            pl.BlockSpec((1, d), lambda i, k: (0, 0)),
        ],
        out_specs=pl.BlockSpec((tm, d), lambda i, k: (i, 0)),
        out_shape=jax.ShapeDtypeStruct((m, d), F32),
        scratch_shapes=[pltpu.VMEM((tm, d), F32)],
        compiler_params=_params(2, vmem),
        name="matmul_norm_resid",
    )(lhs, w, resid, gain.reshape(1, d))


def _t5_bucket(rel):
    nb = N_BUCKETS // 2
    ret = jnp.where(rel > 0, nb, 0)
    n = jnp.abs(rel)
    max_exact = nb // 2
    nf = jnp.maximum(n, 1).astype(F32)
    large = max_exact + (jnp.log(nf / max_exact) / math.log(MAX_DISTANCE / max_exact)
                         * (nb - max_exact)).astype(jnp.int32)
    large = jnp.minimum(large, nb - 1)
    return ret + jnp.where(n < max_exact, n, large)


def _far_bias_is_constant(t, seq):
    n = np.arange(t + 1, max(seq, t + 2), dtype=np.float64)
    nb, me = N_BUCKETS // 2, N_BUCKETS // 4
    b = np.minimum(me + (np.log(n / me) / math.log(MAX_DISTANCE / me) * (nb - me)).astype(np.int64), nb - 1)
    return bool(np.all(b == nb - 1)) and (t + 1) >= 2 * MAX_DISTANCE


def _diff_bias_vectors(rel_bias, t):
    j = jnp.arange(2 * t, dtype=jnp.int32)
    rel0 = jnp.where(j < t, j, j - 2 * t)
    vecs = [rel_bias[_t5_bucket(rel0 - d * t)].astype(F32) for d in range(3)]
    out = jnp.transpose(jnp.stack(vecs, axis=0), (2, 0, 1)) * LOG2E
    return out[:, :, None, :]


def _widen(col, width):
    return jnp.concatenate([col] * (width // LANES), axis=1)


def _diff_attn_kernel(q_ref, k_ref, v_ref, bvec_ref, lamv_ref, gain_ref, o_ref,
                      bias_ref, qq_ref, m_ref, l_ref, acc_ref, *, t, n_g, lam_init):
    qi = pl.program_id(2)
    nc = t // LANES
    hd = HEAD_DIM

    @pl.when(qi == 0)
    def _():
        row = lax.broadcasted_iota(jnp.int32, (t, t), 0)
        col = lax.broadcasted_iota(jnp.int32, (t, t), 1)
        shift = CHUNK.bit_length() - 1
        allowed = lax.shift_right_logical(col, shift) <= lax.shift_right_logical(row, shift)
        for g in range(n_g):
            for d in range(2):
                band = jnp.broadcast_to(bvec_ref[g, d], (t, 2 * t))
                tile_b = pltpu.roll(band, 0, 1, stride=1, stride_axis=0)[:, :t]
                bias_ref[g, d] = jnp.where(allowed, tile_b, -jnp.inf) if d == 0 else tile_b

    lane = lax.broadcasted_iota(jnp.int32, (t, hd), 1)
    first = lane < DIFF_QK_DIM
    for g in range(n_g):
        q = q_ref[:, g * hd:(g + 1) * hd].astype(F32)
        qq_ref[g, :t] = jnp.where(first, q, 0.0).astype(BF16)
        qq_ref[g, t:] = jnp.where(first, 0.0, q).astype(BF16)

    m_ref[...] = jnp.full_like(m_ref, -jnp.inf)
    l_ref[...] = jnp.zeros_like(l_ref)
    acc_ref[...] = jnp.zeros_like(acc_ref)

    def tile(ki, near):
        off = pl.multiple_of(ki * t, t)
        for g in range(n_g):
            cols = slice(g * hd, (g + 1) * hd)
            k = k_ref[pl.ds(off, t), cols]
            v = v_ref[pl.ds(off, t), cols]
            s_all = _dot_nt(qq_ref[g], k)
            for half in range(2):
                rows = slice(half * t, (half + 1) * t)
                s = s_all[rows]
                if near is not None:
                    s = s + bias_ref[g, near]
                mx = s[:, :LANES]
                for c in range(1, nc):
                    mx = jnp.maximum(mx, s[:, c * LANES:(c + 1) * LANES])
                mx = jnp.max(mx, axis=-1, keepdims=True)
                m_prev = m_ref[g, rows]
                if near is None:
                    far = bvec_ref[g, 2][:, :LANES]
                    m_new = jnp.maximum(m_prev, mx + far)
                    sub = m_new - far
                else:
                    m_new = jnp.maximum(m_prev, mx)
                    sub = m_new
                alpha = jnp.exp2(m_prev - m_new)
                l_part = None
                p = []
                for c in range(nc):
                    pc = jnp.exp2(s[:, c * LANES:(c + 1) * LANES] - sub)
                    l_part = pc if l_part is None else l_part + pc
                    p.append(pc.astype(BF16))
                l_ref[g, rows] = alpha * l_ref[g, rows] + l_part
                acc_ref[g, rows] = alpha * acc_ref[g, rows] + _dot(jnp.concatenate(p, axis=1), v)
                m_ref[g, rows] = m_new

    def far_body(ki, carry):
        tile(ki, None)
        return carry

    lax.fori_loop(0, jnp.maximum(qi - 1, 0), far_body, 0)

    @pl.when(qi >= 1)
    def _():
        tile(qi - 1, 1)

    tile(qi, 0)

    lv = lamv_ref[...]
    lam = (jnp.exp(jnp.sum(lv[0:1] * lv[1:2], axis=-1, keepdims=True))
           - jnp.exp(jnp.sum(lv[2:3] * lv[3:4], axis=-1, keepdims=True)) + lam_init)
    for g in range(n_g):
        o = acc_ref[g] / jnp.sum(l_ref[g], axis=-1, keepdims=True)
        out = o[:t] - lam * o[t:]
        y = out * _rms_scale(out) * gain_ref[...] * (1.0 - lam_init)
        o_ref[:, g * hd:(g + 1) * hd] = y.astype(o_ref.dtype)


def _diff_attn(proj, bias_vecs, lam_vecs, sub_gain, *, batch, seq, n_heads, col0, t, n_g, lam_init):
    nq = seq // t
    hd = HEAD_DIM
    assert n_heads % n_g == 0 and col0 % n_g == 0
    ng_blocks = n_heads // n_g
    c0 = col0 // n_g
    w = n_g * hd
    vmem = (4 * t * w * 2 + 4 * seq * w * 2 + n_g * 2 * t * t * 4 + n_g * 2 * t * hd * 2
            + 3 * n_g * 2 * t * LANES * 4 + n_g * 6 * 2 * t * t * 4)
    kern = functools.partial(_diff_attn_kernel, t=t, n_g=n_g, lam_init=lam_init)
    return pl.pallas_call(
        kern,
        grid=(batch, ng_blocks, nq),
        in_specs=[
            pl.BlockSpec((t, w), lambda b, h, i: (b * nq + i, c0 + h)),
            pl.BlockSpec((seq, w), lambda b, h, i: (b, c0 + ng_blocks + h)),
            pl.BlockSpec((seq, w), lambda b, h, i: (b, c0 + 2 * ng_blocks + h)),
            pl.BlockSpec((n_g, 3, 1, 2 * t), lambda b, h, i: (h, 0, 0, 0)),
            pl.BlockSpec((4, DIFF_QK_DIM), lambda b, h, i: (0, 0)),
            pl.BlockSpec((1, hd), lambda b, h, i: (0, 0)),
        ],
        out_specs=pl.BlockSpec((t, w), lambda b, h, i: (b * nq + i, h)),
        out_shape=jax.ShapeDtypeStruct((batch * seq, n_heads * hd), BF16),
        scratch_shapes=[
            pltpu.VMEM((n_g, 2, t, t), F32),
            pltpu.VMEM((n_g, 2 * t, hd), BF16),
            pltpu.VMEM((n_g, 2 * t, LANES), F32),
            pltpu.VMEM((n_g, 2 * t, LANES), F32),
            pltpu.VMEM((n_g, 2 * t, hd), F32),
        ],
        compiler_params=_params(3, vmem),
        name="diff_attn",
    )(proj, proj, proj, bias_vecs, lam_vecs, sub_gain.reshape(1, hd))


def _sb_attn_kernel(q_ref, k_ref, v_ref, tri_ref, gain_ref, o_ref, carry_ref, acc_ref, *, t, cw, n_g):
    qi = pl.program_id(2)
    hd = HEAD_DIM

    def tile(ki, diag):
        off = pl.multiple_of(ki * t, t)
        if diag:
            row = lax.broadcasted_iota(jnp.int32, (t, t), 0)
            col = lax.broadcasted_iota(jnp.int32, (t, t), 1)
            strict = col < row
        for g in range(n_g):
            cols = slice(g * hd, (g + 1) * hd)
            k = k_ref[pl.ds(off, t), cols]
            v = v_ref[pl.ds(off, t), cols]
            z = _dot_nt(q_ref[:, cols], k)
            soft = jnp.log(1.0 + jnp.exp2(-jnp.abs(z))) * LOG2E
            log_beta = jnp.minimum(z, 0.0) - soft
            log_keep = log_beta - z
            if diag:
                log_keep = jnp.where(strict, log_keep, 0.0)
            tri = tri_ref[...]
            run = None if diag else carry_ref[g]
            after = [None] * (t // cw)
            for c in reversed(range(t // cw)):
                lk = log_keep[:, c * cw:(c + 1) * cw]
                hi = lk.astype(BF16)
                lo = (lk - hi.astype(F32)).astype(BF16)
                cs = _dot(hi, tri) + _dot(lo, tri)
                after[c] = cs if run is None else cs + _widen(run, cw)
                row_sum = jnp.sum(lk, axis=-1, keepdims=True)
                run = jnp.broadcast_to(row_sum, (t, LANES)) if run is None else run + row_sum
            w = jnp.exp2(log_beta + jnp.concatenate(after, axis=1))
            if diag:
                w = jnp.where(strict, w, 0.0)
                acc_ref[g] = _dot(w.astype(BF16), v)
            else:
                acc_ref[g] += _dot(w.astype(BF16), v)
            carry_ref[g] = run

    tile(qi, True)

    def body(j, carry):
        tile(qi - 1 - j, False)
        return carry

    lax.fori_loop(0, qi, body, 0)

    for g in range(n_g):
        out = acc_ref[g]
        o_ref[:, g * hd:(g + 1) * hd] = (out * _rms_scale(out) * gain_ref[...]).astype(o_ref.dtype)


def _sb_attn(proj, out_gain, *, batch, seq, n_heads, col0, t, n_g):
    nq = seq // t
    hd = HEAD_DIM
    assert n_heads % n_g == 0 and col0 % n_g == 0
    ng_blocks = n_heads // n_g
    c0 = col0 // n_g
    wd = n_g * hd
    cw = min(t, MXU_DIM)
    tri = jnp.asarray(np.tril(np.ones((cw, cw), np.float32), -1), BF16)
    vmem = (4 * t * wd * 2 + 4 * seq * wd * 2 + 2 * cw * cw * 2
            + n_g * (t * hd * 4 + t * LANES * 4) + n_g * 8 * t * t * 4)
    kern = functools.partial(_sb_attn_kernel, t=t, cw=cw, n_g=n_g)
    return pl.pallas_call(
        kern,
        grid=(batch, ng_blocks, nq),
        in_specs=[
            pl.BlockSpec((t, wd), lambda b, h, i: (b * nq + i, c0 + h)),
            pl.BlockSpec((seq, wd), lambda b, h, i: (b, c0 + ng_blocks + h)),
            pl.BlockSpec((seq, wd), lambda b, h, i: (b, c0 + 2 * ng_blocks + h)),
            pl.BlockSpec((cw, cw), lambda b, h, i: (0, 0)),
            pl.BlockSpec((1, hd), lambda b, h, i: (0, 0)),
        ],
        out_specs=pl.BlockSpec((t, wd), lambda b, h, i: (b * nq + i, h)),
        out_shape=jax.ShapeDtypeStruct((batch * seq, n_heads * hd), BF16),
        scratch_shapes=[
            pltpu.VMEM((n_g, t, LANES), F32),
            pltpu.VMEM((n_g, t, hd), F32),
        ],
        compiler_params=_params(3, vmem),
        name="sb_attn",
    )(proj, proj, proj, tri, out_gain.reshape(1, hd))


def _mem_attn_kernel(q_ref, k_ref, v_ref, o_ref, *, n_heads, dh):
    scale = dh ** -0.5
    for h in range(n_heads):
        cols = slice(h * dh, (h + 1) * dh)
        s = _dot_nt(q_ref[:, cols], k_ref[:, cols]) * scale
        p = jnp.exp(s - jnp.max(s, axis=-1, keepdims=True))
        denom = jnp.sum(p, axis=-1, keepdims=True)
        o = _dot(p.astype(BF16), v_ref[:, cols]) / denom
        o_ref[:, cols] = o.astype(o_ref.dtype)


def _mem_attn(q, kv, *, batch, seq, n_mem, tq):
    d = q.shape[1]
    nq = seq // tq
    vmem = 4 * tq * d * 2 + 4 * n_mem * d * 2 + 8 * tq * n_mem * 4 + 2 * tq * d * 4
    kern = functools.partial(_mem_attn_kernel, n_heads=N_HEADS_MEM, dh=d // N_HEADS_MEM)
    return pl.pallas_call(
        kern,
        grid=(batch, nq),
        in_specs=[
            pl.BlockSpec((tq, d), lambda b, i: (b * nq + i, 0)),
            pl.BlockSpec((n_mem, d), lambda b, i: (b, 0)),
            pl.BlockSpec((n_mem, d), lambda b, i: (b, 1)),
        ],
        out_specs=pl.BlockSpec((tq, d), lambda b, i: (b * nq + i, 0)),
        out_shape=jax.ShapeDtypeStruct((batch * seq, d), BF16),
        compiler_params=_params(2, vmem),
        name="mem_attn",
    )(q, kv, kv)


def _lambda_init(layer_idx):
    return 0.8 - 0.6 * math.exp(-0.3 * layer_idx)


def _pick(n, prefs):
    for p in prefs:
        if n % p == 0:
            return p
    return n


def kernel(x, mem, w_in, w_out, rel_bias, lambda_q1, lambda_k1, lambda_q2, lambda_k2, diff_sub_gain, sb_gain, g_mix_pre, g_mix_post, w_mq, w_mkv, w_mo, g_mem_kv, g_mem_pre, g_mem_post, w_gate_up, w_down, g_ffn_pre, g_ffn_post):
    batch, seq, d_model = x.shape
    n_mem = mem.shape[1]
    depth = w_in.shape[0]
    width = w_out.shape[1] // 2
    n_heads = width // HEAD_DIM
    d_ff = w_down.shape[1]
    t = _pick(seq, (512, 256))
    assert w_in.shape[2] == 6 * width and t % CHUNK == 0
    assert _far_bias_is_constant(t, seq)
    n_g = _pick(n_heads, (HEADS_PER_STEP,))

    rows = batch * seq
    tm = _pick(rows, (1024, 512, 256, 128))
    xf = x.reshape(rows, d_model)
    memf = mem.reshape(batch * n_mem, d_model)
    ones = lambda n: jnp.ones((n,), F32)

    col_scale = jnp.concatenate([
        jnp.full((width,), DIFF_QK_DIM ** -0.5 * LOG2E, F32), ones(2 * width),
        jnp.full((width,), HEAD_DIM ** -0.5 * LOG2E, F32), ones(2 * width)])

    for l in range(depth):
        lam_init = _lambda_init(l)
        proj = _norm_matmul(xf, g_mix_pre[l], w_in[l].astype(BF16), col_scale,
                            tm=tm, tn=_pick(6 * width, (1024, 512, 256, 128)))
        lam_vecs = jnp.stack([lambda_q1[l], lambda_k1[l], lambda_q2[l], lambda_k2[l]]).astype(F32)
        out_a = _diff_attn(proj, _diff_bias_vectors(rel_bias, t), lam_vecs, diff_sub_gain[l],
                           batch=batch, seq=seq, n_heads=n_heads, col0=0, t=t, n_g=n_g, lam_init=lam_init)
        out_b = _sb_attn(proj, sb_gain[l], batch=batch, seq=seq, n_heads=n_heads,
                         col0=3 * n_heads, t=t, n_g=n_g)
        mix_in = jnp.concatenate([out_a, out_b], axis=-1)
        tm_r = _pick(rows, (512, 256, 128))
        xf = _matmul_norm_resid(mix_in, w_out[l].astype(BF16), xf, g_mix_post[l],
                                tm=tm_r, tk=_pick(2 * width, (1024, 512, 256, 128)))
        q_mem = _norm_matmul(xf, g_mem_pre[l], w_mq[l].astype(BF16), ones(d_model),
                             tm=tm, tn=_pick(d_model, (1024, 512, 256, 128)))
        kv = _norm_matmul(memf, g_mem_kv[l], w_mkv[l].astype(BF16), ones(2 * d_model),
                          tm=_pick(batch * n_mem, (1024, 512, 256, 128)),
                          tn=_pick(2 * d_model, (1024, 512, 256, 128)))
        o_mem = _mem_attn(q_mem, kv, batch=batch, seq=seq, n_mem=n_mem,
                          tq=_pick(seq, (512, 256, 128)))
        xf = _matmul_norm_resid(o_mem, w_mo[l].astype(BF16), xf, g_mem_post[l],
                                tm=tm_r, tk=_pick(d_model, (1024, 512, 256, 128)))
        act = _norm_swiglu(xf, g_ffn_pre[l], w_gate_up[l].astype(BF16),
                           tm=tm, tn=_pick(d_ff, (512, 256, 128)))
        xf = _matmul_norm_resid(act, w_down[l].astype(BF16), xf, g_ffn_post[l],
                                tm=tm_r, tk=_pick(d_ff, (1408, 512, 256, 128)))
    return xf.reshape(batch, seq, d_model)
```

```python
import functools
import math

import numpy as np
import jax
import jax.numpy as jnp
from jax import lax
from jax.experimental import pallas as pl
from jax.experimental.pallas import tpu as pltpu

CHUNK = 64
HEAD_DIM = 128
DIFF_QK_DIM = HEAD_DIM // 2
N_HEADS_MEM = 4
N_BUCKETS = 32
MAX_DISTANCE = 128
EPS = 1e-6
LOG2E = math.log2(math.e)

V7X_VMEM_BYTES = 64 * 1024 * 1024
LANES = 128
MXU_DIM = 256
HEADS_PER_STEP = 4
VMEM_CAP_BYTES = V7X_VMEM_BYTES - 8 * 1024 * 1024

F32 = jnp.float32
BF16 = jnp.bfloat16


def _vmem_limit(estimate_bytes):
    return int(min(VMEM_CAP_BYTES, max(estimate_bytes, 16 * 1024 * 1024)))


def _params(n_axes, vmem_bytes):
    return pltpu.CompilerParams(
        dimension_semantics=("arbitrary",) * n_axes,
        vmem_limit_bytes=_vmem_limit(vmem_bytes))


def _rms_scale(v):
    return lax.rsqrt(jnp.mean(v * v, axis=-1, keepdims=True) + EPS)


def _dot(a, b):
    return jnp.dot(a, b, preferred_element_type=F32)


def _dot_nt(a, b):
    return lax.dot_general(a, b, (((1,), (1,)), ((), ())), preferred_element_type=F32)


def _norm_matmul_kernel(x_ref, g_ref, w_ref, cs_ref, o_ref, h_ref):
    @pl.when(pl.program_id(1) == 0)
    def _():
        x = x_ref[...]
        h_ref[...] = (x * _rms_scale(x) * g_ref[...]).astype(BF16)

    acc = _dot(h_ref[...], w_ref[...])
    o_ref[...] = (acc * cs_ref[...]).astype(o_ref.dtype)


def _norm_matmul(x, gain, w, col_scale, *, tm, tn):
    m, d = x.shape
    n = w.shape[1]
    assert m % tm == 0 and n % tn == 0
    vmem = 2 * tm * d * 4 + tm * d * 2 + 2 * d * tn * 2 + 2 * tm * tn * 2 + 2 * tm * tn * 4
    return pl.pallas_call(
        _norm_matmul_kernel,
        grid=(m // tm, n // tn),
        in_specs=[
            pl.BlockSpec((tm, d), lambda i, j: (i, 0)),
            pl.BlockSpec((1, d), lambda i, j: (0, 0)),
            pl.BlockSpec((d, tn), lambda i, j: (0, j)),
            pl.BlockSpec((1, tn), lambda i, j: (0, j)),
        ],
        out_specs=pl.BlockSpec((tm, tn), lambda i, j: (i, j)),
        out_shape=jax.ShapeDtypeStruct((m, n), BF16),
        scratch_shapes=[pltpu.VMEM((tm, d), BF16)],
        compiler_params=_params(2, vmem),
        name="norm_matmul",
    )(x, gain.reshape(1, d), w, col_scale.reshape(1, n))


def _norm_swiglu_kernel(x_ref, g_ref, wg_ref, wu_ref, o_ref, h_ref):
    @pl.when(pl.program_id(1) == 0)
    def _():
        x = x_ref[...]
        h_ref[...] = (x * _rms_scale(x) * g_ref[...]).astype(BF16)

    h = h_ref[...]
    gate = _dot(h, wg_ref[...])
    up = _dot(h, wu_ref[...])
    o_ref[...] = (gate * (1.0 / (1.0 + jnp.exp(-gate))) * up).astype(o_ref.dtype)


def _norm_swiglu(x, gain, w_gate_up, *, tm, tn):
    m, d = x.shape
    f = w_gate_up.shape[1] // 2
    assert m % tm == 0 and f % tn == 0
    nf = f // tn
    vmem = 2 * tm * d * 4 + tm * d * 2 + 4 * d * tn * 2 + 2 * tm * tn * 2 + 4 * tm * tn * 4
    return pl.pallas_call(
        _norm_swiglu_kernel,
        grid=(m // tm, nf),
        in_specs=[
            pl.BlockSpec((tm, d), lambda i, j: (i, 0)),
            pl.BlockSpec((1, d), lambda i, j: (0, 0)),
            pl.BlockSpec((d, tn), lambda i, j: (0, j)),
            pl.BlockSpec((d, tn), lambda i, j: (0, j + nf)),
        ],
        out_specs=pl.BlockSpec((tm, tn), lambda i, j: (i, j)),
        out_shape=jax.ShapeDtypeStruct((m, f), BF16),
        scratch_shapes=[pltpu.VMEM((tm, d), BF16)],
        compiler_params=_params(2, vmem),
        name="norm_swiglu",
    )(x, gain.reshape(1, d), w_gate_up, w_gate_up)


def _matmul_norm_resid_kernel(l_ref, w_ref, r_ref, g_ref, o_ref, acc_ref):
    k = pl.program_id(1)

    @pl.when(k == 0)
    def _():
        acc_ref[...] = jnp.zeros_like(acc_ref)

    acc_ref[...] += _dot(l_ref[...], w_ref[...])

    @pl.when(k == pl.num_programs(1) - 1)
    def _():
        y = acc_ref[...]
        o_ref[...] = r_ref[...] + y * _rms_scale(y) * g_ref[...]


def _matmul_norm_resid(lhs, w, resid, gain, *, tm, tk):
    m, kdim = lhs.shape
    d = w.shape[1]
    assert m % tm == 0 and kdim % tk == 0
    vmem = 2 * tm * tk * 2 + 2 * tk * d * 2 + 4 * tm * d * 4 + 2 * tm * d * 4
    return pl.pallas_call(
        _matmul_norm_resid_kernel,
        grid=(m // tm, kdim // tk),
        in_specs=[
            pl.BlockSpec((tm, tk), lambda i, k: (i, k)),
            pl.BlockSpec((tk, d), lambda i, k: (k, 0)),
            pl.BlockSpec((tm, d), lambda i, k: (i, 0)),
            pl.BlockSpec((1, d), lambda i, k: (0, 0)),
        ],
        out_specs=pl.BlockSpec((tm, d), lambda i, k: (i, 0)),
        out_shape=jax.ShapeDtypeStruct((m, d), F32),
        scratch_shapes=[pltpu.VMEM((tm, d), F32)],
        compiler_params=_params(2, vmem),
        name="matmul_norm_resid",
    )(lhs, w, resid, gain.reshape(1, d))


def _matmul2_norm_resid_kernel(a_ref, b_ref, w_ref, r_ref, g_ref, o_ref):
    ka = a_ref.shape[1]
    y = _dot(a_ref[...], w_ref[:ka, :]) + _dot(b_ref[...], w_ref[ka:, :])
    o_ref[...] = r_ref[...] + y * _rms_scale(y) * g_ref[...]


def _matmul2_norm_resid(lhs_a, lhs_b, w, resid, gain, *, tm):
    m, ka = lhs_a.shape
    kb = lhs_b.shape[1]
    d = w.shape[1]
    assert m % tm == 0 and w.shape[0] == ka + kb
    vmem = 2 * tm * (ka + kb) * 2 + 2 * (ka + kb) * d * 2 + 4 * tm * d * 4 + 2 * tm * d * 4
    return pl.pallas_call(
        _matmul2_norm_resid_kernel,
        grid=(m // tm,),
        in_specs=[
            pl.BlockSpec((tm, ka), lambda i: (i, 0)),
            pl.BlockSpec((tm, kb), lambda i: (i, 0)),
            pl.BlockSpec((ka + kb, d), lambda i: (0, 0)),
            pl.BlockSpec((tm, d), lambda i: (i, 0)),
            pl.BlockSpec((1, d), lambda i: (0, 0)),
        ],
        out_specs=pl.BlockSpec((tm, d), lambda i: (i, 0)),
        out_shape=jax.ShapeDtypeStruct((m, d), F32),
        compiler_params=_params(1, vmem),
        name="matmul2_norm_resid",
    )(lhs_a, lhs_b, w, resid, gain.reshape(1, d))


def _t5_bucket(rel):
    nb = N_BUCKETS // 2
    ret = jnp.where(rel > 0, nb, 0)
    n = jnp.abs(rel)
    max_exact = nb // 2
    nf = jnp.maximum(n, 1).astype(F32)
    large = max_exact + (jnp.log(nf / max_exact) / math.log(MAX_DISTANCE / max_exact)
                         * (nb - max_exact)).astype(jnp.int32)
    large = jnp.minimum(large, nb - 1)
    return ret + jnp.where(n < max_exact, n, large)


def _far_bias_is_constant(t, seq):
    n = np.arange(t + 1, max(seq, t + 2), dtype=np.float64)
    nb, me = N_BUCKETS // 2, N_BUCKETS // 4
    b = np.minimum(me + (np.log(n / me) / math.log(MAX_DISTANCE / me) * (nb - me)).astype(np.int64), nb - 1)
    return bool(np.all(b == nb - 1)) and (t + 1) >= 2 * MAX_DISTANCE


def _diff_bias_vectors(rel_bias, t):
    j = jnp.arange(2 * t, dtype=jnp.int32)
    rel0 = jnp.where(j < t, j, j - 2 * t)
    vecs = [rel_bias[_t5_bucket(rel0 - d * t)].astype(F32) for d in range(3)]
    out = jnp.transpose(jnp.stack(vecs, axis=0), (2, 0, 1)) * LOG2E
    return out[:, :, None, :]


def _widen(col, width):
    return jnp.concatenate([col] * (width // LANES), axis=1)


def _diff_attn_kernel(q_ref, k_ref, v_ref, bvec_ref, lamv_ref, gain_ref, o_ref,
                      bias_ref, qq_ref, m_ref, l_ref, acc_ref, *, t, n_g, lam_init):
    qi = pl.program_id(2)
    nc = t // LANES
    hd = HEAD_DIM

    @pl.when(qi == 0)
    def _():
        row = lax.broadcasted_iota(jnp.int32, (t, t), 0)
        col = lax.broadcasted_iota(jnp.int32, (t, t), 1)
        shift = CHUNK.bit_length() - 1
        allowed = lax.shift_right_logical(col, shift) <= lax.shift_right_logical(row, shift)
        for g in range(n_g):
            for d in range(2):
                band = jnp.broadcast_to(bvec_ref[g, d], (t, 2 * t))
                tile_b = pltpu.roll(band, 0, 1, stride=1, stride_axis=0)[:, :t]
                bias_ref[g, d] = jnp.where(allowed, tile_b, -jnp.inf) if d == 0 else tile_b

    lane = lax.broadcasted_iota(jnp.int32, (t, hd), 1)
    first = lane < DIFF_QK_DIM
    for g in range(n_g):
        q = q_ref[:, g * hd:(g + 1) * hd].astype(F32)
        qq_ref[g, :t] = jnp.where(first, q, 0.0).astype(BF16)
        qq_ref[g, t:] = jnp.where(first, 0.0, q).astype(BF16)

    m_ref[...] = jnp.full_like(m_ref, -jnp.inf)
    l_ref[...] = jnp.zeros_like(l_ref)
    acc_ref[...] = jnp.zeros_like(acc_ref)

    def tile(ki, near):
        off = pl.multiple_of(ki * t, t)
        for g in range(n_g):
            cols = slice(g * hd, (g + 1) * hd)
            k = k_ref[pl.ds(off, t), cols]
            v = v_ref[pl.ds(off, t), cols]
            s_all = _dot_nt(qq_ref[g], k)
            for half in range(2):
                rows = slice(half * t, (half + 1) * t)
                s = s_all[rows]
                if near is not None:
                    s = s + bias_ref[g, near]
                mx = s[:, :LANES]
                for c in range(1, nc):
                    mx = jnp.maximum(mx, s[:, c * LANES:(c + 1) * LANES])
                mx = jnp.max(mx, axis=-1, keepdims=True)
                m_prev = m_ref[g, rows]
                if near is None:
                    far = bvec_ref[g, 2][:, :LANES]
                    m_new = jnp.maximum(m_prev, mx + far)
                    sub = m_new - far
                else:
                    m_new = jnp.maximum(m_prev, mx)
                    sub = m_new
                alpha = jnp.exp2(m_prev - m_new)
                l_part = None
                p = []
                for c in range(nc):
                    pc = jnp.exp2(s[:, c * LANES:(c + 1) * LANES] - sub)
                    l_part = pc if l_part is None else l_part + pc
                    p.append(pc.astype(BF16))
                l_ref[g, rows] = alpha * l_ref[g, rows] + l_part
                acc_ref[g, rows] = alpha * acc_ref[g, rows] + _dot(jnp.concatenate(p, axis=1), v)
                m_ref[g, rows] = m_new

    def far_body(ki, carry):
        tile(ki, None)
        return carry

    lax.fori_loop(0, jnp.maximum(qi - 1, 0), far_body, 0)

    @pl.when(qi >= 1)
    def _():
        tile(qi - 1, 1)

    tile(qi, 0)

    lv = lamv_ref[...]
    lam = (jnp.exp(jnp.sum(lv[0:1] * lv[1:2], axis=-1, keepdims=True))
           - jnp.exp(jnp.sum(lv[2:3] * lv[3:4], axis=-1, keepdims=True)) + lam_init)
    for g in range(n_g):
        o = acc_ref[g] / jnp.sum(l_ref[g], axis=-1, keepdims=True)
        out = o[:t] - lam * o[t:]
        y = out * _rms_scale(out) * gain_ref[...] * (1.0 - lam_init)
        o_ref[:, g * hd:(g + 1) * hd] = y.astype(o_ref.dtype)


def _diff_attn(proj, bias_vecs, lam_vecs, sub_gain, *, batch, seq, n_heads, col0, t, n_g, lam_init):
    nq = seq // t
    hd = HEAD_DIM
    assert n_heads % n_g == 0 and col0 % n_g == 0
    ng_blocks = n_heads // n_g
    c0 = col0 // n_g
    w = n_g * hd
    vmem = (4 * t * w * 2 + 4 * seq * w * 2 + n_g * 2 * t * t * 4 + n_g * 2 * t * hd * 2
            + 3 * n_g * 2 * t * LANES * 4 + n_g * 6 * 2 * t * t * 4)
    kern = functools.partial(_diff_attn_kernel, t=t, n_g=n_g, lam_init=lam_init)
    return pl.pallas_call(
        kern,
        grid=(batch, ng_blocks, nq),
        in_specs=[
            pl.BlockSpec((t, w), lambda b, h, i: (b * nq + i, c0 + h)),
            pl.BlockSpec((seq, w), lambda b, h, i: (b, c0 + ng_blocks + h)),
            pl.BlockSpec((seq, w), lambda b, h, i: (b, c0 + 2 * ng_blocks + h)),
            pl.BlockSpec((n_g, 3, 1, 2 * t), lambda b, h, i: (h, 0, 0, 0)),
            pl.BlockSpec((4, DIFF_QK_DIM), lambda b, h, i: (0, 0)),
            pl.BlockSpec((1, hd), lambda b, h, i: (0, 0)),
        ],
        out_specs=pl.BlockSpec((t, w), lambda b, h, i: (b * nq + i, h)),
        out_shape=jax.ShapeDtypeStruct((batch * seq, n_heads * hd), BF16),
        scratch_shapes=[
            pltpu.VMEM((n_g, 2, t, t), F32),
            pltpu.VMEM((n_g, 2 * t, hd), BF16),
            pltpu.VMEM((n_g, 2 * t, LANES), F32),
            pltpu.VMEM((n_g, 2 * t, LANES), F32),
            pltpu.VMEM((n_g, 2 * t, hd), F32),
        ],
        compiler_params=_params(3, vmem),
        name="diff_attn",
    )(proj, proj, proj, bias_vecs, lam_vecs, sub_gain.reshape(1, hd))


def _sb_attn_kernel(q_ref, k_ref, v_ref, tri_ref, gain_ref, o_ref, carry_ref, acc_ref, *, t, cw, n_g):
    qi = pl.program_id(2)
    hd = HEAD_DIM

    def tile(ki, diag):
        off = pl.multiple_of(ki * t, t)
        if diag:
            row = lax.broadcasted_iota(jnp.int32, (t, t), 0)
            col = lax.broadcasted_iota(jnp.int32, (t, t), 1)
            strict = col < row
        heads = range(n_g)
        head_cols = [slice(g * hd, (g + 1) * hd) for g in heads]
        tri = tri_ref[...]
        zs = [_dot_nt(q_ref[:, head_cols[g]], k_ref[pl.ds(off, t), head_cols[g]]) for g in heads]
        log_betas, log_keeps = [], []
        for g in heads:
            z = zs[g]
            neg_abs = lax.bitcast_convert_type(
                lax.bitcast_convert_type(z, jnp.uint32) | jnp.uint32(0x80000000), F32)
            neg_soft = jnp.log(1.0 + jnp.exp2(neg_abs)) * (-LOG2E)
            log_keep = neg_soft - jnp.maximum(z, 0.0)
            log_betas.append(log_keep + z)
            log_keeps.append(jnp.where(strict, log_keep, 0.0) if diag else log_keep)
        afters = []
        for g in heads:
            run = None if diag else carry_ref[g]
            after = [None] * (t // cw)
            for c in reversed(range(t // cw)):
                lk = log_keeps[g][:, c * cw:(c + 1) * cw]
                cs = _dot(lk.astype(BF16), tri)
                after[c] = cs if run is None else cs + _widen(run, cw)
                row_sum = jnp.sum(lk, axis=-1, keepdims=True)
                run = jnp.broadcast_to(row_sum, (t, LANES)) if run is None else run + row_sum
            carry_ref[g] = run
            afters.append(jnp.concatenate(after, axis=1))
        for g in heads:
            w = jnp.exp2(log_betas[g] + afters[g])
            if diag:
                w = jnp.where(strict, w, 0.0)
            pv = _dot(w.astype(BF16), v_ref[pl.ds(off, t), head_cols[g]])
            if diag:
                acc_ref[g] = pv
            else:
                acc_ref[g] += pv

    tile(qi, True)

    def body(j, carry):
        tile(qi - 1 - j, False)
        return carry

    lax.fori_loop(0, qi, body, 0)

    for g in range(n_g):
        out = acc_ref[g]
        o_ref[:, g * hd:(g + 1) * hd] = (out * _rms_scale(out) * gain_ref[...]).astype(o_ref.dtype)


def _sb_attn(proj, out_gain, *, batch, seq, n_heads, col0, t, n_g):
    nq = seq // t
    hd = HEAD_DIM
    assert n_heads % n_g == 0 and col0 % n_g == 0
    ng_blocks = n_heads // n_g
    c0 = col0 // n_g
    wd = n_g * hd
    cw = min(t, MXU_DIM)
    tri = jnp.asarray(np.tril(np.ones((cw, cw), np.float32), -1), BF16)
    vmem = (4 * t * wd * 2 + 4 * seq * wd * 2 + 2 * cw * cw * 2
            + n_g * (t * hd * 4 + t * LANES * 4) + n_g * 8 * t * t * 4)
    kern = functools.partial(_sb_attn_kernel, t=t, cw=cw, n_g=n_g)
    return pl.pallas_call(
        kern,
        grid=(batch, ng_blocks, nq),
        in_specs=[
            pl.BlockSpec((t, wd), lambda b, h, i: (b * nq + i, c0 + h)),
            pl.BlockSpec((seq, wd), lambda b, h, i: (b, c0 + ng_blocks + h)),
            pl.BlockSpec((seq, wd), lambda b, h, i: (b, c0 + 2 * ng_blocks + h)),
            pl.BlockSpec((cw, cw), lambda b, h, i: (0, 0)),
            pl.BlockSpec((1, hd), lambda b, h, i: (0, 0)),
        ],
        out_specs=pl.BlockSpec((t, wd), lambda b, h, i: (b * nq + i, h)),
        out_shape=jax.ShapeDtypeStruct((batch * seq, n_heads * hd), BF16),
        scratch_shapes=[
            pltpu.VMEM((n_g, t, LANES), F32),
            pltpu.VMEM((n_g, t, hd), F32),
        ],
        compiler_params=_params(3, vmem),
        name="sb_attn",
    )(proj, proj, proj, tri, out_gain.reshape(1, hd))


def _mem_attn_kernel(q_ref, k_ref, v_ref, o_ref, *, n_heads, dh):
    scale = dh ** -0.5
    for h in range(n_heads):
        cols = slice(h * dh, (h + 1) * dh)
        s = _dot_nt(q_ref[:, cols], k_ref[:, cols]) * scale
        p = jnp.exp(s - jnp.max(s, axis=-1, keepdims=True))
        denom = jnp.sum(p, axis=-1, keepdims=True)
        o = _dot(p.astype(BF16), v_ref[:, cols]) / denom
        o_ref[:, cols] = o.astype(o_ref.dtype)


def _mem_attn(q, kv, *, batch, seq, n_mem, tq):
    d = q.shape[1]
    nq = seq // tq
    vmem = 4 * tq * d * 2 + 4 * n_mem * d * 2 + 8 * tq * n_mem * 4 + 2 * tq * d * 4
    kern = functools.partial(_mem_attn_kernel, n_heads=N_HEADS_MEM, dh=d // N_HEADS_MEM)
    return pl.pallas_call(
        kern,
        grid=(batch, nq),
        in_specs=[
            pl.BlockSpec((tq, d), lambda b, i: (b * nq + i, 0)),
            pl.BlockSpec((n_mem, d), lambda b, i: (b, 0)),
            pl.BlockSpec((n_mem, d), lambda b, i: (b, 1)),
        ],
        out_specs=pl.BlockSpec((tq, d), lambda b, i: (b * nq + i, 0)),
        out_shape=jax.ShapeDtypeStruct((batch * seq, d), BF16),
        compiler_params=_params(2, vmem),
        name="mem_attn",
    )(q, kv, kv)


def _lambda_init(layer_idx):
    return 0.8 - 0.6 * math.exp(-0.3 * layer_idx)


def _pick(n, prefs):
    for p in prefs:
        if n % p == 0:
            return p
    return n


def kernel(x, mem, w_in, w_out, rel_bias, lambda_q1, lambda_k1, lambda_q2, lambda_k2, diff_sub_gain, sb_gain, g_mix_pre, g_mix_post, w_mq, w_mkv, w_mo, g_mem_kv, g_mem_pre, g_mem_post, w_gate_up, w_down, g_ffn_pre, g_ffn_post):
    batch, seq, d_model = x.shape
    n_mem = mem.shape[1]
    depth = w_in.shape[0]
    width = w_out.shape[1] // 2
    n_heads = width // HEAD_DIM
    d_ff = w_down.shape[1]
    t = _pick(seq, (512, 256))
    assert w_in.shape[2] == 6 * width and t % CHUNK == 0
    assert _far_bias_is_constant(t, seq)
    n_g = _pick(n_heads, (HEADS_PER_STEP,))

    rows = batch * seq
    tm = _pick(rows, (512, 256, 128))
    xf = x.reshape(rows, d_model)
    memf = mem.reshape(batch * n_mem, d_model)
    ones = lambda n: jnp.ones((n,), F32)

    col_scale = jnp.concatenate([
        jnp.full((width,), DIFF_QK_DIM ** -0.5 * LOG2E, F32), ones(2 * width),
        jnp.full((width,), HEAD_DIM ** -0.5 * LOG2E, F32), ones(2 * width)])

    for l in range(depth):
        lam_init = _lambda_init(l)
        proj = _norm_matmul(xf, g_mix_pre[l], w_in[l].astype(BF16), col_scale,
                            tm=tm, tn=_pick(6 * width, (2048, 1024, 512, 256, 128)))
        lam_vecs = jnp.stack([lambda_q1[l], lambda_k1[l], lambda_q2[l], lambda_k2[l]]).astype(F32)
        out_a = _diff_attn(proj, _diff_bias_vectors(rel_bias, t), lam_vecs, diff_sub_gain[l],
                           batch=batch, seq=seq, n_heads=n_heads, col0=0, t=t, n_g=n_g, lam_init=lam_init)
        out_b = _sb_attn(proj, sb_gain[l], batch=batch, seq=seq, n_heads=n_heads,
                         col0=3 * n_heads, t=t, n_g=n_g)
        xf = _matmul2_norm_resid(out_a, out_b, w_out[l].astype(BF16), xf, g_mix_post[l], tm=tm)
        q_mem = _norm_matmul(xf, g_mem_pre[l], w_mq[l].astype(BF16), ones(d_model),
                             tm=tm, tn=_pick(d_model, (2048, 1024, 512, 256, 128)))
        kv = _norm_matmul(memf, g_mem_kv[l], w_mkv[l].astype(BF16), ones(2 * d_model),
                          tm=_pick(batch * n_mem, (512, 256, 128)),
                          tn=_pick(2 * d_model, (2048, 1024, 512, 256, 128)))
        o_mem = _mem_attn(q_mem, kv, batch=batch, seq=seq, n_mem=n_mem,
                          tq=_pick(seq, (512, 256, 128)))
        xf = _matmul_norm_resid(o_mem, w_mo[l].astype(BF16), xf, g_mem_post[l],
                                tm=tm, tk=_pick(d_model, (2048, 1024, 512, 256, 128)))
        act = _norm_swiglu(xf, g_ffn_pre[l], w_gate_up[l].astype(BF16),
                           tm=tm, tn=_pick(d_ff, (1408, 512, 256, 128)))
        xf = _matmul_norm_resid(act, w_down[l].astype(BF16), xf, g_ffn_post[l],
                                tm=tm, tk=_pick(d_ff, (2816, 1408, 512, 256, 128)))
    return xf.reshape(batch, seq, d_model)
```

```python
import functools
import math

import numpy as np
import jax
import jax.numpy as jnp
from jax import lax
from jax.experimental import pallas as pl
from jax.experimental.pallas import tpu as pltpu

CHUNK = 64
HEAD_DIM = 128
DIFF_QK_DIM = HEAD_DIM // 2
N_HEADS_MEM = 4
N_BUCKETS = 32
MAX_DISTANCE = 128
EPS = 1e-6
LOG2E = math.log2(math.e)

V7X_VMEM_BYTES = 64 * 1024 * 1024
LANES = 128
MXU_DIM = 256
HEADS_PER_STEP = 4
SCORES_AHEAD = 4
VMEM_CAP_BYTES = V7X_VMEM_BYTES - 8 * 1024 * 1024

F32 = jnp.float32
BF16 = jnp.bfloat16


def _vmem_limit(estimate_bytes):
    return int(min(VMEM_CAP_BYTES, max(estimate_bytes, 16 * 1024 * 1024)))


def _params(n_axes, vmem_bytes):
    return pltpu.CompilerParams(
        dimension_semantics=("arbitrary",) * n_axes,
        vmem_limit_bytes=_vmem_limit(vmem_bytes))


def _rms_scale(v):
    return lax.rsqrt(jnp.mean(v * v, axis=-1, keepdims=True) + EPS)


def _dot(a, b):
    return jnp.dot(a, b, preferred_element_type=F32)


def _dot_nt(a, b):
    return lax.dot_general(a, b, (((1,), (1,)), ((), ())), preferred_element_type=F32)


def _norm_matmul_kernel(x_ref, g_ref, w_ref, cs_ref, o_ref, h_ref):
    @pl.when(pl.program_id(1) == 0)
    def _():
        x = x_ref[...]
        h_ref[...] = (x * _rms_scale(x) * g_ref[...]).astype(BF16)

    acc = _dot(h_ref[...], w_ref[...])
    o_ref[...] = (acc * cs_ref[...]).astype(o_ref.dtype)


def _norm_matmul(x, gain, w, col_scale, *, tm, tn):
    m, d = x.shape
    n = w.shape[1]
    assert m % tm == 0 and n % tn == 0
    vmem = 2 * tm * d * 4 + tm * d * 2 + 2 * d * tn * 2 + 2 * tm * tn * 2 + 2 * tm * tn * 4
    return pl.pallas_call(
        _norm_matmul_kernel,
        grid=(m // tm, n // tn),
        in_specs=[
            pl.BlockSpec((tm, d), lambda i, j: (i, 0)),
            pl.BlockSpec((1, d), lambda i, j: (0, 0)),
            pl.BlockSpec((d, tn), lambda i, j: (0, j)),
            pl.BlockSpec((1, tn), lambda i, j: (0, j)),
        ],
        out_specs=pl.BlockSpec((tm, tn), lambda i, j: (i, j)),
        out_shape=jax.ShapeDtypeStruct((m, n), BF16),
        scratch_shapes=[pltpu.VMEM((tm, d), BF16)],
        compiler_params=_params(2, vmem),
        name="norm_matmul",
    )(x, gain.reshape(1, d), w, col_scale.reshape(1, n))


def _norm_swiglu_kernel(x_ref, g_ref, wg_ref, wu_ref, o_ref, h_ref):
    @pl.when(pl.program_id(1) == 0)
    def _():
        x = x_ref[...]
        h_ref[...] = (x * _rms_scale(x) * g_ref[...]).astype(BF16)

    h = h_ref[...]
    gate = _dot(h, wg_ref[...])
    up = _dot(h, wu_ref[...])
    o_ref[...] = (gate * (1.0 / (1.0 + jnp.exp(-gate))) * up).astype(o_ref.dtype)


def _norm_swiglu(x, gain, w_gate_up, *, tm, tn):
    m, d = x.shape
    f = w_gate_up.shape[1] // 2
    assert m % tm == 0 and f % tn == 0
    nf = f // tn
    vmem = 2 * tm * d * 4 + tm * d * 2 + 4 * d * tn * 2 + 2 * tm * tn * 2 + 4 * tm * tn * 4
    return pl.pallas_call(
        _norm_swiglu_kernel,
        grid=(m // tm, nf),
        in_specs=[
            pl.BlockSpec((tm, d), lambda i, j: (i, 0)),
            pl.BlockSpec((1, d), lambda i, j: (0, 0)),
            pl.BlockSpec((d, tn), lambda i, j: (0, j)),
            pl.BlockSpec((d, tn), lambda i, j: (0, j + nf)),
        ],
        out_specs=pl.BlockSpec((tm, tn), lambda i, j: (i, j)),
        out_shape=jax.ShapeDtypeStruct((m, f), BF16),
        scratch_shapes=[pltpu.VMEM((tm, d), BF16)],
        compiler_params=_params(2, vmem),
        name="norm_swiglu",
    )(x, gain.reshape(1, d), w_gate_up, w_gate_up)


def _matmul_norm_resid_kernel(l_ref, w_ref, r_ref, g_ref, o_ref, acc_ref):
    k = pl.program_id(1)

    @pl.when(k == 0)
    def _():
        acc_ref[...] = jnp.zeros_like(acc_ref)

    acc_ref[...] += _dot(l_ref[...], w_ref[...])

    @pl.when(k == pl.num_programs(1) - 1)
    def _():
        y = acc_ref[...]
        o_ref[...] = r_ref[...] + y * _rms_scale(y) * g_ref[...]


def _matmul_norm_resid(lhs, w, resid, gain, *, tm, tk):
    m, kdim = lhs.shape
    d = w.shape[1]
    assert m % tm == 0 and kdim % tk == 0
    vmem = 2 * tm * tk * 2 + 2 * tk * d * 2 + 4 * tm * d * 4 + 2 * tm * d * 4
    return pl.pallas_call(
        _matmul_norm_resid_kernel,
        grid=(m // tm, kdim // tk),
        in_specs=[
            pl.BlockSpec((tm, tk), lambda i, k: (i, k)),
            pl.BlockSpec((tk, d), lambda i, k: (k, 0)),
            pl.BlockSpec((tm, d), lambda i, k: (i, 0)),
            pl.BlockSpec((1, d), lambda i, k: (0, 0)),
        ],
        out_specs=pl.BlockSpec((tm, d), lambda i, k: (i, 0)),
        out_shape=jax.ShapeDtypeStruct((m, d), F32),
        scratch_shapes=[pltpu.VMEM((tm, d), F32)],
        compiler_params=_params(2, vmem),
        name="matmul_norm_resid",
    )(lhs, w, resid, gain.reshape(1, d))


def _matmul2_norm_resid_kernel(a_ref, b_ref, w_ref, r_ref, g_ref, o_ref):
    ka = a_ref.shape[1]
    y = _dot(a_ref[...], w_ref[:ka, :]) + _dot(b_ref[...], w_ref[ka:, :])
    o_ref[...] = r_ref[...] + y * _rms_scale(y) * g_ref[...]


def _matmul2_norm_resid(lhs_a, lhs_b, w, resid, gain, *, tm):
    m, ka = lhs_a.shape
    kb = lhs_b.shape[1]
    d = w.shape[1]
    assert m % tm == 0 and w.shape[0] == ka + kb
    vmem = 2 * tm * (ka + kb) * 2 + 2 * (ka + kb) * d * 2 + 4 * tm * d * 4 + 2 * tm * d * 4
    return pl.pallas_call(
        _matmul2_norm_resid_kernel,
        grid=(m // tm,),
        in_specs=[
            pl.BlockSpec((tm, ka), lambda i: (i, 0)),
            pl.BlockSpec((tm, kb), lambda i: (i, 0)),
            pl.BlockSpec((ka + kb, d), lambda i: (0, 0)),
            pl.BlockSpec((tm, d), lambda i: (i, 0)),
            pl.BlockSpec((1, d), lambda i: (0, 0)),
        ],
        out_specs=pl.BlockSpec((tm, d), lambda i: (i, 0)),
        out_shape=jax.ShapeDtypeStruct((m, d), F32),
        compiler_params=_params(1, vmem),
        name="matmul2_norm_resid",
    )(lhs_a, lhs_b, w, resid, gain.reshape(1, d))


def _t5_bucket(rel):
    nb = N_BUCKETS // 2
    ret = jnp.where(rel > 0, nb, 0)
    n = jnp.abs(rel)
    max_exact = nb // 2
    nf = jnp.maximum(n, 1).astype(F32)
    large = max_exact + (jnp.log(nf / max_exact) / math.log(MAX_DISTANCE / max_exact)
                         * (nb - max_exact)).astype(jnp.int32)
    large = jnp.minimum(large, nb - 1)
    return ret + jnp.where(n < max_exact, n, large)


def _far_bias_is_constant(t, seq):
    n = np.arange(t + 1, max(seq, t + 2), dtype=np.float64)
    nb, me = N_BUCKETS // 2, N_BUCKETS // 4
    b = np.minimum(me + (np.log(n / me) / math.log(MAX_DISTANCE / me) * (nb - me)).astype(np.int64), nb - 1)
    return bool(np.all(b == nb - 1)) and (t + 1) >= 2 * MAX_DISTANCE


def _diff_bias_vectors(rel_bias, t):
    j = jnp.arange(2 * t, dtype=jnp.int32)
    rel0 = jnp.where(j < t, -j, 2 * t - j)
    vecs = [rel_bias[_t5_bucket(rel0 - d * t)].astype(F32) for d in range(3)]
    out = jnp.transpose(jnp.stack(vecs, axis=0), (2, 0, 1)) * LOG2E
    return out[:, :, None, :]


def _widen(col, width):
    return jnp.concatenate([col] * (width // LANES), axis=1)


def _diff_attn_kernel(q_ref, k_ref, v_ref, bvec_ref, lamv_ref, gain_ref, o_ref,
                      bias_ref, vt_ref, qq_ref, m_ref, l_ref, acc_ref, *, t, n_g, lam_init):
    qi = pl.program_id(2)
    hd = HEAD_DIM
    seq = k_ref.shape[0]

    @pl.when(qi == 0)
    def _():
        key = lax.broadcasted_iota(jnp.int32, (t, t), 0)
        qry = lax.broadcasted_iota(jnp.int32, (t, t), 1)
        shift = CHUNK.bit_length() - 1
        allowed = lax.shift_right_logical(key, shift) <= lax.shift_right_logical(qry, shift)
        for g in range(n_g):
            for d in range(2):
                band = jnp.broadcast_to(bvec_ref[g, d], (t, 2 * t))
                tile_b = pltpu.roll(band, 0, 1, stride=1, stride_axis=0)[:, :t]
                bias_ref[g, d] = jnp.where(allowed, tile_b, -jnp.inf) if d == 0 else tile_b
            for c in range(seq // t):
                blk = v_ref[c * t:(c + 1) * t, g * hd:(g + 1) * hd].astype(F32)
                vt_ref[g, c] = blk.T.astype(BF16)

    lane = lax.broadcasted_iota(jnp.int32, (t, hd), 1)
    first = lane < DIFF_QK_DIM
    for g in range(n_g):
        q = q_ref[:, g * hd:(g + 1) * hd].astype(F32)
        qq_ref[g, :t] = jnp.where(first, q, 0.0).astype(BF16)
        qq_ref[g, t:] = jnp.where(first, 0.0, q).astype(BF16)

    m_ref[...] = jnp.full_like(m_ref, -jnp.inf)
    l_ref[...] = jnp.zeros_like(l_ref)
    acc_ref[...] = jnp.zeros_like(acc_ref)

    def tile(ki, near):
        off = pl.multiple_of(ki * t, t)
        qc = MXU_DIM
        depth = SCORES_AHEAD
        items = [(g, c) for g in range(n_g) for c in range(2 * t // qc)]

        def scores(g, c):
            return _dot_nt(k_ref[pl.ds(off, t), g * hd:(g + 1) * hd], qq_ref[g, c * qc:(c + 1) * qc])

        def softmax(g, c, s):
            cols = slice(c * qc, (c + 1) * qc)
            if near is not None:
                bc = (c * qc) % t
                s = bias_ref[g, near, :, bc:bc + qc] + s
            mx = jnp.max(s, axis=0, keepdims=True)
            m_prev = m_ref[g, :, cols]
            if near is None:
                far = bvec_ref[g, 2][:, :qc]
                m_new = jnp.maximum(m_prev, mx + far)
                sub = m_new - far
            else:
                m_new = jnp.maximum(m_prev, mx)
                sub = m_new
            alpha = jnp.exp2(m_prev - m_new)
            p = jnp.exp2(s - sub)
            l_ref[g, :, cols] = alpha * l_ref[g, :, cols] + jnp.sum(p, axis=0, keepdims=True)
            m_ref[g, :, cols] = m_new
            return alpha, p.astype(BF16)

        def weigh(g, c, alpha, p):
            cols = slice(c * qc, (c + 1) * qc)
            acc_ref[g, :, cols] = alpha * acc_ref[g, :, cols] + _dot(vt_ref[g, ki], p)

        pending = [scores(*items[i]) for i in range(min(depth, len(items)))]
        for i, (g, c) in enumerate(items):
            if i + depth < len(items):
                pending.append(scores(*items[i + depth]))
            weigh(g, c, *softmax(g, c, pending[i]))
            pending[i] = None

    def far_body(ki, carry):
        tile(ki, None)
        return carry

    lax.fori_loop(0, jnp.maximum(qi - 1, 0), far_body, 0)

    @pl.when(qi >= 1)
    def _():
        tile(qi - 1, 1)

    tile(qi, 0)

    lv = lamv_ref[...]
    lam = (jnp.exp(jnp.sum(lv[0:1] * lv[1:2], axis=-1, keepdims=True))
           - jnp.exp(jnp.sum(lv[2:3] * lv[3:4], axis=-1, keepdims=True)) + lam_init)
    for g in range(n_g):
        o = acc_ref[g] / l_ref[g]
        out = o[:, :t] - lam * o[:, t:]
        scale = lax.rsqrt(jnp.mean(out * out, axis=0, keepdims=True) + EPS)
        y = out * scale * gain_ref[...] * (1.0 - lam_init)
        o_ref[:, g * hd:(g + 1) * hd] = y.T.astype(o_ref.dtype)


def _diff_attn(proj, bias_vecs, lam_vecs, sub_gain, *, batch, seq, n_heads, col0, t, n_g, lam_init):
    nq = seq // t
    hd = HEAD_DIM
    assert n_heads % n_g == 0 and col0 % n_g == 0
    ng_blocks = n_heads // n_g
    c0 = col0 // n_g
    w = n_g * hd
    vmem = (4 * t * w * 2 + 4 * seq * w * 2 + n_g * 2 * t * t * 4 + n_g * 2 * t * hd * 2
            + n_g * seq * hd * 2 + n_g * 2 * t * hd * 4 + n_g * 6 * 2 * t * t * 4)
    kern = functools.partial(_diff_attn_kernel, t=t, n_g=n_g, lam_init=lam_init)
    return pl.pallas_call(
        kern,
        grid=(batch, ng_blocks, nq),
        in_specs=[
            pl.BlockSpec((t, w), lambda b, h, i: (b * nq + i, c0 + h)),
            pl.BlockSpec((seq, w), lambda b, h, i: (b, c0 + ng_blocks + h)),
            pl.BlockSpec((seq, w), lambda b, h, i: (b, c0 + 2 * ng_blocks + h)),
            pl.BlockSpec((n_g, 3, 1, 2 * t), lambda b, h, i: (h, 0, 0, 0)),
            pl.BlockSpec((4, DIFF_QK_DIM), lambda b, h, i: (0, 0)),
            pl.BlockSpec((hd, 1), lambda b, h, i: (0, 0)),
        ],
        out_specs=pl.BlockSpec((t, w), lambda b, h, i: (b * nq + i, h)),
        out_shape=jax.ShapeDtypeStruct((batch * seq, n_heads * hd), BF16),
        scratch_shapes=[
            pltpu.VMEM((n_g, 2, t, t), F32),
            pltpu.VMEM((n_g, seq // t, hd, t), BF16),
            pltpu.VMEM((n_g, 2 * t, hd), BF16),
            pltpu.VMEM((n_g, 1, 2 * t), F32),
            pltpu.VMEM((n_g, 1, 2 * t), F32),
            pltpu.VMEM((n_g, hd, 2 * t), F32),
        ],
        compiler_params=_params(3, vmem),
        name="diff_attn",
    )(proj, proj, proj, bias_vecs, lam_vecs, sub_gain.reshape(hd, 1))


def _sb_attn_kernel(q_ref, k_ref, v_ref, tri_ref, gain_ref, o_ref, carry_ref, acc_ref, *, t, cw, n_g):
    qi = pl.program_id(2)
    hd = HEAD_DIM

    def tile(ki, diag):
        off = pl.multiple_of(ki * t, t)
        if diag:
            row = lax.broadcasted_iota(jnp.int32, (t, t), 0)
            col = lax.broadcasted_iota(jnp.int32, (t, t), 1)
            strict = col < row
        head_cols = [slice(g * hd, (g + 1) * hd) for g in range(n_g)]
        tri = tri_ref[...]

        def scores(g):
            return _dot_nt(q_ref[:, head_cols[g]], k_ref[pl.ds(off, t), head_cols[g]])

        def logits(g, z):
            neg_abs = lax.bitcast_convert_type(
                lax.bitcast_convert_type(z, jnp.uint32) | jnp.uint32(0x80000000), F32)
            neg_soft = jnp.log(1.0 + jnp.exp2(neg_abs)) * (-LOG2E)
            log_keep = neg_soft - jnp.maximum(z, 0.0)
            log_beta = log_keep + z
            if diag:
                log_keep = jnp.where(strict, log_keep, 0.0)
            run = None if diag else carry_ref[g]
            after = [None] * (t // cw)
            for c in reversed(range(t // cw)):
                lk = log_keep[:, c * cw:(c + 1) * cw]
                cs = _dot(lk.astype(BF16), tri)
                after[c] = cs if run is None else _widen(run, cw) + cs
                row_sum = jnp.sum(lk, axis=-1, keepdims=True)
                run = jnp.broadcast_to(row_sum, (t, LANES)) if run is None else run + row_sum
            carry_ref[g] = run
            return log_beta, jnp.concatenate(after, axis=1)

        def mix(g, log_beta, after):
            w = jnp.exp2(log_beta + after)
            if diag:
                w = jnp.where(strict, w, 0.0)
            pv = _dot(w.astype(BF16), v_ref[pl.ds(off, t), head_cols[g]])
            if diag:
                acc_ref[g] = pv
            else:
                acc_ref[g] += pv

        z_next = scores(0)
        for g in range(n_g):
            z = z_next
            if g + 1 < n_g:
                z_next = scores(g + 1)
            mix(g, *logits(g, z))

    tile(qi, True)

    def body(j, carry):
        tile(qi - 1 - j, False)
        return carry

    lax.fori_loop(0, qi, body, 0)

    for g in range(n_g):
        out = acc_ref[g]
        o_ref[:, g * hd:(g + 1) * hd] = (out * _rms_scale(out) * gain_ref[...]).astype(o_ref.dtype)


def _sb_attn(proj, out_gain, *, batch, seq, n_heads, col0, t, n_g):
    nq = seq // t
    hd = HEAD_DIM
    assert n_heads % n_g == 0 and col0 % n_g == 0
    ng_blocks = n_heads // n_g
    c0 = col0 // n_g
    wd = n_g * hd
    cw = min(t, MXU_DIM)
    tri = jnp.asarray(np.tril(np.ones((cw, cw), np.float32), -1), BF16)
    vmem = (4 * t * wd * 2 + 4 * seq * wd * 2 + 2 * cw * cw * 2
            + n_g * (t * hd * 4 + t * LANES * 4) + n_g * 8 * t * t * 4)
    kern = functools.partial(_sb_attn_kernel, t=t, cw=cw, n_g=n_g)
    return pl.pallas_call(
        kern,
        grid=(batch, ng_blocks, nq),
        in_specs=[
            pl.BlockSpec((t, wd), lambda b, h, i: (b * nq + i, c0 + h)),
            pl.BlockSpec((seq, wd), lambda b, h, i: (b, c0 + ng_blocks + h)),
            pl.BlockSpec((seq, wd), lambda b, h, i: (b, c0 + 2 * ng_blocks + h)),
            pl.BlockSpec((cw, cw), lambda b, h, i: (0, 0)),
            pl.BlockSpec((1, hd), lambda b, h, i: (0, 0)),
        ],
        out_specs=pl.BlockSpec((t, wd), lambda b, h, i: (b * nq + i, h)),
        out_shape=jax.ShapeDtypeStruct((batch * seq, n_heads * hd), BF16),
        scratch_shapes=[
            pltpu.VMEM((n_g, t, LANES), F32),
            pltpu.VMEM((n_g, t, hd), F32),
        ],
        compiler_params=_params(3, vmem),
        name="sb_attn",
    )(proj, proj, proj, tri, out_gain.reshape(1, hd))


def _mem_attn_kernel(q_ref, k_ref, v_ref, o_ref, *, n_heads, dh):
    scale = dh ** -0.5
    for h in range(n_heads):
        cols = slice(h * dh, (h + 1) * dh)
        s = _dot_nt(q_ref[:, cols], k_ref[:, cols]) * scale
        p = jnp.exp(s - jnp.max(s, axis=-1, keepdims=True))
        denom = jnp.sum(p, axis=-1, keepdims=True)
        o = _dot(p.astype(BF16), v_ref[:, cols]) / denom
        o_ref[:, cols] = o.astype(o_ref.dtype)


def _mem_attn(q, kv, *, batch, seq, n_mem, tq):
    d = q.shape[1]
    nq = seq // tq
    vmem = 4 * tq * d * 2 + 4 * n_mem * d * 2 + 8 * tq * n_mem * 4 + 2 * tq * d * 4
    kern = functools.partial(_mem_attn_kernel, n_heads=N_HEADS_MEM, dh=d // N_HEADS_MEM)
    return pl.pallas_call(
        kern,
        grid=(batch, nq),
        in_specs=[
            pl.BlockSpec((tq, d), lambda b, i: (b * nq + i, 0)),
            pl.BlockSpec((n_mem, d), lambda b, i: (b, 0)),
            pl.BlockSpec((n_mem, d), lambda b, i: (b, 1)),
        ],
        out_specs=pl.BlockSpec((tq, d), lambda b, i: (b * nq + i, 0)),
        out_shape=jax.ShapeDtypeStruct((batch * seq, d), BF16),
        compiler_params=_params(2, vmem),
        name="mem_attn",
    )(q, kv, kv)


def _lambda_init(layer_idx):
    return 0.8 - 0.6 * math.exp(-0.3 * layer_idx)


def _pick(n, prefs):
    for p in prefs:
        if n % p == 0:
            return p
    return n


def kernel(x, mem, w_in, w_out, rel_bias, lambda_q1, lambda_k1, lambda_q2, lambda_k2, diff_sub_gain, sb_gain, g_mix_pre, g_mix_post, w_mq, w_mkv, w_mo, g_mem_kv, g_mem_pre, g_mem_post, w_gate_up, w_down, g_ffn_pre, g_ffn_post):
    batch, seq, d_model = x.shape
    n_mem = mem.shape[1]
    depth = w_in.shape[0]
    width = w_out.shape[1] // 2
    n_heads = width // HEAD_DIM
    d_ff = w_down.shape[1]
    t = _pick(seq, (512, 256))
    assert w_in.shape[2] == 6 * width and t % CHUNK == 0
    assert _far_bias_is_constant(t, seq)
    n_g = _pick(n_heads, (HEADS_PER_STEP,))

    rows = batch * seq
    tm = _pick(rows, (512, 256, 128))
    tm_n = _pick(rows, (1024, 512, 256, 128))
    xf = x.reshape(rows, d_model)
    memf = mem.reshape(batch * n_mem, d_model)
    ones = lambda n: jnp.ones((n,), F32)

    col_scale = jnp.concatenate([
        jnp.full((width,), DIFF_QK_DIM ** -0.5 * LOG2E, F32), ones(2 * width),
        jnp.full((width,), HEAD_DIM ** -0.5 * LOG2E, F32), ones(2 * width)])

    for l in range(depth):
        lam_init = _lambda_init(l)
        proj = _norm_matmul(xf, g_mix_pre[l], w_in[l].astype(BF16), col_scale,
                            tm=tm_n, tn=_pick(6 * width, (1536, 1024, 512, 256, 128)))
        lam_vecs = jnp.stack([lambda_q1[l], lambda_k1[l], lambda_q2[l], lambda_k2[l]]).astype(F32)
        out_a = _diff_attn(proj, _diff_bias_vectors(rel_bias, t), lam_vecs, diff_sub_gain[l],
                           batch=batch, seq=seq, n_heads=n_heads, col0=0, t=t, n_g=n_g, lam_init=lam_init)
        out_b = _sb_attn(proj, sb_gain[l], batch=batch, seq=seq, n_heads=n_heads,
                         col0=3 * n_heads, t=t, n_g=n_g)
        xf = _matmul2_norm_resid(out_a, out_b, w_out[l].astype(BF16), xf, g_mix_post[l], tm=tm)
        q_mem = _norm_matmul(xf, g_mem_pre[l], w_mq[l].astype(BF16), ones(d_model),
                             tm=tm_n, tn=_pick(d_model, (2048, 1024, 512, 256, 128)))
        kv = _norm_matmul(memf, g_mem_kv[l], w_mkv[l].astype(BF16), ones(2 * d_model),
                          tm=_pick(batch * n_mem, (512, 256, 128)),
                          tn=_pick(2 * d_model, (2048, 1024, 512, 256, 128)))
        o_mem = _mem_attn(q_mem, kv, batch=batch, seq=seq, n_mem=n_mem,
                          tq=_pick(seq, (512, 256, 128)))
        xf = _matmul_norm_resid(o_mem, w_mo[l].astype(BF16), xf, g_mem_post[l],
                                tm=tm, tk=_pick(d_model, (2048, 1024, 512, 256, 128)))
        act = _norm_swiglu(xf, g_ffn_pre[l], w_gate_up[l].astype(BF16),
                           tm=tm_n, tn=_pick(d_ff, (512, 256, 128)))
        xf = _matmul_norm_resid(act, w_down[l].astype(BF16), xf, g_ffn_post[l],
                                tm=tm, tk=_pick(d_ff, (2816, 1408, 512, 256, 128)))
    return xf.reshape(batch, seq, d_model)
```

```python
import functools
import math

import numpy as np
import jax
import jax.numpy as jnp
from jax import lax
from jax.experimental import pallas as pl
from jax.experimental.pallas import tpu as pltpu

CHUNK = 64
HEAD_DIM = 128
DIFF_QK_DIM = HEAD_DIM // 2
N_HEADS_MEM = 4
N_BUCKETS = 32
MAX_DISTANCE = 128
EPS = 1e-6
LOG2E = math.log2(math.e)
F32_UNDERFLOW_LOG2 = -160.0

V7X_VMEM_BYTES = 64 * 1024 * 1024
LANES = 128
MXU_DIM = 256
HEADS_PER_STEP = 4
SCORES_AHEAD = 4
VMEM_CAP_BYTES = V7X_VMEM_BYTES - 8 * 1024 * 1024

F32 = jnp.float32
BF16 = jnp.bfloat16


def _vmem_limit(estimate_bytes):
    return int(min(VMEM_CAP_BYTES, max(estimate_bytes, 16 * 1024 * 1024)))


def _params(n_axes, vmem_bytes):
    return pltpu.CompilerParams(
        dimension_semantics=("arbitrary",) * n_axes,
        vmem_limit_bytes=_vmem_limit(vmem_bytes))


def _rms_scale(v):
    return lax.rsqrt(jnp.mean(v * v, axis=-1, keepdims=True) + EPS)


def _dot(a, b):
    return jnp.dot(a, b, preferred_element_type=F32)


def _dot_nt(a, b):
    return lax.dot_general(a, b, (((1,), (1,)), ((), ())), preferred_element_type=F32)


def _norm_matmul_kernel(x_ref, g_ref, w_ref, cs_ref, o_ref, h_ref):
    @pl.when(pl.program_id(1) == 0)
    def _():
        x = x_ref[...]
        h_ref[...] = (x * _rms_scale(x) * g_ref[...]).astype(BF16)

    acc = _dot(h_ref[...], w_ref[...])
    o_ref[...] = (acc * cs_ref[...]).astype(o_ref.dtype)


def _norm_matmul(x, gain, w, col_scale, *, tm, tn):
    m, d = x.shape
    n = w.shape[1]
    assert m % tm == 0 and n % tn == 0
    vmem = 2 * tm * d * 4 + tm * d * 2 + 2 * d * tn * 2 + 2 * tm * tn * 2 + 2 * tm * tn * 4
    return pl.pallas_call(
        _norm_matmul_kernel,
        grid=(m // tm, n // tn),
        in_specs=[
            pl.BlockSpec((tm, d), lambda i, j: (i, 0)),
            pl.BlockSpec((1, d), lambda i, j: (0, 0)),
            pl.BlockSpec((d, tn), lambda i, j: (0, j)),
            pl.BlockSpec((1, tn), lambda i, j: (0, j)),
        ],
        out_specs=pl.BlockSpec((tm, tn), lambda i, j: (i, j)),
        out_shape=jax.ShapeDtypeStruct((m, n), BF16),
        scratch_shapes=[pltpu.VMEM((tm, d), BF16)],
        compiler_params=_params(2, vmem),
        name="norm_matmul",
    )(x, gain.reshape(1, d), w, col_scale.reshape(1, n))


def _norm_swiglu_kernel(x_ref, g_ref, wg_ref, wu_ref, o_ref, h_ref):
    @pl.when(pl.program_id(1) == 0)
    def _():
        x = x_ref[...]
        h_ref[...] = (x * _rms_scale(x) * g_ref[...]).astype(BF16)

    h = h_ref[...]
    gate = _dot(h, wg_ref[...])
    up = _dot(h, wu_ref[...])
    o_ref[...] = (gate * (1.0 / (1.0 + jnp.exp(-gate))) * up).astype(o_ref.dtype)


def _norm_swiglu(x, gain, w_gate_up, *, tm, tn):
    m, d = x.shape
    f = w_gate_up.shape[1] // 2
    assert m % tm == 0 and f % tn == 0
    nf = f // tn
    vmem = 2 * tm * d * 4 + tm * d * 2 + 4 * d * tn * 2 + 2 * tm * tn * 2 + 4 * tm * tn * 4
    return pl.pallas_call(
        _norm_swiglu_kernel,
        grid=(m // tm, nf),
        in_specs=[
            pl.BlockSpec((tm, d), lambda i, j: (i, 0)),
            pl.BlockSpec((1, d), lambda i, j: (0, 0)),
            pl.BlockSpec((d, tn), lambda i, j: (0, j)),
            pl.BlockSpec((d, tn), lambda i, j: (0, j + nf)),
        ],
        out_specs=pl.BlockSpec((tm, tn), lambda i, j: (i, j)),
        out_shape=jax.ShapeDtypeStruct((m, f), BF16),
        scratch_shapes=[pltpu.VMEM((tm, d), BF16)],
        compiler_params=_params(2, vmem),
        name="norm_swiglu",
    )(x, gain.reshape(1, d), w_gate_up, w_gate_up)


def _matmul_norm_resid_kernel(l_ref, w_ref, r_ref, g_ref, o_ref, acc_ref):
    k = pl.program_id(1)

    @pl.when(k == 0)
    def _():
        acc_ref[...] = jnp.zeros_like(acc_ref)

    acc_ref[...] += _dot(l_ref[...], w_ref[...])

    @pl.when(k == pl.num_programs(1) - 1)
    def _():
        y = acc_ref[...]
        o_ref[...] = r_ref[...] + y * _rms_scale(y) * g_ref[...]


def _matmul_norm_resid(lhs, w, resid, gain, *, tm, tk):
    m, kdim = lhs.shape
    d = w.shape[1]
    assert m % tm == 0 and kdim % tk == 0
    vmem = 2 * tm * tk * 2 + 2 * tk * d * 2 + 4 * tm * d * 4 + 2 * tm * d * 4
    return pl.pallas_call(
        _matmul_norm_resid_kernel,
        grid=(m // tm, kdim // tk),
        in_specs=[
            pl.BlockSpec((tm, tk), lambda i, k: (i, k)),
            pl.BlockSpec((tk, d), lambda i, k: (k, 0)),
            pl.BlockSpec((tm, d), lambda i, k: (i, 0)),
            pl.BlockSpec((1, d), lambda i, k: (0, 0)),
        ],
        out_specs=pl.BlockSpec((tm, d), lambda i, k: (i, 0)),
        out_shape=jax.ShapeDtypeStruct((m, d), F32),
        scratch_shapes=[pltpu.VMEM((tm, d), F32)],
        compiler_params=_params(2, vmem),
        name="matmul_norm_resid",
    )(lhs, w, resid, gain.reshape(1, d))


def _matmul2_norm_resid_kernel(a_ref, b_ref, w_ref, r_ref, g_ref, o_ref):
    ka = a_ref.shape[1]
    y = _dot(a_ref[...], w_ref[:ka, :]) + _dot(b_ref[...], w_ref[ka:, :])
    o_ref[...] = r_ref[...] + y * _rms_scale(y) * g_ref[...]


def _matmul2_norm_resid(lhs_a, lhs_b, w, resid, gain, *, tm):
    m, ka = lhs_a.shape
    kb = lhs_b.shape[1]
    d = w.shape[1]
    assert m % tm == 0 and w.shape[0] == ka + kb
    vmem = 2 * tm * (ka + kb) * 2 + 2 * (ka + kb) * d * 2 + 4 * tm * d * 4 + 2 * tm * d * 4
    return pl.pallas_call(
        _matmul2_norm_resid_kernel,
        grid=(m // tm,),
        in_specs=[
            pl.BlockSpec((tm, ka), lambda i: (i, 0)),
            pl.BlockSpec((tm, kb), lambda i: (i, 0)),
            pl.BlockSpec((ka + kb, d), lambda i: (0, 0)),
            pl.BlockSpec((tm, d), lambda i: (i, 0)),
            pl.BlockSpec((1, d), lambda i: (0, 0)),
        ],
        out_specs=pl.BlockSpec((tm, d), lambda i: (i, 0)),
        out_shape=jax.ShapeDtypeStruct((m, d), F32),
        compiler_params=_params(1, vmem),
        name="matmul2_norm_resid",
    )(lhs_a, lhs_b, w, resid, gain.reshape(1, d))


def _t5_bucket(rel):
    nb = N_BUCKETS // 2
    ret = jnp.where(rel > 0, nb, 0)
    n = jnp.abs(rel)
    max_exact = nb // 2
    nf = jnp.maximum(n, 1).astype(F32)
    large = max_exact + (jnp.log(nf / max_exact) / math.log(MAX_DISTANCE / max_exact)
                         * (nb - max_exact)).astype(jnp.int32)
    large = jnp.minimum(large, nb - 1)
    return ret + jnp.where(n < max_exact, n, large)


def _far_bias_is_constant(t, seq):
    n = np.arange(t + 1, max(seq, t + 2), dtype=np.float64)
    nb, me = N_BUCKETS // 2, N_BUCKETS // 4
    b = np.minimum(me + (np.log(n / me) / math.log(MAX_DISTANCE / me) * (nb - me)).astype(np.int64), nb - 1)
    return bool(np.all(b == nb - 1)) and (t + 1) >= 2 * MAX_DISTANCE


def _diff_bias_vectors(rel_bias, t):
    j = jnp.arange(2 * t, dtype=jnp.int32)
    rel0 = jnp.where(j < t, -j, 2 * t - j)
    vecs = [rel_bias[_t5_bucket(rel0 - d * t)].astype(F32) for d in range(3)]
    out = jnp.transpose(jnp.stack(vecs, axis=0), (2, 0, 1)) * LOG2E
    return out[:, :, None, :]


def _widen(col, width):
    return jnp.concatenate([col] * (width // LANES), axis=1)


def _diff_attn_kernel(q_ref, k_ref, v_ref, bvec_ref, lamv_ref, gain_ref, o_ref,
                      bias_ref, vt_ref, qq_ref, m_ref, l_ref, acc_ref, *, t, n_g, lam_init):
    qi = pl.program_id(2)
    hd = HEAD_DIM
    seq = k_ref.shape[0]

    @pl.when(qi == 0)
    def _():
        key = lax.broadcasted_iota(jnp.int32, (t, t), 0)
        qry = lax.broadcasted_iota(jnp.int32, (t, t), 1)
        shift = CHUNK.bit_length() - 1
        allowed = lax.shift_right_logical(key, shift) <= lax.shift_right_logical(qry, shift)
        for g in range(n_g):
            for d in range(2):
                band = jnp.broadcast_to(bvec_ref[g, d], (t, 2 * t))
                tile_b = pltpu.roll(band, 0, 1, stride=1, stride_axis=0)[:, :t]
                bias_ref[g, d] = jnp.where(allowed, tile_b, -jnp.inf) if d == 0 else tile_b
            for c in range(seq // t):
                blk = v_ref[c * t:(c + 1) * t, g * hd:(g + 1) * hd].astype(F32)
                vt_ref[g, c] = blk.T.astype(BF16)

    lane = lax.broadcasted_iota(jnp.int32, (t, hd), 1)
    first = lane < DIFF_QK_DIM
    for g in range(n_g):
        q = q_ref[:, g * hd:(g + 1) * hd].astype(F32)
        qq_ref[g, :t] = jnp.where(first, q, 0.0).astype(BF16)
        qq_ref[g, t:] = jnp.where(first, 0.0, q).astype(BF16)

    m_ref[...] = jnp.full_like(m_ref, -jnp.inf)
    l_ref[...] = jnp.zeros_like(l_ref)
    acc_ref[...] = jnp.zeros_like(acc_ref)

    def tile(ki, near):
        off = pl.multiple_of(ki * t, t)
        qc = MXU_DIM
        depth = SCORES_AHEAD
        items = [(g, c) for g in range(n_g) for c in range(2 * t // qc)]

        def scores(g, c):
            return _dot_nt(k_ref[pl.ds(off, t), g * hd:(g + 1) * hd], qq_ref[g, c * qc:(c + 1) * qc])

        def softmax(g, c, s):
            cols = slice(c * qc, (c + 1) * qc)
            if near is not None:
                bc = (c * qc) % t
                s = bias_ref[g, near, :, bc:bc + qc] + s
            mx = jnp.max(s, axis=0, keepdims=True)
            m_prev = m_ref[g, :, cols]
            if near is None:
                far = bvec_ref[g, 2][:, :qc]
                m_new = jnp.maximum(m_prev, mx + far)
                sub = m_new - far
            else:
                m_new = jnp.maximum(m_prev, mx)
                sub = m_new
            alpha = jnp.exp2(m_prev - m_new)
            p = jnp.exp2(s - sub)
            l_ref[g, :, cols] = alpha * l_ref[g, :, cols] + jnp.sum(p, axis=0, keepdims=True)
            m_ref[g, :, cols] = m_new
            return alpha, p.astype(BF16)

        def weigh(g, c, alpha, p):
            cols = slice(c * qc, (c + 1) * qc)
            acc_ref[g, :, cols] = alpha * acc_ref[g, :, cols] + _dot(vt_ref[g, ki], p)

        pending = [scores(*items[i]) for i in range(min(depth, len(items)))]
        for i, (g, c) in enumerate(items):
            if i + depth < len(items):
                pending.append(scores(*items[i + depth]))
            weigh(g, c, *softmax(g, c, pending[i]))
            pending[i] = None

    def far_body(ki, carry):
        tile(ki, None)
        return carry

    lax.fori_loop(0, jnp.maximum(qi - 1, 0), far_body, 0)

    @pl.when(qi >= 1)
    def _():
        tile(qi - 1, 1)

    tile(qi, 0)

    lv = lamv_ref[...]
    lam = (jnp.exp(jnp.sum(lv[0:1] * lv[1:2], axis=-1, keepdims=True))
           - jnp.exp(jnp.sum(lv[2:3] * lv[3:4], axis=-1, keepdims=True)) + lam_init)
    for g in range(n_g):
        o = acc_ref[g] / l_ref[g]
        out = o[:, :t] - lam * o[:, t:]
        scale = lax.rsqrt(jnp.mean(out * out, axis=0, keepdims=True) + EPS)
        y = out * scale * gain_ref[...] * (1.0 - lam_init)
        o_ref[:, g * hd:(g + 1) * hd] = y.T.astype(o_ref.dtype)


def _diff_attn(proj, bias_vecs, lam_vecs, sub_gain, *, batch, seq, n_heads, col0, t, n_g, lam_init):
    nq = seq // t
    hd = HEAD_DIM
    assert n_heads % n_g == 0 and col0 % n_g == 0
    ng_blocks = n_heads // n_g
    c0 = col0 // n_g
    w = n_g * hd
    vmem = (4 * t * w * 2 + 4 * seq * w * 2 + n_g * 2 * t * t * 4 + n_g * 2 * t * hd * 2
            + n_g * seq * hd * 2 + n_g * 2 * t * hd * 4 + n_g * 6 * 2 * t * t * 4)
    kern = functools.partial(_diff_attn_kernel, t=t, n_g=n_g, lam_init=lam_init)
    return pl.pallas_call(
        kern,
        grid=(batch, ng_blocks, nq),
        in_specs=[
            pl.BlockSpec((t, w), lambda b, h, i: (b * nq + i, c0 + h)),
            pl.BlockSpec((seq, w), lambda b, h, i: (b, c0 + ng_blocks + h)),
            pl.BlockSpec((seq, w), lambda b, h, i: (b, c0 + 2 * ng_blocks + h)),
            pl.BlockSpec((n_g, 3, 1, 2 * t), lambda b, h, i: (h, 0, 0, 0)),
            pl.BlockSpec((4, DIFF_QK_DIM), lambda b, h, i: (0, 0)),
            pl.BlockSpec((hd, 1), lambda b, h, i: (0, 0)),
        ],
        out_specs=pl.BlockSpec((t, w), lambda b, h, i: (b * nq + i, h)),
        out_shape=jax.ShapeDtypeStruct((batch * seq, n_heads * hd), BF16),
        scratch_shapes=[
            pltpu.VMEM((n_g, 2, t, t), F32),
            pltpu.VMEM((n_g, seq // t, hd, t), BF16),
            pltpu.VMEM((n_g, 2 * t, hd), BF16),
            pltpu.VMEM((n_g, 1, 2 * t), F32),
            pltpu.VMEM((n_g, 1, 2 * t), F32),
            pltpu.VMEM((n_g, hd, 2 * t), F32),
        ],
        compiler_params=_params(3, vmem),
        name="diff_attn",
    )(proj, proj, proj, bias_vecs, lam_vecs, sub_gain.reshape(hd, 1))


def _sb_attn_kernel(q_ref, k_ref, v_ref, tri_ref, gain_ref, o_ref, carry_ref, acc_ref, *, t, cw, n_g):
    qi = pl.program_id(2)
    hd = HEAD_DIM

    def tile(ki, diag):
        off = pl.multiple_of(ki * t, t)
        if diag:
            row = lax.broadcasted_iota(jnp.int32, (t, t), 0)
            col = lax.broadcasted_iota(jnp.int32, (t, t), 1)
            strict = col < row
        head_cols = [slice(g * hd, (g + 1) * hd) for g in range(n_g)]
        tri = tri_ref[...]

        def scores(g):
            return _dot_nt(q_ref[:, head_cols[g]], k_ref[pl.ds(off, t), head_cols[g]])

        def logits(g, z):
            neg_abs = lax.bitcast_convert_type(
                lax.bitcast_convert_type(z, jnp.uint32) | jnp.uint32(0x80000000), F32)
            neg_soft = jnp.log(1.0 + jnp.exp2(neg_abs)) * (-LOG2E)
            log_keep = neg_soft - jnp.maximum(z, 0.0)
            log_beta = log_keep + z
            if diag:
                log_keep = jnp.where(strict, log_keep, 0.0)
            run = None if diag else carry_ref[g]
            after = [None] * (t // cw)
            for c in reversed(range(t // cw)):
                lk = log_keep[:, c * cw:(c + 1) * cw]
                cs = _dot(lk.astype(BF16), tri)
                after[c] = cs if run is None else _widen(run, cw) + cs
                row_sum = jnp.sum(lk, axis=-1, keepdims=True)
                run = jnp.broadcast_to(row_sum, (t, LANES)) if run is None else run + row_sum
            carry_ref[g] = run
            return log_beta, jnp.concatenate(after, axis=1)

        def mix(g, log_beta, after):
            w = jnp.exp2(log_beta + after)
            if diag:
                w = jnp.where(strict, w, 0.0)
            pv = _dot(w.astype(BF16), v_ref[pl.ds(off, t), head_cols[g]])
            if diag:
                acc_ref[g] = pv
            else:
                acc_ref[g] += pv

        z_next = scores(0)
        for g in range(n_g):
            z = z_next
            if g + 1 < n_g:
                z_next = scores(g + 1)
            mix(g, *logits(g, z))

    tile(qi, True)

    def not_done(state):
        j, underflowed = state
        return jnp.logical_and(j < qi, jnp.logical_not(underflowed))

    def body(state):
        j, _ = state
        tile(qi - 1 - j, False)
        largest = jnp.max(carry_ref[0])
        for g in range(1, n_g):
            largest = jnp.maximum(largest, jnp.max(carry_ref[g]))
        return j + 1, largest < F32_UNDERFLOW_LOG2

    lax.while_loop(not_done, body, (jnp.int32(0), jnp.bool_(False)))

    for g in range(n_g):
        out = acc_ref[g]
        o_ref[:, g * hd:(g + 1) * hd] = (out * _rms_scale(out) * gain_ref[...]).astype(o_ref.dtype)


def _sb_attn(proj, out_gain, *, batch, seq, n_heads, col0, t, n_g):
    nq = seq // t
    hd = HEAD_DIM
    assert n_heads % n_g == 0 and col0 % n_g == 0
    ng_blocks = n_heads // n_g
    c0 = col0 // n_g
    wd = n_g * hd
    cw = min(t, MXU_DIM)
    tri = jnp.asarray(np.tril(np.ones((cw, cw), np.float32), -1), BF16)
    vmem = (4 * t * wd * 2 + 4 * seq * wd * 2 + 2 * cw * cw * 2
            + n_g * (t * hd * 4 + t * LANES * 4) + n_g * 8 * t * t * 4)
    kern = functools.partial(_sb_attn_kernel, t=t, cw=cw, n_g=n_g)
    return pl.pallas_call(
        kern,
        grid=(batch, ng_blocks, nq),
        in_specs=[
            pl.BlockSpec((t, wd), lambda b, h, i: (b * nq + i, c0 + h)),
            pl.BlockSpec((seq, wd), lambda b, h, i: (b, c0 + ng_blocks + h)),
            pl.BlockSpec((seq, wd), lambda b, h, i: (b, c0 + 2 * ng_blocks + h)),
            pl.BlockSpec((cw, cw), lambda b, h, i: (0, 0)),
            pl.BlockSpec((1, hd), lambda b, h, i: (0, 0)),
        ],
        out_specs=pl.BlockSpec((t, wd), lambda b, h, i: (b * nq + i, h)),
        out_shape=jax.ShapeDtypeStruct((batch * seq, n_heads * hd), BF16),
        scratch_shapes=[
            pltpu.VMEM((n_g, t, LANES), F32),
            pltpu.VMEM((n_g, t, hd), F32),
        ],
        compiler_params=_params(3, vmem),
        name="sb_attn",
    )(proj, proj, proj, tri, out_gain.reshape(1, hd))


def _mem_attn_kernel(q_ref, k_ref, v_ref, o_ref, *, n_heads, dh):
    scale = dh ** -0.5
    for h in range(n_heads):
        cols = slice(h * dh, (h + 1) * dh)
        s = _dot_nt(q_ref[:, cols], k_ref[:, cols]) * scale
        p = jnp.exp(s - jnp.max(s, axis=-1, keepdims=True))
        denom = jnp.sum(p, axis=-1, keepdims=True)
        o = _dot(p.astype(BF16), v_ref[:, cols]) / denom
        o_ref[:, cols] = o.astype(o_ref.dtype)


def _mem_attn(q, kv, *, batch, seq, n_mem, tq):
    d = q.shape[1]
    nq = seq // tq
    vmem = 4 * tq * d * 2 + 4 * n_mem * d * 2 + 8 * tq * n_mem * 4 + 2 * tq * d * 4
    kern = functools.partial(_mem_attn_kernel, n_heads=N_HEADS_MEM, dh=d // N_HEADS_MEM)
    return pl.pallas_call(
        kern,
        grid=(batch, nq),
        in_specs=[
            pl.BlockSpec((tq, d), lambda b, i: (b * nq + i, 0)),
            pl.BlockSpec((n_mem, d), lambda b, i: (b, 0)),
            pl.BlockSpec((n_mem, d), lambda b, i: (b, 1)),
        ],
        out_specs=pl.BlockSpec((tq, d), lambda b, i: (b * nq + i, 0)),
        out_shape=jax.ShapeDtypeStruct((batch * seq, d), BF16),
        compiler_params=_params(2, vmem),
        name="mem_attn",
    )(q, kv, kv)


def _lambda_init(layer_idx):
    return 0.8 - 0.6 * math.exp(-0.3 * layer_idx)


def _pick(n, prefs):
    for p in prefs:
        if n % p == 0:
            return p
    return n


def kernel(x, mem, w_in, w_out, rel_bias, lambda_q1, lambda_k1, lambda_q2, lambda_k2, diff_sub_gain, sb_gain, g_mix_pre, g_mix_post, w_mq, w_mkv, w_mo, g_mem_kv, g_mem_pre, g_mem_post, w_gate_up, w_down, g_ffn_pre, g_ffn_post):
    batch, seq, d_model = x.shape
    n_mem = mem.shape[1]
    depth = w_in.shape[0]
    width = w_out.shape[1] // 2
    n_heads = width // HEAD_DIM
    d_ff = w_down.shape[1]
    t = _pick(seq, (512, 256))
    assert w_in.shape[2] == 6 * width and t % CHUNK == 0
    assert _far_bias_is_constant(t, seq)
    n_g = _pick(n_heads, (HEADS_PER_STEP,))

    rows = batch * seq
    tm = _pick(rows, (512, 256, 128))
    tm_n = _pick(rows, (1024, 512, 256, 128))
    xf = x.reshape(rows, d_model)
    memf = mem.reshape(batch * n_mem, d_model)
    ones = lambda n: jnp.ones((n,), F32)

    col_scale = jnp.concatenate([
        jnp.full((width,), DIFF_QK_DIM ** -0.5 * LOG2E, F32), ones(2 * width),
        jnp.full((width,), HEAD_DIM ** -0.5 * LOG2E, F32), ones(2 * width)])

    for l in range(depth):
        lam_init = _lambda_init(l)
        proj = _norm_matmul(xf, g_mix_pre[l], w_in[l].astype(BF16), col_scale,
                            tm=tm_n, tn=_pick(6 * width, (1536, 1024, 512, 256, 128)))
        lam_vecs = jnp.stack([lambda_q1[l], lambda_k1[l], lambda_q2[l], lambda_k2[l]]).astype(F32)
        out_a = _diff_attn(proj, _diff_bias_vectors(rel_bias, t), lam_vecs, diff_sub_gain[l],
                           batch=batch, seq=seq, n_heads=n_heads, col0=0, t=t, n_g=n_g, lam_init=lam_init)
        out_b = _sb_attn(proj, sb_gain[l], batch=batch, seq=seq, n_heads=n_heads,
                         col0=3 * n_heads, t=t, n_g=n_g)
        xf = _matmul2_norm_resid(out_a, out_b, w_out[l].astype(BF16), xf, g_mix_post[l], tm=tm)
        q_mem = _norm_matmul(xf, g_mem_pre[l], w_mq[l].astype(BF16), ones(d_model),
                             tm=tm_n, tn=_pick(d_model, (2048, 1024, 512, 256, 128)))
        kv = _norm_matmul(memf, g_mem_kv[l], w_mkv[l].astype(BF16), ones(2 * d_model),
                          tm=_pick(batch * n_mem, (512, 256, 128)),
                          tn=_pick(2 * d_model, (2048, 1024, 512, 256, 128)))
        o_mem = _mem_attn(q_mem, kv, batch=batch, seq=seq, n_mem=n_mem,
                          tq=_pick(seq, (512, 256, 128)))
        xf = _matmul_norm_resid(o_mem, w_mo[l].astype(BF16), xf, g_mem_post[l],
                                tm=tm, tk=_pick(d_model, (2048, 1024, 512, 256, 128)))
        act = _norm_swiglu(xf, g_ffn_pre[l], w_gate_up[l].astype(BF16),
                           tm=tm_n, tn=_pick(d_ff, (512, 256, 128)))
        xf = _matmul_norm_resid(act, w_down[l].astype(BF16), xf, g_ffn_post[l],
                                tm=tm, tk=_pick(d_ff, (2816, 1408, 512, 256, 128)))
    return xf.reshape(batch, seq, d_model)
```

```python
import functools
import math

import numpy as np
import jax
import jax.numpy as jnp
from jax import lax
from jax.experimental import pallas as pl
from jax.experimental.pallas import tpu as pltpu

CHUNK = 64
HEAD_DIM = 128
DIFF_QK_DIM = HEAD_DIM // 2
N_HEADS_MEM = 4
N_BUCKETS = 32
MAX_DISTANCE = 128
EPS = 1e-6
LOG2E = math.log2(math.e)
F32_UNDERFLOW_LOG2 = -160.0

V7X_VMEM_BYTES = 64 * 1024 * 1024
LANES = 128
MXU_DIM = 256
HEADS_PER_STEP = 4
SCORES_AHEAD = 4
VMEM_CAP_BYTES = V7X_VMEM_BYTES - 8 * 1024 * 1024

F32 = jnp.float32
BF16 = jnp.bfloat16


def _vmem_limit(estimate_bytes):
    return int(min(VMEM_CAP_BYTES, max(estimate_bytes, 16 * 1024 * 1024)))


def _params(n_axes, vmem_bytes):
    return pltpu.CompilerParams(
        dimension_semantics=("arbitrary",) * n_axes,
        vmem_limit_bytes=_vmem_limit(vmem_bytes))


def _rms_scale(v):
    return lax.rsqrt(jnp.mean(v * v, axis=-1, keepdims=True) + EPS)


def _dot(a, b):
    return jnp.dot(a, b, preferred_element_type=F32)


def _dot_nt(a, b):
    return lax.dot_general(a, b, (((1,), (1,)), ((), ())), preferred_element_type=F32)


def _norm_matmul_kernel(x_ref, g_ref, w_ref, cs_ref, o_ref, h_ref):
    @pl.when(pl.program_id(1) == 0)
    def _():
        x = x_ref[...]
        h_ref[...] = (x * _rms_scale(x) * g_ref[...]).astype(BF16)

    acc = _dot(h_ref[...], w_ref[...])
    o_ref[...] = (acc * cs_ref[...]).astype(o_ref.dtype)


def _norm_matmul(x, gain, w, col_scale, *, tm, tn):
    m, d = x.shape
    n = w.shape[1]
    assert m % tm == 0 and n % tn == 0
    vmem = 2 * tm * d * 4 + tm * d * 2 + 2 * d * tn * 2 + 2 * tm * tn * 2 + 2 * tm * tn * 4
    return pl.pallas_call(
        _norm_matmul_kernel,
        grid=(m // tm, n // tn),
        in_specs=[
            pl.BlockSpec((tm, d), lambda i, j: (i, 0)),
            pl.BlockSpec((1, d), lambda i, j: (0, 0)),
            pl.BlockSpec((d, tn), lambda i, j: (0, j)),
            pl.BlockSpec((1, tn), lambda i, j: (0, j)),
        ],
        out_specs=pl.BlockSpec((tm, tn), lambda i, j: (i, j)),
        out_shape=jax.ShapeDtypeStruct((m, n), BF16),
        scratch_shapes=[pltpu.VMEM((tm, d), BF16)],
        compiler_params=_params(2, vmem),
        name="norm_matmul",
    )(x, gain.reshape(1, d), w, col_scale.reshape(1, n))


def _norm_swiglu_kernel(x_ref, g_ref, wg_ref, wu_ref, o_ref, h_ref):
    @pl.when(pl.program_id(1) == 0)
    def _():
        x = x_ref[...]
        h_ref[...] = (x * _rms_scale(x) * g_ref[...]).astype(BF16)

    h = h_ref[...]
    gate = _dot(h, wg_ref[...])
    up = _dot(h, wu_ref[...])
    o_ref[...] = (gate * (1.0 / (1.0 + jnp.exp(-gate))) * up).astype(o_ref.dtype)


def _norm_swiglu(x, gain, w_gate_up, *, tm, tn):
    m, d = x.shape
    f = w_gate_up.shape[1] // 2
    assert m % tm == 0 and f % tn == 0
    nf = f // tn
    vmem = 2 * tm * d * 4 + tm * d * 2 + 4 * d * tn * 2 + 2 * tm * tn * 2 + 4 * tm * tn * 4
    return pl.pallas_call(
        _norm_swiglu_kernel,
        grid=(m // tm, nf),
        in_specs=[
            pl.BlockSpec((tm, d), lambda i, j: (i, 0)),
            pl.BlockSpec((1, d), lambda i, j: (0, 0)),
            pl.BlockSpec((d, tn), lambda i, j: (0, j)),
            pl.BlockSpec((d, tn), lambda i, j: (0, j + nf)),
        ],
        out_specs=pl.BlockSpec((tm, tn), lambda i, j: (i, j)),
        out_shape=jax.ShapeDtypeStruct((m, f), BF16),
        scratch_shapes=[pltpu.VMEM((tm, d), BF16)],
        compiler_params=_params(2, vmem),
        name="norm_swiglu",
    )(x, gain.reshape(1, d), w_gate_up, w_gate_up)


def _matmul_norm_resid_kernel(l_ref, w_ref, r_ref, g_ref, o_ref, acc_ref):
    k = pl.program_id(1)

    @pl.when(k == 0)
    def _():
        acc_ref[...] = jnp.zeros_like(acc_ref)

    acc_ref[...] += _dot(l_ref[...], w_ref[...])

    @pl.when(k == pl.num_programs(1) - 1)
    def _():
        y = acc_ref[...]
        o_ref[...] = r_ref[...] + y * _rms_scale(y) * g_ref[...]


def _matmul_norm_resid(lhs, w, resid, gain, *, tm, tk):
    m, kdim = lhs.shape
    d = w.shape[1]
    assert m % tm == 0 and kdim % tk == 0
    vmem = 2 * tm * tk * 2 + 2 * tk * d * 2 + 4 * tm * d * 4 + 2 * tm * d * 4
    return pl.pallas_call(
        _matmul_norm_resid_kernel,
        grid=(m // tm, kdim // tk),
        in_specs=[
            pl.BlockSpec((tm, tk), lambda i, k: (i, k)),
            pl.BlockSpec((tk, d), lambda i, k: (k, 0)),
            pl.BlockSpec((tm, d), lambda i, k: (i, 0)),
            pl.BlockSpec((1, d), lambda i, k: (0, 0)),
        ],
        out_specs=pl.BlockSpec((tm, d), lambda i, k: (i, 0)),
        out_shape=jax.ShapeDtypeStruct((m, d), F32),
        scratch_shapes=[pltpu.VMEM((tm, d), F32)],
        compiler_params=_params(2, vmem),
        name="matmul_norm_resid",
    )(lhs, w, resid, gain.reshape(1, d))


def _matmul2_norm_resid_kernel(a_ref, b_ref, w_ref, r_ref, g_ref, o_ref):
    ka = a_ref.shape[1]
    y = _dot(a_ref[...], w_ref[:ka, :]) + _dot(b_ref[...], w_ref[ka:, :])
    o_ref[...] = r_ref[...] + y * _rms_scale(y) * g_ref[...]


def _matmul2_norm_resid(lhs_a, lhs_b, w, resid, gain, *, tm):
    m, ka = lhs_a.shape
    kb = lhs_b.shape[1]
    d = w.shape[1]
    assert m % tm == 0 and w.shape[0] == ka + kb
    vmem = 2 * tm * (ka + kb) * 2 + 2 * (ka + kb) * d * 2 + 4 * tm * d * 4 + 2 * tm * d * 4
    return pl.pallas_call(
        _matmul2_norm_resid_kernel,
        grid=(m // tm,),
        in_specs=[
            pl.BlockSpec((tm, ka), lambda i: (i, 0)),
            pl.BlockSpec((tm, kb), lambda i: (i, 0)),
            pl.BlockSpec((ka + kb, d), lambda i: (0, 0)),
            pl.BlockSpec((tm, d), lambda i: (i, 0)),
            pl.BlockSpec((1, d), lambda i: (0, 0)),
        ],
        out_specs=pl.BlockSpec((tm, d), lambda i: (i, 0)),
        out_shape=jax.ShapeDtypeStruct((m, d), F32),
        compiler_params=_params(1, vmem),
        name="matmul2_norm_resid",
    )(lhs_a, lhs_b, w, resid, gain.reshape(1, d))


def _t5_bucket(rel):
    nb = N_BUCKETS // 2
    ret = jnp.where(rel > 0, nb, 0)
    n = jnp.abs(rel)
    max_exact = nb // 2
    nf = jnp.maximum(n, 1).astype(F32)
    large = max_exact + (jnp.log(nf / max_exact) / math.log(MAX_DISTANCE / max_exact)
                         * (nb - max_exact)).astype(jnp.int32)
    large = jnp.minimum(large, nb - 1)
    return ret + jnp.where(n < max_exact, n, large)


def _far_bias_is_constant(t, seq):
    n = np.arange(t + 1, max(seq, t + 2), dtype=np.float64)
    nb, me = N_BUCKETS // 2, N_BUCKETS // 4
    b = np.minimum(me + (np.log(n / me) / math.log(MAX_DISTANCE / me) * (nb - me)).astype(np.int64), nb - 1)
    return bool(np.all(b == nb - 1)) and (t + 1) >= 2 * MAX_DISTANCE


def _diff_bias_vectors(rel_bias, t):
    j = jnp.arange(2 * t, dtype=jnp.int32)
    rel0 = jnp.where(j < t, -j, 2 * t - j)
    vecs = [rel_bias[_t5_bucket(rel0 - d * t)].astype(F32) for d in range(3)]
    out = jnp.transpose(jnp.stack(vecs, axis=0), (2, 0, 1)) * LOG2E
    return out[:, :, None, :]


def _widen(col, width):
    return jnp.concatenate([col] * (width // LANES), axis=1)


def _diff_attn_kernel(q_ref, k_ref, v_ref, bvec_ref, lamv_ref, gain_ref, o_ref,
                      bias_ref, vt_ref, qq_ref, s_ref, m_ref, l_ref, acc_ref, *, t, n_g, lam_init):
    qi = pl.program_id(2)
    hd = HEAD_DIM
    seq = k_ref.shape[0]

    @pl.when(qi == 0)
    def _():
        key = lax.broadcasted_iota(jnp.int32, (t, t), 0)
        qry = lax.broadcasted_iota(jnp.int32, (t, t), 1)
        shift = CHUNK.bit_length() - 1
        allowed = lax.shift_right_logical(key, shift) <= lax.shift_right_logical(qry, shift)
        for g in range(n_g):
            for d in range(2):
                band = jnp.broadcast_to(bvec_ref[g, d], (t, 2 * t))
                tile_b = pltpu.roll(band, 0, 1, stride=1, stride_axis=0)[:, :t]
                bias_ref[g, d] = jnp.where(allowed, tile_b, -jnp.inf) if d == 0 else tile_b
            for c in range(seq // t):
                blk = v_ref[c * t:(c + 1) * t, g * hd:(g + 1) * hd].astype(F32)
                vt_ref[g, c] = blk.T.astype(BF16)

    lane = lax.broadcasted_iota(jnp.int32, (t, hd), 1)
    first = lane < DIFF_QK_DIM
    for g in range(n_g):
        q = q_ref[:, g * hd:(g + 1) * hd].astype(F32)
        qq_ref[g, :t] = jnp.where(first, q, 0.0).astype(BF16)
        qq_ref[g, t:] = jnp.where(first, 0.0, q).astype(BF16)

    m_ref[...] = jnp.full_like(m_ref, -jnp.inf)
    l_ref[...] = jnp.zeros_like(l_ref)
    acc_ref[...] = jnp.zeros_like(acc_ref)

    def tile(ki, near):
        off = pl.multiple_of(ki * t, t)
        qc = MXU_DIM
        depth = SCORES_AHEAD
        items = [(g, c) for g in range(n_g) for c in range(2 * t // qc)]

        def n_keys(c):
            return (c * qc) % t + qc if near == 0 else t

        def scores(n, g, c):
            s_ref[n % (depth + 1), :n_keys(c)] = _dot_nt(k_ref[pl.ds(off, n_keys(c)), g * hd:(g + 1) * hd],
                                                         qq_ref[g, c * qc:(c + 1) * qc])

        def softmax(n, g, c):
            cols = slice(c * qc, (c + 1) * qc)
            s = s_ref[n % (depth + 1), :n_keys(c)]
            if near is not None:
                bc = (c * qc) % t
                s = bias_ref[g, near, :n_keys(c), bc:bc + qc] + s
            mx = jnp.max(s, axis=0, keepdims=True)
            m_prev = m_ref[g, :, cols]
            if near is None:
                far = bvec_ref[g, 2][:, :qc]
                m_new = jnp.maximum(m_prev, mx + far)
                sub = m_new - far
            else:
                m_new = jnp.maximum(m_prev, mx)
                sub = m_new
            alpha = jnp.exp2(m_prev - m_new)
            p = jnp.exp2(s - sub)
            l_ref[g, :, cols] = alpha * l_ref[g, :, cols] + jnp.sum(p, axis=0, keepdims=True)
            m_ref[g, :, cols] = m_new
            return alpha, p.astype(BF16)

        def weigh(g, c, alpha, p):
            cols = slice(c * qc, (c + 1) * qc)
            acc_ref[g, :, cols] = alpha * acc_ref[g, :, cols] + _dot(vt_ref[g, ki][:, :n_keys(c)], p)

        for n in range(min(depth, len(items))):
            scores(n, *items[n])
        for n, (g, c) in enumerate(items):
            if n + depth < len(items):
                scores(n + depth, *items[n + depth])
            weigh(g, c, *softmax(n, g, c))

    def far_body(ki, carry):
        tile(ki, None)
        return carry

    lax.fori_loop(0, jnp.maximum(qi - 1, 0), far_body, 0)

    @pl.when(qi >= 1)
    def _():
        tile(qi - 1, 1)

    tile(qi, 0)

    lv = lamv_ref[...]
    lam = (jnp.exp(jnp.sum(lv[0:1] * lv[1:2], axis=-1, keepdims=True))
           - jnp.exp(jnp.sum(lv[2:3] * lv[3:4], axis=-1, keepdims=True)) + lam_init)
    for g in range(n_g):
        o = acc_ref[g] / l_ref[g]
        out = o[:, :t] - lam * o[:, t:]
        scale = lax.rsqrt(jnp.mean(out * out, axis=0, keepdims=True) + EPS)
        y = out * scale * gain_ref[...] * (1.0 - lam_init)
        o_ref[:, g * hd:(g + 1) * hd] = y.T.astype(o_ref.dtype)


def _diff_attn(proj, bias_vecs, lam_vecs, sub_gain, *, batch, seq, n_heads, col0, t, n_g, lam_init):
    nq = seq // t
    hd = HEAD_DIM
    assert n_heads % n_g == 0 and col0 % n_g == 0
    ng_blocks = n_heads // n_g
    c0 = col0 // n_g
    w = n_g * hd
    vmem = (4 * t * w * 2 + 4 * seq * w * 2 + n_g * 2 * t * t * 4 + n_g * 2 * t * hd * 2
            + n_g * seq * hd * 2 + n_g * 2 * t * hd * 4 + n_g * 6 * 2 * t * t * 4)
    kern = functools.partial(_diff_attn_kernel, t=t, n_g=n_g, lam_init=lam_init)
    return pl.pallas_call(
        kern,
        grid=(batch, ng_blocks, nq),
        in_specs=[
            pl.BlockSpec((t, w), lambda b, h, i: (b * nq + i, c0 + h)),
            pl.BlockSpec((seq, w), lambda b, h, i: (b, c0 + ng_blocks + h)),
            pl.BlockSpec((seq, w), lambda b, h, i: (b, c0 + 2 * ng_blocks + h)),
            pl.BlockSpec((n_g, 3, 1, 2 * t), lambda b, h, i: (h, 0, 0, 0)),
            pl.BlockSpec((4, DIFF_QK_DIM), lambda b, h, i: (0, 0)),
            pl.BlockSpec((hd, 1), lambda b, h, i: (0, 0)),
        ],
        out_specs=pl.BlockSpec((t, w), lambda b, h, i: (b * nq + i, h)),
        out_shape=jax.ShapeDtypeStruct((batch * seq, n_heads * hd), BF16),
        scratch_shapes=[
            pltpu.VMEM((n_g, 2, t, t), F32),
            pltpu.VMEM((n_g, seq // t, hd, t), BF16),
            pltpu.VMEM((n_g, 2 * t, hd), BF16),
            pltpu.VMEM((SCORES_AHEAD + 1, t, MXU_DIM), F32),
            pltpu.VMEM((n_g, 1, 2 * t), F32),
            pltpu.VMEM((n_g, 1, 2 * t), F32),
            pltpu.VMEM((n_g, hd, 2 * t), F32),
        ],
        compiler_params=_params(3, vmem),
        name="diff_attn",
    )(proj, proj, proj, bias_vecs, lam_vecs, sub_gain.reshape(hd, 1))


def _sb_attn_kernel(q_ref, k_ref, v_ref, tri_ref, gain_ref, o_ref, carry_ref, acc_ref, *, t, cw, n_g):
    qi = pl.program_id(2)
    hd = HEAD_DIM

    def tile(ki, diag):
        off = pl.multiple_of(ki * t, t)
        if diag:
            row = lax.broadcasted_iota(jnp.int32, (t, t), 0)
            col = lax.broadcasted_iota(jnp.int32, (t, t), 1)
            strict = col < row
        head_cols = [slice(g * hd, (g + 1) * hd) for g in range(n_g)]
        tri = tri_ref[...]

        def scores(g):
            return _dot_nt(q_ref[:, head_cols[g]], k_ref[pl.ds(off, t), head_cols[g]])

        def logits(g, z):
            neg_abs = lax.bitcast_convert_type(
                lax.bitcast_convert_type(z, jnp.uint32) | jnp.uint32(0x80000000), F32)
            neg_soft = jnp.log(1.0 + jnp.exp2(neg_abs)) * (-LOG2E)
            log_keep = neg_soft - jnp.maximum(z, 0.0)
            log_beta = log_keep + z
            if diag:
                log_keep = jnp.where(strict, log_keep, 0.0)
            run = None if diag else carry_ref[g]
            after = [None] * (t // cw)
            for c in reversed(range(t // cw)):
                lk = log_keep[:, c * cw:(c + 1) * cw]
                cs = _dot(lk.astype(BF16), tri)
                after[c] = cs if run is None else _widen(run, cw) + cs
                row_sum = jnp.sum(lk, axis=-1, keepdims=True)
                run = jnp.broadcast_to(row_sum, (t, LANES)) if run is None else run + row_sum
            carry_ref[g] = run
            return log_beta, jnp.concatenate(after, axis=1)

        def mix(g, log_beta, after):
            w = jnp.exp2(log_beta + after)
            if diag:
                w = jnp.where(strict, w, 0.0)
            pv = _dot(w.astype(BF16), v_ref[pl.ds(off, t), head_cols[g]])
            if diag:
                acc_ref[g] = pv
            else:
                acc_ref[g] += pv

        z_next = scores(0)
        for g in range(n_g):
            z = z_next
            if g + 1 < n_g:
                z_next = scores(g + 1)
            mix(g, *logits(g, z))

    tile(qi, True)

    def not_done(state):
        j, underflowed = state
        return jnp.logical_and(j < qi, jnp.logical_not(underflowed))

    def body(state):
        j, _ = state
        tile(qi - 1 - j, False)
        largest = jnp.max(carry_ref[0])
        for g in range(1, n_g):
            largest = jnp.maximum(largest, jnp.max(carry_ref[g]))
        return j + 1, largest < F32_UNDERFLOW_LOG2

    lax.while_loop(not_done, body, (jnp.int32(0), jnp.bool_(False)))

    for g in range(n_g):
        out = acc_ref[g]
        o_ref[:, g * hd:(g + 1) * hd] = (out * _rms_scale(out) * gain_ref[...]).astype(o_ref.dtype)


def _sb_attn(proj, out_gain, *, batch, seq, n_heads, col0, t, n_g):
    nq = seq // t
    hd = HEAD_DIM
    assert n_heads % n_g == 0 and col0 % n_g == 0
    ng_blocks = n_heads // n_g
    c0 = col0 // n_g
    wd = n_g * hd
    cw = min(t, MXU_DIM)
    tri = jnp.asarray(np.tril(np.ones((cw, cw), np.float32), -1), BF16)
    vmem = (4 * t * wd * 2 + 4 * seq * wd * 2 + 2 * cw * cw * 2
            + n_g * (t * hd * 4 + t * LANES * 4) + n_g * 8 * t * t * 4)
    kern = functools.partial(_sb_attn_kernel, t=t, cw=cw, n_g=n_g)
    return pl.pallas_call(
        kern,
        grid=(batch, ng_blocks, nq),
        in_specs=[
            pl.BlockSpec((t, wd), lambda b, h, i: (b * nq + i, c0 + h)),
            pl.BlockSpec((seq, wd), lambda b, h, i: (b, c0 + ng_blocks + h)),
            pl.BlockSpec((seq, wd), lambda b, h, i: (b, c0 + 2 * ng_blocks + h)),
            pl.BlockSpec((cw, cw), lambda b, h, i: (0, 0)),
            pl.BlockSpec((1, hd), lambda b, h, i: (0, 0)),
        ],
        out_specs=pl.BlockSpec((t, wd), lambda b, h, i: (b * nq + i, h)),
        out_shape=jax.ShapeDtypeStruct((batch * seq, n_heads * hd), BF16),
        scratch_shapes=[
            pltpu.VMEM((n_g, t, LANES), F32),
            pltpu.VMEM((n_g, t, hd), F32),
        ],
        compiler_params=_params(3, vmem),
        name="sb_attn",
    )(proj, proj, proj, tri, out_gain.reshape(1, hd))


def _mem_attn_kernel(q_ref, k_ref, v_ref, o_ref, *, n_heads, dh):
    scale = dh ** -0.5
    for h in range(n_heads):
        cols = slice(h * dh, (h + 1) * dh)
        s = _dot_nt(q_ref[:, cols], k_ref[:, cols]) * scale
        p = jnp.exp(s - jnp.max(s, axis=-1, keepdims=True))
        denom = jnp.sum(p, axis=-1, keepdims=True)
        o = _dot(p.astype(BF16), v_ref[:, cols]) / denom
        o_ref[:, cols] = o.astype(o_ref.dtype)


def _mem_attn(q, kv, *, batch, seq, n_mem, tq):
    d = q.shape[1]
    nq = seq // tq
    vmem = 4 * tq * d * 2 + 4 * n_mem * d * 2 + 8 * tq * n_mem * 4 + 2 * tq * d * 4
    kern = functools.partial(_mem_attn_kernel, n_heads=N_HEADS_MEM, dh=d // N_HEADS_MEM)
    return pl.pallas_call(
        kern,
        grid=(batch, nq),
        in_specs=[
            pl.BlockSpec((tq, d), lambda b, i: (b * nq + i, 0)),
            pl.BlockSpec((n_mem, d), lambda b, i: (b, 0)),
            pl.BlockSpec((n_mem, d), lambda b, i: (b, 1)),
        ],
        out_specs=pl.BlockSpec((tq, d), lambda b, i: (b * nq + i, 0)),
        out_shape=jax.ShapeDtypeStruct((batch * seq, d), BF16),
        compiler_params=_params(2, vmem),
        name="mem_attn",
    )(q, kv, kv)


def _lambda_init(layer_idx):
    return 0.8 - 0.6 * math.exp(-0.3 * layer_idx)


def _pick(n, prefs):
    for p in prefs:
        if n % p == 0:
            return p
    return n


def kernel(x, mem, w_in, w_out, rel_bias, lambda_q1, lambda_k1, lambda_q2, lambda_k2, diff_sub_gain, sb_gain, g_mix_pre, g_mix_post, w_mq, w_mkv, w_mo, g_mem_kv, g_mem_pre, g_mem_post, w_gate_up, w_down, g_ffn_pre, g_ffn_post):
    batch, seq, d_model = x.shape
    n_mem = mem.shape[1]
    depth = w_in.shape[0]
    width = w_out.shape[1] // 2
    n_heads = width // HEAD_DIM
    d_ff = w_down.shape[1]
    t = _pick(seq, (512, 256))
    assert w_in.shape[2] == 6 * width and t % CHUNK == 0
    assert _far_bias_is_constant(t, seq)
    n_g = _pick(n_heads, (HEADS_PER_STEP,))

    rows = batch * seq
    tm = _pick(rows, (512, 256, 128))
    tm_n = _pick(rows, (1024, 512, 256, 128))
    xf = x.reshape(rows, d_model)
    memf = mem.reshape(batch * n_mem, d_model)
    ones = lambda n: jnp.ones((n,), F32)

    col_scale = jnp.concatenate([
        jnp.full((width,), DIFF_QK_DIM ** -0.5 * LOG2E, F32), ones(2 * width),
        jnp.full((width,), HEAD_DIM ** -0.5 * LOG2E, F32), ones(2 * width)])

    for l in range(depth):
        lam_init = _lambda_init(l)
        proj = _norm_matmul(xf, g_mix_pre[l], w_in[l].astype(BF16), col_scale,
                            tm=tm_n, tn=_pick(6 * width, (1536, 1024, 512, 256, 128)))
        lam_vecs = jnp.stack([lambda_q1[l], lambda_k1[l], lambda_q2[l], lambda_k2[l]]).astype(F32)
        out_a = _diff_attn(proj, _diff_bias_vectors(rel_bias, t), lam_vecs, diff_sub_gain[l],
                           batch=batch, seq=seq, n_heads=n_heads, col0=0, t=t, n_g=n_g, lam_init=lam_init)
        out_b = _sb_attn(proj, sb_gain[l], batch=batch, seq=seq, n_heads=n_heads,
                         col0=3 * n_heads, t=t, n_g=n_g)
        xf = _matmul2_norm_resid(out_a, out_b, w_out[l].astype(BF16), xf, g_mix_post[l], tm=tm)
        q_mem = _norm_matmul(xf, g_mem_pre[l], w_mq[l].astype(BF16), ones(d_model),
                             tm=tm_n, tn=_pick(d_model, (2048, 1024, 512, 256, 128)))
        kv = _norm_matmul(memf, g_mem_kv[l], w_mkv[l].astype(BF16), ones(2 * d_model),
                          tm=_pick(batch * n_mem, (512, 256, 128)),
                          tn=_pick(2 * d_model, (2048, 1024, 512, 256, 128)))
        o_mem = _mem_attn(q_mem, kv, batch=batch, seq=seq, n_mem=n_mem,
                          tq=_pick(seq, (512, 256, 128)))
        xf = _matmul_norm_resid(o_mem, w_mo[l].astype(BF16), xf, g_mem_post[l],
                                tm=tm, tk=_pick(d_model, (2048, 1024, 512, 256, 128)))
        act = _norm_swiglu(xf, g_ffn_pre[l], w_gate_up[l].astype(BF16),
                           tm=tm_n, tn=_pick(d_ff, (512, 256, 128)))
        xf = _matmul_norm_resid(act, w_down[l].astype(BF16), xf, g_ffn_post[l],
                                tm=tm, tk=_pick(d_ff, (2816, 1408, 512, 256, 128)))
    return xf.reshape(batch, seq, d_model)
```

```python
import functools
import math

import numpy as np
import jax
import jax.numpy as jnp
from jax import lax
from jax.experimental import pallas as pl
from jax.experimental.pallas import tpu as pltpu

CHUNK = 64
HEAD_DIM = 128
DIFF_QK_DIM = HEAD_DIM // 2
N_HEADS_MEM = 4
N_BUCKETS = 32
MAX_DISTANCE = 128
EPS = 1e-6
LOG2E = math.log2(math.e)
F32_UNDERFLOW_LOG2 = -160.0

V7X_VMEM_BYTES = 64 * 1024 * 1024
LANES = 128
MXU_DIM = 256
NORM_ROW_CHUNK = 256
HEADS_PER_STEP = 4
SCORES_AHEAD = 4
VMEM_CAP_BYTES = V7X_VMEM_BYTES - 8 * 1024 * 1024

F32 = jnp.float32
BF16 = jnp.bfloat16


def _vmem_limit(estimate_bytes):
    return int(min(VMEM_CAP_BYTES, max(estimate_bytes, 16 * 1024 * 1024)))


def _params(n_axes, vmem_bytes):
    return pltpu.CompilerParams(
        dimension_semantics=("arbitrary",) * n_axes,
        vmem_limit_bytes=_vmem_limit(vmem_bytes))


def _rms_scale(v):
    return lax.rsqrt(jnp.mean(v * v, axis=-1, keepdims=True) + EPS)


def _dot(a, b):
    return jnp.dot(a, b, preferred_element_type=F32)


def _dot_nt(a, b):
    return lax.dot_general(a, b, (((1,), (1,)), ((), ())), preferred_element_type=F32)


def _norm_matmul_kernel(x_ref, g_ref, w_ref, cs_ref, o_ref, h_ref):
    first = pl.program_id(1) == 0

    @pl.when(first)
    def _():
        tm = x_ref.shape[0]
        rc = min(NORM_ROW_CHUNK, tm)
        for r in range(tm // rc):
            rows = slice(r * rc, (r + 1) * rc)
            x = x_ref[rows]
            h = (x * _rms_scale(x) * g_ref[...]).astype(BF16)
            h_ref[rows] = h
            o_ref[rows] = (_dot(h, w_ref[...]) * cs_ref[...]).astype(o_ref.dtype)

    @pl.when(jnp.logical_not(first))
    def _():
        acc = _dot(h_ref[...], w_ref[...])
        o_ref[...] = (acc * cs_ref[...]).astype(o_ref.dtype)


def _norm_matmul(x, gain, w, col_scale, *, tm, tn):
    m, d = x.shape
    n = w.shape[1]
    assert m % tm == 0 and n % tn == 0
    vmem = 2 * tm * d * 4 + tm * d * 2 + 2 * d * tn * 2 + 2 * tm * tn * 2 + 2 * tm * tn * 4
    return pl.pallas_call(
        _norm_matmul_kernel,
        grid=(m // tm, n // tn),
        in_specs=[
            pl.BlockSpec((tm, d), lambda i, j: (i, 0)),
            pl.BlockSpec((1, d), lambda i, j: (0, 0)),
            pl.BlockSpec((d, tn), lambda i, j: (0, j)),
            pl.BlockSpec((1, tn), lambda i, j: (0, j)),
        ],
        out_specs=pl.BlockSpec((tm, tn), lambda i, j: (i, j)),
        out_shape=jax.ShapeDtypeStruct((m, n), BF16),
        scratch_shapes=[pltpu.VMEM((tm, d), BF16)],
        compiler_params=_params(2, vmem),
        name="norm_matmul",
    )(x, gain.reshape(1, d), w, col_scale.reshape(1, n))


def _swiglu(gate, up):
    return gate * (1.0 / (1.0 + jnp.exp(-gate))) * up


def _norm_swiglu_kernel(x_ref, g_ref, wg_ref, wu_ref, o_ref, h_ref):
    first = pl.program_id(1) == 0

    @pl.when(first)
    def _():
        tm = x_ref.shape[0]
        rc = min(NORM_ROW_CHUNK, tm)
        for r in range(tm // rc):
            rows = slice(r * rc, (r + 1) * rc)
            x = x_ref[rows]
            h = (x * _rms_scale(x) * g_ref[...]).astype(BF16)
            h_ref[rows] = h
            o_ref[rows] = _swiglu(_dot(h, wg_ref[...]), _dot(h, wu_ref[...])).astype(o_ref.dtype)

    @pl.when(jnp.logical_not(first))
    def _():
        h = h_ref[...]
        o_ref[...] = _swiglu(_dot(h, wg_ref[...]), _dot(h, wu_ref[...])).astype(o_ref.dtype)


def _norm_swiglu(x, gain, w_gate_up, *, tm, tn):
    m, d = x.shape
    f = w_gate_up.shape[1] // 2
    assert m % tm == 0 and f % tn == 0
    nf = f // tn
    vmem = 2 * tm * d * 4 + tm * d * 2 + 4 * d * tn * 2 + 2 * tm * tn * 2 + 4 * tm * tn * 4
    return pl.pallas_call(
        _norm_swiglu_kernel,
        grid=(m // tm, nf),
        in_specs=[
            pl.BlockSpec((tm, d), lambda i, j: (i, 0)),
            pl.BlockSpec((1, d), lambda i, j: (0, 0)),
            pl.BlockSpec((d, tn), lambda i, j: (0, j)),
            pl.BlockSpec((d, tn), lambda i, j: (0, j + nf)),
        ],
        out_specs=pl.BlockSpec((tm, tn), lambda i, j: (i, j)),
        out_shape=jax.ShapeDtypeStruct((m, f), BF16),
        scratch_shapes=[pltpu.VMEM((tm, d), BF16)],
        compiler_params=_params(2, vmem),
        name="norm_swiglu",
    )(x, gain.reshape(1, d), w_gate_up, w_gate_up)


def _matmul_norm_resid_kernel(l_ref, w_ref, r_ref, g_ref, o_ref, acc_ref):
    k = pl.program_id(1)

    @pl.when(k == 0)
    def _():
        acc_ref[...] = jnp.zeros_like(acc_ref)

    acc_ref[...] += _dot(l_ref[...], w_ref[...])

    @pl.when(k == pl.num_programs(1) - 1)
    def _():
        y = acc_ref[...]
        o_ref[...] = r_ref[...] + y * _rms_scale(y) * g_ref[...]


def _matmul_norm_resid(lhs, w, resid, gain, *, tm, tk):
    m, kdim = lhs.shape
    d = w.shape[1]
    assert m % tm == 0 and kdim % tk == 0
    vmem = 2 * tm * tk * 2 + 2 * tk * d * 2 + 4 * tm * d * 4 + 2 * tm * d * 4
    return pl.pallas_call(
        _matmul_norm_resid_kernel,
        grid=(m // tm, kdim // tk),
        in_specs=[
            pl.BlockSpec((tm, tk), lambda i, k: (i, k)),
            pl.BlockSpec((tk, d), lambda i, k: (k, 0)),
            pl.BlockSpec((tm, d), lambda i, k: (i, 0)),
            pl.BlockSpec((1, d), lambda i, k: (0, 0)),
        ],
        out_specs=pl.BlockSpec((tm, d), lambda i, k: (i, 0)),
        out_shape=jax.ShapeDtypeStruct((m, d), F32),
        scratch_shapes=[pltpu.VMEM((tm, d), F32)],
        compiler_params=_params(2, vmem),
        name="matmul_norm_resid",
    )(lhs, w, resid, gain.reshape(1, d))


def _matmul2_norm_resid_kernel(a_ref, b_ref, w_ref, r_ref, g_ref, o_ref):
    ka = a_ref.shape[1]
    y = _dot(a_ref[...], w_ref[:ka, :]) + _dot(b_ref[...], w_ref[ka:, :])
    o_ref[...] = r_ref[...] + y * _rms_scale(y) * g_ref[...]


def _matmul2_norm_resid(lhs_a, lhs_b, w, resid, gain, *, tm):
    m, ka = lhs_a.shape
    kb = lhs_b.shape[1]
    d = w.shape[1]
    assert m % tm == 0 and w.shape[0] == ka + kb
    vmem = 2 * tm * (ka + kb) * 2 + 2 * (ka + kb) * d * 2 + 4 * tm * d * 4 + 2 * tm * d * 4
    return pl.pallas_call(
        _matmul2_norm_resid_kernel,
        grid=(m // tm,),
        in_specs=[
            pl.BlockSpec((tm, ka), lambda i: (i, 0)),
            pl.BlockSpec((tm, kb), lambda i: (i, 0)),
            pl.BlockSpec((ka + kb, d), lambda i: (0, 0)),
            pl.BlockSpec((tm, d), lambda i: (i, 0)),
            pl.BlockSpec((1, d), lambda i: (0, 0)),
        ],
        out_specs=pl.BlockSpec((tm, d), lambda i: (i, 0)),
        out_shape=jax.ShapeDtypeStruct((m, d), F32),
        compiler_params=_params(1, vmem),
        name="matmul2_norm_resid",
    )(lhs_a, lhs_b, w, resid, gain.reshape(1, d))


def _t5_bucket(rel):
    nb = N_BUCKETS // 2
    ret = jnp.where(rel > 0, nb, 0)
    n = jnp.abs(rel)
    max_exact = nb // 2
    nf = jnp.maximum(n, 1).astype(F32)
    large = max_exact + (jnp.log(nf / max_exact) / math.log(MAX_DISTANCE / max_exact)
                         * (nb - max_exact)).astype(jnp.int32)
    large = jnp.minimum(large, nb - 1)
    return ret + jnp.where(n < max_exact, n, large)


def _far_bias_is_constant(t, seq):
    n = np.arange(t + 1, max(seq, t + 2), dtype=np.float64)
    nb, me = N_BUCKETS // 2, N_BUCKETS // 4
    b = np.minimum(me + (np.log(n / me) / math.log(MAX_DISTANCE / me) * (nb - me)).astype(np.int64), nb - 1)
    return bool(np.all(b == nb - 1)) and (t + 1) >= 2 * MAX_DISTANCE


def _diff_bias_vectors(rel_bias, t):
    j = jnp.arange(2 * t, dtype=jnp.int32)
    rel0 = jnp.where(j < t, -j, 2 * t - j)
    vecs = [rel_bias[_t5_bucket(rel0 - d * t)].astype(F32) for d in range(3)]
    out = jnp.transpose(jnp.stack(vecs, axis=0), (2, 0, 1)) * LOG2E
    return out[:, :, None, :]


def _widen(col, width):
    return jnp.concatenate([col] * (width // LANES), axis=1)


def _diff_attn_kernel(q_ref, k_ref, v_ref, bvec_ref, lamv_ref, gain_ref, o_ref,
                      bias_ref, vt_ref, qq_ref, s_ref, m_ref, l_ref, acc_ref, *, t, n_g, lam_init):
    qi = pl.program_id(2)
    hd = HEAD_DIM
    seq = k_ref.shape[0]

    @pl.when(qi == 0)
    def _():
        key = lax.broadcasted_iota(jnp.int32, (t, t), 0)
        qry = lax.broadcasted_iota(jnp.int32, (t, t), 1)
        shift = CHUNK.bit_length() - 1
        allowed = lax.shift_right_logical(key, shift) <= lax.shift_right_logical(qry, shift)
        for g in range(n_g):
            for d in range(2):
                band = jnp.broadcast_to(bvec_ref[g, d], (t, 2 * t))
                tile_b = pltpu.roll(band, 0, 1, stride=1, stride_axis=0)[:, :t]
                bias_ref[g, d] = jnp.where(allowed, tile_b, -jnp.inf) if d == 0 else tile_b
            for c in range(seq // t):
                blk = v_ref[c * t:(c + 1) * t, g * hd:(g + 1) * hd].astype(F32)
                vt_ref[g, c] = blk.T.astype(BF16)

    lane = lax.broadcasted_iota(jnp.int32, (t, hd), 1)
    first = lane < DIFF_QK_DIM
    for g in range(n_g):
        q = q_ref[:, g * hd:(g + 1) * hd].astype(F32)
        qq_ref[g, :t] = jnp.where(first, q, 0.0).astype(BF16)
        qq_ref[g, t:] = jnp.where(first, 0.0, q).astype(BF16)

    m_ref[...] = jnp.full_like(m_ref, -jnp.inf)
    l_ref[...] = jnp.zeros_like(l_ref)
    acc_ref[...] = jnp.zeros_like(acc_ref)

    def tile(ki, near):
        off = pl.multiple_of(ki * t, t)
        qc = MXU_DIM
        depth = SCORES_AHEAD
        items = [(g, c) for g in range(n_g) for c in range(2 * t // qc)]

        def n_keys(c):
            return (c * qc) % t + qc if near == 0 else t

        def scores(n, g, c):
            s_ref[n % (depth + 1), :n_keys(c)] = _dot_nt(k_ref[pl.ds(off, n_keys(c)), g * hd:(g + 1) * hd],
                                                         qq_ref[g, c * qc:(c + 1) * qc])

        def softmax(n, g, c):
            cols = slice(c * qc, (c + 1) * qc)
            s = s_ref[n % (depth + 1), :n_keys(c)]
            if near is not None:
                bc = (c * qc) % t
                s = bias_ref[g, near, :n_keys(c), bc:bc + qc] + s
            mx = jnp.max(s, axis=0, keepdims=True)
            m_prev = m_ref[g, :, cols]
            if near is None:
                far = bvec_ref[g, 2][:, :qc]
                m_new = jnp.maximum(m_prev, mx + far)
                sub = m_new - far
            else:
                m_new = jnp.maximum(m_prev, mx)
                sub = m_new
            alpha = jnp.exp2(m_prev - m_new)
            p = jnp.exp2(s - sub)
            l_ref[g, :, cols] = alpha * l_ref[g, :, cols] + jnp.sum(p, axis=0, keepdims=True)
            m_ref[g, :, cols] = m_new
            return alpha, p.astype(BF16)

        def weigh(g, c, alpha, p):
            cols = slice(c * qc, (c + 1) * qc)
            acc_ref[g, :, cols] = alpha * acc_ref[g, :, cols] + _dot(vt_ref[g, ki][:, :n_keys(c)], p)

        for n in range(min(depth, len(items))):
            scores(n, *items[n])
        for n, (g, c) in enumerate(items):
            if n + depth < len(items):
                scores(n + depth, *items[n + depth])
            weigh(g, c, *softmax(n, g, c))

    def far_body(ki, carry):
        tile(ki, None)
        return carry

    lax.fori_loop(0, jnp.maximum(qi - 1, 0), far_body, 0)

    @pl.when(qi >= 1)
    def _():
        tile(qi - 1, 1)

    tile(qi, 0)

    lv = lamv_ref[...]
    lam = (jnp.exp(jnp.sum(lv[0:1] * lv[1:2], axis=-1, keepdims=True))
           - jnp.exp(jnp.sum(lv[2:3] * lv[3:4], axis=-1, keepdims=True)) + lam_init)
    for g in range(n_g):
        o = acc_ref[g] / l_ref[g]
        out = o[:, :t] - lam * o[:, t:]
        scale = lax.rsqrt(jnp.mean(out * out, axis=0, keepdims=True) + EPS)
        y = out * scale * gain_ref[...] * (1.0 - lam_init)
        o_ref[:, g * hd:(g + 1) * hd] = y.T.astype(o_ref.dtype)


def _diff_attn(proj, bias_vecs, lam_vecs, sub_gain, *, batch, seq, n_heads, col0, t, n_g, lam_init):
    nq = seq // t
    hd = HEAD_DIM
    assert n_heads % n_g == 0 and col0 % n_g == 0
    ng_blocks = n_heads // n_g
    c0 = col0 // n_g
    w = n_g * hd
    vmem = (4 * t * w * 2 + 4 * seq * w * 2 + n_g * 2 * t * t * 4 + n_g * 2 * t * hd * 2
            + n_g * seq * hd * 2 + n_g * 2 * t * hd * 4 + n_g * 6 * 2 * t * t * 4)
    kern = functools.partial(_diff_attn_kernel, t=t, n_g=n_g, lam_init=lam_init)
    return pl.pallas_call(
        kern,
        grid=(batch, ng_blocks, nq),
        in_specs=[
            pl.BlockSpec((t, w), lambda b, h, i: (b * nq + i, c0 + h)),
            pl.BlockSpec((seq, w), lambda b, h, i: (b, c0 + ng_blocks + h)),
            pl.BlockSpec((seq, w), lambda b, h, i: (b, c0 + 2 * ng_blocks + h)),
            pl.BlockSpec((n_g, 3, 1, 2 * t), lambda b, h, i: (h, 0, 0, 0)),
            pl.BlockSpec((4, DIFF_QK_DIM), lambda b, h, i: (0, 0)),
            pl.BlockSpec((hd, 1), lambda b, h, i: (0, 0)),
        ],
        out_specs=pl.BlockSpec((t, w), lambda b, h, i: (b * nq + i, h)),
        out_shape=jax.ShapeDtypeStruct((batch * seq, n_heads * hd), BF16),
        scratch_shapes=[
            pltpu.VMEM((n_g, 2, t, t), F32),
            pltpu.VMEM((n_g, seq // t, hd, t), BF16),
            pltpu.VMEM((n_g, 2 * t, hd), BF16),
            pltpu.VMEM((SCORES_AHEAD + 1, t, MXU_DIM), F32),
            pltpu.VMEM((n_g, 1, 2 * t), F32),
            pltpu.VMEM((n_g, 1, 2 * t), F32),
            pltpu.VMEM((n_g, hd, 2 * t), F32),
        ],
        compiler_params=_params(3, vmem),
        name="diff_attn",
    )(proj, proj, proj, bias_vecs, lam_vecs, sub_gain.reshape(hd, 1))


def _sb_attn_kernel(q_ref, k_ref, v_ref, tri_ref, gain_ref, o_ref, carry_ref, acc_ref, *, t, cw, n_g):
    qi = pl.program_id(2)
    hd = HEAD_DIM

    def tile(ki, diag):
        off = pl.multiple_of(ki * t, t)
        if diag:
            row = lax.broadcasted_iota(jnp.int32, (t, t), 0)
            col = lax.broadcasted_iota(jnp.int32, (t, t), 1)
            strict = col < row
        head_cols = [slice(g * hd, (g + 1) * hd) for g in range(n_g)]
        tri = tri_ref[...]

        def scores(g):
            return _dot_nt(q_ref[:, head_cols[g]], k_ref[pl.ds(off, t), head_cols[g]])

        def logits(g, z):
            neg_abs = lax.bitcast_convert_type(
                lax.bitcast_convert_type(z, jnp.uint32) | jnp.uint32(0x80000000), F32)
            neg_soft = jnp.log(1.0 + jnp.exp2(neg_abs)) * (-LOG2E)
            log_keep = neg_soft - jnp.maximum(z, 0.0)
            log_beta = log_keep + z
            if diag:
                log_keep = jnp.where(strict, log_keep, 0.0)
            run = None if diag else carry_ref[g]
            after = [None] * (t // cw)
            for c in reversed(range(t // cw)):
                lk = log_keep[:, c * cw:(c + 1) * cw]
                cs = _dot(lk.astype(BF16), tri)
                after[c] = cs if run is None else _widen(run, cw) + cs
                row_sum = jnp.sum(lk, axis=-1, keepdims=True)
                run = jnp.broadcast_to(row_sum, (t, LANES)) if run is None else run + row_sum
            carry_ref[g] = run
            return log_beta, jnp.concatenate(after, axis=1)

        def mix(g, log_beta, after):
            w = jnp.exp2(log_beta + after)
            if diag:
                w = jnp.where(strict, w, 0.0)
            pv = _dot(w.astype(BF16), v_ref[pl.ds(off, t), head_cols[g]])
            if diag:
                acc_ref[g] = pv
            else:
                acc_ref[g] += pv

        z_next = scores(0)
        for g in range(n_g):
            z = z_next
            if g + 1 < n_g:
                z_next = scores(g + 1)
            mix(g, *logits(g, z))

    tile(qi, True)

    def not_done(state):
        j, underflowed = state
        return jnp.logical_and(j < qi, jnp.logical_not(underflowed))

    def body(state):
        j, _ = state
        tile(qi - 1 - j, False)
        largest = jnp.max(carry_ref[0])
        for g in range(1, n_g):
            largest = jnp.maximum(largest, jnp.max(carry_ref[g]))
        return j + 1, largest < F32_UNDERFLOW_LOG2

    lax.while_loop(not_done, body, (jnp.int32(0), jnp.bool_(False)))

    for g in range(n_g):
        out = acc_ref[g]
        o_ref[:, g * hd:(g + 1) * hd] = (out * _rms_scale(out) * gain_ref[...]).astype(o_ref.dtype)


def _sb_attn(proj, out_gain, *, batch, seq, n_heads, col0, t, n_g):
    nq = seq // t
    hd = HEAD_DIM
    assert n_heads % n_g == 0 and col0 % n_g == 0
    ng_blocks = n_heads // n_g
    c0 = col0 // n_g
    wd = n_g * hd
    cw = min(t, MXU_DIM)
    tri = jnp.asarray(np.tril(np.ones((cw, cw), np.float32), -1), BF16)
    vmem = (4 * t * wd * 2 + 4 * seq * wd * 2 + 2 * cw * cw * 2
            + n_g * (t * hd * 4 + t * LANES * 4) + n_g * 8 * t * t * 4)
    kern = functools.partial(_sb_attn_kernel, t=t, cw=cw, n_g=n_g)
    return pl.pallas_call(
        kern,
        grid=(batch, ng_blocks, nq),
        in_specs=[
            pl.BlockSpec((t, wd), lambda b, h, i: (b * nq + i, c0 + h)),
            pl.BlockSpec((seq, wd), lambda b, h, i: (b, c0 + ng_blocks + h)),
            pl.BlockSpec((seq, wd), lambda b, h, i: (b, c0 + 2 * ng_blocks + h)),
            pl.BlockSpec((cw, cw), lambda b, h, i: (0, 0)),
            pl.BlockSpec((1, hd), lambda b, h, i: (0, 0)),
        ],
        out_specs=pl.BlockSpec((t, wd), lambda b, h, i: (b * nq + i, h)),
        out_shape=jax.ShapeDtypeStruct((batch * seq, n_heads * hd), BF16),
        scratch_shapes=[
            pltpu.VMEM((n_g, t, LANES), F32),
            pltpu.VMEM((n_g, t, hd), F32),
        ],
        compiler_params=_params(3, vmem),
        name="sb_attn",
    )(proj, proj, proj, tri, out_gain.reshape(1, hd))


def _mem_attn_kernel(q_ref, k_ref, v_ref, o_ref, *, n_heads, dh):
    scale = dh ** -0.5
    head_cols = [slice(h * dh, (h + 1) * dh) for h in range(n_heads)]

    def scores(h):
        return _dot_nt(q_ref[:, head_cols[h]], k_ref[:, head_cols[h]])

    s_next = scores(0)
    for h in range(n_heads):
        s = s_next * scale
        if h + 1 < n_heads:
            s_next = scores(h + 1)
        p = jnp.exp(s - jnp.max(s, axis=-1, keepdims=True))
        denom = jnp.sum(p, axis=-1, keepdims=True)
        o = _dot(p.astype(BF16), v_ref[:, head_cols[h]]) / denom
        o_ref[:, head_cols[h]] = o.astype(o_ref.dtype)


def _mem_attn(q, kv, *, batch, seq, n_mem, tq):
    d = q.shape[1]
    nq = seq // tq
    vmem = 4 * tq * d * 2 + 4 * n_mem * d * 2 + 8 * tq * n_mem * 4 + 2 * tq * d * 4
    kern = functools.partial(_mem_attn_kernel, n_heads=N_HEADS_MEM, dh=d // N_HEADS_MEM)
    return pl.pallas_call(
        kern,
        grid=(batch, nq),
        in_specs=[
            pl.BlockSpec((tq, d), lambda b, i: (b * nq + i, 0)),
            pl.BlockSpec((n_mem, d), lambda b, i: (b, 0)),
            pl.BlockSpec((n_mem, d), lambda b, i: (b, 1)),
        ],
        out_specs=pl.BlockSpec((tq, d), lambda b, i: (b * nq + i, 0)),
        out_shape=jax.ShapeDtypeStruct((batch * seq, d), BF16),
        compiler_params=_params(2, vmem),
        name="mem_attn",
    )(q, kv, kv)


def _lambda_init(layer_idx):
    return 0.8 - 0.6 * math.exp(-0.3 * layer_idx)


def _pick(n, prefs):
    for p in prefs:
        if n % p == 0:
            return p
    return n


def kernel(x, mem, w_in, w_out, rel_bias, lambda_q1, lambda_k1, lambda_q2, lambda_k2, diff_sub_gain, sb_gain, g_mix_pre, g_mix_post, w_mq, w_mkv, w_mo, g_mem_kv, g_mem_pre, g_mem_post, w_gate_up, w_down, g_ffn_pre, g_ffn_post):
    batch, seq, d_model = x.shape
    n_mem = mem.shape[1]
    depth = w_in.shape[0]
    width = w_out.shape[1] // 2
    n_heads = width // HEAD_DIM
    d_ff = w_down.shape[1]
    t = _pick(seq, (512, 256))
    assert w_in.shape[2] == 6 * width and t % CHUNK == 0
    assert _far_bias_is_constant(t, seq)
    n_g = _pick(n_heads, (HEADS_PER_STEP,))

    rows = batch * seq
    tm = _pick(rows, (512, 256, 128))
    tm_n = _pick(rows, (1024, 512, 256, 128))
    xf = x.reshape(rows, d_model)
    memf = mem.reshape(batch * n_mem, d_model)
    ones = lambda n: jnp.ones((n,), F32)

    col_scale = jnp.concatenate([
        jnp.full((width,), DIFF_QK_DIM ** -0.5 * LOG2E, F32), ones(2 * width),
        jnp.full((width,), HEAD_DIM ** -0.5 * LOG2E, F32), ones(2 * width)])

    for l in range(depth):
        lam_init = _lambda_init(l)
        proj = _norm_matmul(xf, g_mix_pre[l], w_in[l].astype(BF16), col_scale,
                            tm=tm_n, tn=_pick(6 * width, (1536, 1024, 512, 256, 128)))
        lam_vecs = jnp.stack([lambda_q1[l], lambda_k1[l], lambda_q2[l], lambda_k2[l]]).astype(F32)
        out_a = _diff_attn(proj, _diff_bias_vectors(rel_bias, t), lam_vecs, diff_sub_gain[l],
                           batch=batch, seq=seq, n_heads=n_heads, col0=0, t=t, n_g=n_g, lam_init=lam_init)
        out_b = _sb_attn(proj, sb_gain[l], batch=batch, seq=seq, n_heads=n_heads,
                         col0=3 * n_heads, t=t, n_g=n_g)
        xf = _matmul2_norm_resid(out_a, out_b, w_out[l].astype(BF16), xf, g_mix_post[l], tm=tm)
        q_mem = _norm_matmul(xf, g_mem_pre[l], w_mq[l].astype(BF16), ones(d_model),
                             tm=tm_n, tn=_pick(d_model, (2048, 1024, 512, 256, 128)))
        kv = _norm_matmul(memf, g_mem_kv[l], w_mkv[l].astype(BF16), ones(2 * d_model),
                          tm=_pick(batch * n_mem, (512, 256, 128)),
                          tn=_pick(2 * d_model, (2048, 1024, 512, 256, 128)))
        o_mem = _mem_attn(q_mem, kv, batch=batch, seq=seq, n_mem=n_mem,
                          tq=_pick(seq, (512, 256, 128)))
        xf = _matmul_norm_resid(o_mem, w_mo[l].astype(BF16), xf, g_mem_post[l],
                                tm=tm, tk=_pick(d_model, (2048, 1024, 512, 256, 128)))
        act = _norm_swiglu(xf, g_ffn_pre[l], w_gate_up[l].astype(BF16),
                           tm=tm_n, tn=_pick(d_ff, (512, 256, 128)))
        xf = _matmul_norm_resid(act, w_down[l].astype(BF16), xf, g_ffn_post[l],
                                tm=tm, tk=_pick(d_ff, (2816, 1408, 512, 256, 128)))
    return xf.reshape(batch, seq, d_model)
```

```python
import functools
import math

import numpy as np
import jax
import jax.numpy as jnp
from jax import lax
from jax.experimental import pallas as pl
from jax.experimental.pallas import tpu as pltpu

CHUNK = 64
HEAD_DIM = 128
DIFF_QK_DIM = HEAD_DIM // 2
N_HEADS_MEM = 4
N_BUCKETS = 32
MAX_DISTANCE = 128
EPS = 1e-6
LOG2E = math.log2(math.e)
F32_UNDERFLOW_LOG2 = -160.0

V7X_VMEM_BYTES = 64 * 1024 * 1024
LANES = 128
MXU_DIM = 256
NORM_ROW_CHUNK = 256
HEADS_PER_STEP = 4
SCORES_AHEAD = 4
VMEM_CAP_BYTES = V7X_VMEM_BYTES - 8 * 1024 * 1024

F32 = jnp.float32
BF16 = jnp.bfloat16


def _vmem_limit(estimate_bytes):
    return int(min(VMEM_CAP_BYTES, max(estimate_bytes, 16 * 1024 * 1024)))


def _params(n_axes, vmem_bytes):
    return pltpu.CompilerParams(
        dimension_semantics=("arbitrary",) * n_axes,
        vmem_limit_bytes=_vmem_limit(vmem_bytes))


def _rms_scale(v):
    return lax.rsqrt(jnp.mean(v * v, axis=-1, keepdims=True) + EPS)


def _dot(a, b):
    return jnp.dot(a, b, preferred_element_type=F32)


def _dot_nt(a, b):
    return lax.dot_general(a, b, (((1,), (1,)), ((), ())), preferred_element_type=F32)


def _norm_matmul_kernel(x_ref, g_ref, w_ref, cs_ref, o_ref, h_ref):
    first = pl.program_id(1) == 0

    @pl.when(first)
    def _():
        tm = x_ref.shape[0]
        rc = min(NORM_ROW_CHUNK, tm)
        for r in range(tm // rc):
            rows = slice(r * rc, (r + 1) * rc)
            x = x_ref[rows]
            h = (x * _rms_scale(x) * g_ref[...]).astype(BF16)
            h_ref[rows] = h
            o_ref[rows] = (_dot(h, w_ref[...]) * cs_ref[...]).astype(o_ref.dtype)

    @pl.when(jnp.logical_not(first))
    def _():
        acc = _dot(h_ref[...], w_ref[...])
        o_ref[...] = (acc * cs_ref[...]).astype(o_ref.dtype)


def _norm_matmul(x, gain, w, col_scale, *, tm, tn):
    m, d = x.shape
    n = w.shape[1]
    assert m % tm == 0 and n % tn == 0
    vmem = 2 * tm * d * 4 + tm * d * 2 + 2 * d * tn * 2 + 2 * tm * tn * 2 + 2 * tm * tn * 4
    return pl.pallas_call(
        _norm_matmul_kernel,
        grid=(m // tm, n // tn),
        in_specs=[
            pl.BlockSpec((tm, d), lambda i, j: (i, 0)),
            pl.BlockSpec((1, d), lambda i, j: (0, 0)),
            pl.BlockSpec((d, tn), lambda i, j: (0, j)),
            pl.BlockSpec((1, tn), lambda i, j: (0, j)),
        ],
        out_specs=pl.BlockSpec((tm, tn), lambda i, j: (i, j)),
        out_shape=jax.ShapeDtypeStruct((m, n), BF16),
        scratch_shapes=[pltpu.VMEM((tm, d), BF16)],
        compiler_params=_params(2, vmem),
        name="norm_matmul",
    )(x, gain.reshape(1, d), w, col_scale.reshape(1, n))


def _swiglu(gate, up):
    return gate * (1.0 / (1.0 + jnp.exp(-gate))) * up


def _norm_swiglu_kernel(x_ref, g_ref, wg_ref, wu_ref, o_ref, h_ref):
    first = pl.program_id(1) == 0

    @pl.when(first)
    def _():
        tm = x_ref.shape[0]
        rc = min(NORM_ROW_CHUNK, tm)
        for r in range(tm // rc):
            rows = slice(r * rc, (r + 1) * rc)
            x = x_ref[rows]
            h = (x * _rms_scale(x) * g_ref[...]).astype(BF16)
            h_ref[rows] = h
            o_ref[rows] = _swiglu(_dot(h, wg_ref[...]), _dot(h, wu_ref[...])).astype(o_ref.dtype)

    @pl.when(jnp.logical_not(first))
    def _():
        h = h_ref[...]
        o_ref[...] = _swiglu(_dot(h, wg_ref[...]), _dot(h, wu_ref[...])).astype(o_ref.dtype)


def _norm_swiglu(x, gain, w_gate_up, *, tm, tn):
    m, d = x.shape
    f = w_gate_up.shape[1] // 2
    assert m % tm == 0 and f % tn == 0
    nf = f // tn
    vmem = 2 * tm * d * 4 + tm * d * 2 + 4 * d * tn * 2 + 2 * tm * tn * 2 + 4 * tm * tn * 4
    return pl.pallas_call(
        _norm_swiglu_kernel,
        grid=(m // tm, nf),
        in_specs=[
            pl.BlockSpec((tm, d), lambda i, j: (i, 0)),
            pl.BlockSpec((1, d), lambda i, j: (0, 0)),
            pl.BlockSpec((d, tn), lambda i, j: (0, j)),
            pl.BlockSpec((d, tn), lambda i, j: (0, j + nf)),
        ],
        out_specs=pl.BlockSpec((tm, tn), lambda i, j: (i, j)),
        out_shape=jax.ShapeDtypeStruct((m, f), BF16),
        scratch_shapes=[pltpu.VMEM((tm, d), BF16)],
        compiler_params=_params(2, vmem),
        name="norm_swiglu",
    )(x, gain.reshape(1, d), w_gate_up, w_gate_up)


def _matmul_norm_resid_kernel(*refs, n_parts):
    lhs_refs, (w_ref, r_ref, g_ref, o_ref) = refs[:n_parts], refs[n_parts:]
    y, row = None, 0
    for l_ref in lhs_refs:
        part = _dot(l_ref[...], w_ref[row:row + l_ref.shape[1], :])
        y = part if y is None else y + part
        row += l_ref.shape[1]
    o_ref[...] = r_ref[...] + y * _rms_scale(y) * g_ref[...]


def _matmul_norm_resid(lhs_parts, w, resid, gain, *, tm):
    m = lhs_parts[0].shape[0]
    kdim, d = w.shape
    assert m % tm == 0 and sum(p.shape[1] for p in lhs_parts) == kdim
    vmem = 2 * tm * kdim * 2 + kdim * d * 2 + 4 * tm * d * 4 + 2 * tm * d * 4
    return pl.pallas_call(
        functools.partial(_matmul_norm_resid_kernel, n_parts=len(lhs_parts)),
        grid=(m // tm,),
        in_specs=[pl.BlockSpec((tm, p.shape[1]), lambda i: (i, 0)) for p in lhs_parts] + [
            pl.BlockSpec((kdim, d), lambda i: (0, 0), pipeline_mode=pl.Buffered(1)),
            pl.BlockSpec((tm, d), lambda i: (i, 0)),
            pl.BlockSpec((1, d), lambda i: (0, 0)),
        ],
        out_specs=pl.BlockSpec((tm, d), lambda i: (i, 0)),
        out_shape=jax.ShapeDtypeStruct((m, d), F32),
        compiler_params=_params(1, vmem),
        name="matmul_norm_resid",
    )(*lhs_parts, w, resid, gain.reshape(1, d))


def _t5_bucket(rel):
    nb = N_BUCKETS // 2
    ret = jnp.where(rel > 0, nb, 0)
    n = jnp.abs(rel)
    max_exact = nb // 2
    nf = jnp.maximum(n, 1).astype(F32)
    large = max_exact + (jnp.log(nf / max_exact) / math.log(MAX_DISTANCE / max_exact)
                         * (nb - max_exact)).astype(jnp.int32)
    large = jnp.minimum(large, nb - 1)
    return ret + jnp.where(n < max_exact, n, large)


def _far_bias_is_constant(t, seq):
    n = np.arange(t + 1, max(seq, t + 2), dtype=np.float64)
    nb, me = N_BUCKETS // 2, N_BUCKETS // 4
    b = np.minimum(me + (np.log(n / me) / math.log(MAX_DISTANCE / me) * (nb - me)).astype(np.int64), nb - 1)
    return bool(np.all(b == nb - 1)) and (t + 1) >= 2 * MAX_DISTANCE


def _diff_bias_vectors(rel_bias, t):
    j = jnp.arange(2 * t, dtype=jnp.int32)
    rel0 = jnp.where(j < t, -j, 2 * t - j)
    vecs = [rel_bias[_t5_bucket(rel0 - d * t)].astype(F32) for d in range(3)]
    out = jnp.transpose(jnp.stack(vecs, axis=0), (2, 0, 1)) * LOG2E
    return out[:, :, None, :]


def _widen(col, width):
    return jnp.concatenate([col] * (width // LANES), axis=1)


def _diff_attn_kernel(q_ref, k_ref, v_ref, bvec_ref, lamv_ref, gain_ref, o_ref,
                      bias_ref, vt_ref, qq_ref, s_ref, m_ref, l_ref, acc_ref, *, t, n_g, lam_init):
    qi = pl.program_id(2)
    hd = HEAD_DIM
    seq = k_ref.shape[0]

    @pl.when(qi == 0)
    def _():
        key = lax.broadcasted_iota(jnp.int32, (t, t), 0)
        qry = lax.broadcasted_iota(jnp.int32, (t, t), 1)
        shift = CHUNK.bit_length() - 1
        allowed = lax.shift_right_logical(key, shift) <= lax.shift_right_logical(qry, shift)
        for g in range(n_g):
            for d in range(2):
                band = jnp.broadcast_to(bvec_ref[g, d], (t, 2 * t))
                tile_b = pltpu.roll(band, 0, 1, stride=1, stride_axis=0)[:, :t]
                bias_ref[g, d] = jnp.where(allowed, tile_b, -jnp.inf) if d == 0 else tile_b
            for c in range(seq // t):
                blk = v_ref[c * t:(c + 1) * t, g * hd:(g + 1) * hd].astype(F32)
                vt_ref[g, c] = blk.T.astype(BF16)

    lane = lax.broadcasted_iota(jnp.int32, (t, hd), 1)
    first = lane < DIFF_QK_DIM
    for g in range(n_g):
        q = q_ref[:, g * hd:(g + 1) * hd].astype(F32)
        qq_ref[g, :t] = jnp.where(first, q, 0.0).astype(BF16)
        qq_ref[g, t:] = jnp.where(first, 0.0, q).astype(BF16)

    m_ref[...] = jnp.full_like(m_ref, -jnp.inf)
    l_ref[...] = jnp.zeros_like(l_ref)
    acc_ref[...] = jnp.zeros_like(acc_ref)

    def tile(ki, near):
        off = pl.multiple_of(ki * t, t)
        qc = MXU_DIM
        depth = SCORES_AHEAD
        items = [(g, c) for g in range(n_g) for c in range(2 * t // qc)]

        def n_keys(c):
            return (c * qc) % t + qc if near == 0 else t

        def scores(n, g, c):
            s_ref[n % (depth + 1), :n_keys(c)] = _dot_nt(k_ref[pl.ds(off, n_keys(c)), g * hd:(g + 1) * hd],
                                                         qq_ref[g, c * qc:(c + 1) * qc])

        def softmax(n, g, c):
            cols = slice(c * qc, (c + 1) * qc)
            s = s_ref[n % (depth + 1), :n_keys(c)]
            if near is not None:
                bc = (c * qc) % t
                s = bias_ref[g, near, :n_keys(c), bc:bc + qc] + s
            mx = jnp.max(s, axis=0, keepdims=True)
            m_prev = m_ref[g, :, cols]
            if near is None:
                far = bvec_ref[g, 2][:, :qc]
                m_new = jnp.maximum(m_prev, mx + far)
                sub = m_new - far
            else:
                m_new = jnp.maximum(m_prev, mx)
                sub = m_new
            alpha = jnp.exp2(m_prev - m_new)
            p = jnp.exp2(s - sub)
            l_ref[g, :, cols] = alpha * l_ref[g, :, cols] + jnp.sum(p, axis=0, keepdims=True)
            m_ref[g, :, cols] = m_new
            return alpha, p.astype(BF16)

        def weigh(g, c, alpha, p):
            cols = slice(c * qc, (c + 1) * qc)
            acc_ref[g, :, cols] = alpha * acc_ref[g, :, cols] + _dot(vt_ref[g, ki][:, :n_keys(c)], p)

        for n in range(min(depth, len(items))):
            scores(n, *items[n])
        for n, (g, c) in enumerate(items):
            if n + depth < len(items):
                scores(n + depth, *items[n + depth])
            weigh(g, c, *softmax(n, g, c))

    def far_body(ki, carry):
        tile(ki, None)
        return carry

    lax.fori_loop(0, jnp.maximum(qi - 1, 0), far_body, 0)

    @pl.when(qi >= 1)
    def _():
        tile(qi - 1, 1)

    tile(qi, 0)

    lv = lamv_ref[...]
    lam = (jnp.exp(jnp.sum(lv[0:1] * lv[1:2], axis=-1, keepdims=True))
           - jnp.exp(jnp.sum(lv[2:3] * lv[3:4], axis=-1, keepdims=True)) + lam_init)
    for g in range(n_g):
        o = acc_ref[g] / l_ref[g]
        out = o[:, :t] - lam * o[:, t:]
        scale = lax.rsqrt(jnp.mean(out * out, axis=0, keepdims=True) + EPS)
        y = out * scale * gain_ref[...] * (1.0 - lam_init)
        o_ref[:, g * hd:(g + 1) * hd] = y.T.astype(o_ref.dtype)


def _diff_attn(proj, bias_vecs, lam_vecs, sub_gain, *, batch, seq, n_heads, col0, t, n_g, lam_init):
    nq = seq // t
    hd = HEAD_DIM
    assert n_heads % n_g == 0 and col0 % n_g == 0
    ng_blocks = n_heads // n_g
    c0 = col0 // n_g
    w = n_g * hd
    vmem = (4 * t * w * 2 + 4 * seq * w * 2 + n_g * 2 * t * t * 4 + n_g * 2 * t * hd * 2
            + n_g * seq * hd * 2 + n_g * 2 * t * hd * 4 + n_g * 6 * 2 * t * t * 4)
    kern = functools.partial(_diff_attn_kernel, t=t, n_g=n_g, lam_init=lam_init)
    return pl.pallas_call(
        kern,
        grid=(batch, ng_blocks, nq),
        in_specs=[
            pl.BlockSpec((t, w), lambda b, h, i: (b * nq + i, c0 + h)),
            pl.BlockSpec((seq, w), lambda b, h, i: (b, c0 + ng_blocks + h)),
            pl.BlockSpec((seq, w), lambda b, h, i: (b, c0 + 2 * ng_blocks + h)),
            pl.BlockSpec((n_g, 3, 1, 2 * t), lambda b, h, i: (h, 0, 0, 0)),
            pl.BlockSpec((4, DIFF_QK_DIM), lambda b, h, i: (0, 0)),
            pl.BlockSpec((hd, 1), lambda b, h, i: (0, 0)),
        ],
        out_specs=pl.BlockSpec((t, w), lambda b, h, i: (b * nq + i, h)),
        out_shape=jax.ShapeDtypeStruct((batch * seq, n_heads * hd), BF16),
        scratch_shapes=[
            pltpu.VMEM((n_g, 2, t, t), F32),
            pltpu.VMEM((n_g, seq // t, hd, t), BF16),
            pltpu.VMEM((n_g, 2 * t, hd), BF16),
            pltpu.VMEM((SCORES_AHEAD + 1, t, MXU_DIM), F32),
            pltpu.VMEM((n_g, 1, 2 * t), F32),
            pltpu.VMEM((n_g, 1, 2 * t), F32),
            pltpu.VMEM((n_g, hd, 2 * t), F32),
        ],
        compiler_params=_params(3, vmem),
        name="diff_attn",
    )(proj, proj, proj, bias_vecs, lam_vecs, sub_gain.reshape(hd, 1))


def _sb_attn_kernel(q_ref, k_ref, v_ref, tri_ref, gain_ref, o_ref, carry_ref, acc_ref, *, t, cw, n_g):
    qi = pl.program_id(2)
    hd = HEAD_DIM

    def tile(ki, diag):
        off = pl.multiple_of(ki * t, t)
        if diag:
            row = lax.broadcasted_iota(jnp.int32, (t, t), 0)
            col = lax.broadcasted_iota(jnp.int32, (t, t), 1)
            strict = col < row
        head_cols = [slice(g * hd, (g + 1) * hd) for g in range(n_g)]
        tri = tri_ref[...]

        def scores(g):
            return _dot_nt(q_ref[:, head_cols[g]], k_ref[pl.ds(off, t), head_cols[g]])

        def logits(g, z):
            neg_abs = lax.bitcast_convert_type(
                lax.bitcast_convert_type(z, jnp.uint32) | jnp.uint32(0x80000000), F32)
            neg_soft = jnp.log(1.0 + jnp.exp2(neg_abs)) * (-LOG2E)
            log_keep = neg_soft - jnp.maximum(z, 0.0)
            log_beta = log_keep + z
            if diag:
                log_keep = jnp.where(strict, log_keep, 0.0)
            run = None if diag else carry_ref[g]
            after = [None] * (t // cw)
            for c in reversed(range(t // cw)):
                lk = log_keep[:, c * cw:(c + 1) * cw]
                cs = _dot(lk.astype(BF16), tri)
                after[c] = cs if run is None else _widen(run, cw) + cs
                row_sum = jnp.sum(lk, axis=-1, keepdims=True)
                run = jnp.broadcast_to(row_sum, (t, LANES)) if run is None else run + row_sum
            carry_ref[g] = run
            return log_beta, jnp.concatenate(after, axis=1)

        def mix(g, log_beta, after):
            w = jnp.exp2(log_beta + after)
            if diag:
                w = jnp.where(strict, w, 0.0)
            pv = _dot(w.astype(BF16), v_ref[pl.ds(off, t), head_cols[g]])
            if diag:
                acc_ref[g] = pv
            else:
                acc_ref[g] += pv

        z_next = scores(0)
        for g in range(n_g):
            z = z_next
            if g + 1 < n_g:
                z_next = scores(g + 1)
            mix(g, *logits(g, z))

    tile(qi, True)

    def not_done(state):
        j, underflowed = state
        return jnp.logical_and(j < qi, jnp.logical_not(underflowed))

    def body(state):
        j, _ = state
        tile(qi - 1 - j, False)
        largest = jnp.max(carry_ref[0])
        for g in range(1, n_g):
            largest = jnp.maximum(largest, jnp.max(carry_ref[g]))
        return j + 1, largest < F32_UNDERFLOW_LOG2

    lax.while_loop(not_done, body, (jnp.int32(0), jnp.bool_(False)))

    for g in range(n_g):
        out = acc_ref[g]
        o_ref[:, g * hd:(g + 1) * hd] = (out * _rms_scale(out) * gain_ref[...]).astype(o_ref.dtype)


def _sb_attn(proj, out_gain, *, batch, seq, n_heads, col0, t, n_g):
    nq = seq // t
    hd = HEAD_DIM
    assert n_heads % n_g == 0 and col0 % n_g == 0
    ng_blocks = n_heads // n_g
    c0 = col0 // n_g
    wd = n_g * hd
    cw = min(t, MXU_DIM)
    tri = jnp.asarray(np.tril(np.ones((cw, cw), np.float32), -1), BF16)
    vmem = (4 * t * wd * 2 + 4 * seq * wd * 2 + 2 * cw * cw * 2
            + n_g * (t * hd * 4 + t * LANES * 4) + n_g * 8 * t * t * 4)
    kern = functools.partial(_sb_attn_kernel, t=t, cw=cw, n_g=n_g)
    return pl.pallas_call(
        kern,
        grid=(batch, ng_blocks, nq),
        in_specs=[
            pl.BlockSpec((t, wd), lambda b, h, i: (b * nq + i, c0 + h)),
            pl.BlockSpec((seq, wd), lambda b, h, i: (b, c0 + ng_blocks + h)),
            pl.BlockSpec((seq, wd), lambda b, h, i: (b, c0 + 2 * ng_blocks + h)),
            pl.BlockSpec((cw, cw), lambda b, h, i: (0, 0)),
            pl.BlockSpec((1, hd), lambda b, h, i: (0, 0)),
        ],
        out_specs=pl.BlockSpec((t, wd), lambda b, h, i: (b * nq + i, h)),
        out_shape=jax.ShapeDtypeStruct((batch * seq, n_heads * hd), BF16),
        scratch_shapes=[
            pltpu.VMEM((n_g, t, LANES), F32),
            pltpu.VMEM((n_g, t, hd), F32),
        ],
        compiler_params=_params(3, vmem),
        name="sb_attn",
    )(proj, proj, proj, tri, out_gain.reshape(1, hd))


def _mem_attn_kernel(q_ref, k_ref, v_ref, o_ref, *, n_heads, dh):
    scale = dh ** -0.5
    head_cols = [slice(h * dh, (h + 1) * dh) for h in range(n_heads)]

    def scores(h):
        return _dot_nt(q_ref[:, head_cols[h]], k_ref[:, head_cols[h]])

    s_next = scores(0)
    for h in range(n_heads):
        s = s_next * scale
        if h + 1 < n_heads:
            s_next = scores(h + 1)
        p = jnp.exp(s - jnp.max(s, axis=-1, keepdims=True))
        denom = jnp.sum(p, axis=-1, keepdims=True)
        o = _dot(p.astype(BF16), v_ref[:, head_cols[h]]) / denom
        o_ref[:, head_cols[h]] = o.astype(o_ref.dtype)


def _mem_attn(q, kv, *, batch, seq, n_mem, tq):
    d = q.shape[1]
    nq = seq // tq
    vmem = 4 * tq * d * 2 + 4 * n_mem * d * 2 + 8 * tq * n_mem * 4 + 2 * tq * d * 4
    kern = functools.partial(_mem_attn_kernel, n_heads=N_HEADS_MEM, dh=d // N_HEADS_MEM)
    return pl.pallas_call(
        kern,
        grid=(batch, nq),
        in_specs=[
            pl.BlockSpec((tq, d), lambda b, i: (b * nq + i, 0)),
            pl.BlockSpec((n_mem, d), lambda b, i: (b, 0)),
            pl.BlockSpec((n_mem, d), lambda b, i: (b, 1)),
        ],
        out_specs=pl.BlockSpec((tq, d), lambda b, i: (b * nq + i, 0)),
        out_shape=jax.ShapeDtypeStruct((batch * seq, d), BF16),
        compiler_params=_params(2, vmem),
        name="mem_attn",
    )(q, kv, kv)


def _lambda_init(layer_idx):
    return 0.8 - 0.6 * math.exp(-0.3 * layer_idx)


def _pick(n, prefs):
    for p in prefs:
        if n % p == 0:
            return p
    return n


def kernel(x, mem, w_in, w_out, rel_bias, lambda_q1, lambda_k1, lambda_q2, lambda_k2, diff_sub_gain, sb_gain, g_mix_pre, g_mix_post, w_mq, w_mkv, w_mo, g_mem_kv, g_mem_pre, g_mem_post, w_gate_up, w_down, g_ffn_pre, g_ffn_post):
    batch, seq, d_model = x.shape
    n_mem = mem.shape[1]
    depth = w_in.shape[0]
    width = w_out.shape[1] // 2
    n_heads = width // HEAD_DIM
    d_ff = w_down.shape[1]
    t = _pick(seq, (512, 256))
    assert w_in.shape[2] == 6 * width and t % CHUNK == 0
    assert _far_bias_is_constant(t, seq)
    n_g = _pick(n_heads, (HEADS_PER_STEP,))

    rows = batch * seq
    tm = _pick(rows, (512, 256, 128))
    tm_n = _pick(rows, (1024, 512, 256, 128))
    xf = x.reshape(rows, d_model)
    memf = mem.reshape(batch * n_mem, d_model)
    ones = lambda n: jnp.ones((n,), F32)

    col_scale = jnp.concatenate([
        jnp.full((width,), DIFF_QK_DIM ** -0.5 * LOG2E, F32), ones(2 * width),
        jnp.full((width,), HEAD_DIM ** -0.5 * LOG2E, F32), ones(2 * width)])

    for l in range(depth):
        lam_init = _lambda_init(l)
        proj = _norm_matmul(xf, g_mix_pre[l], w_in[l].astype(BF16), col_scale,
                            tm=tm_n, tn=_pick(6 * width, (1536, 1024, 512, 256, 128)))
        lam_vecs = jnp.stack([lambda_q1[l], lambda_k1[l], lambda_q2[l], lambda_k2[l]]).astype(F32)
        out_a = _diff_attn(proj, _diff_bias_vectors(rel_bias, t), lam_vecs, diff_sub_gain[l],
                           batch=batch, seq=seq, n_heads=n_heads, col0=0, t=t, n_g=n_g, lam_init=lam_init)
        out_b = _sb_attn(proj, sb_gain[l], batch=batch, seq=seq, n_heads=n_heads,
                         col0=3 * n_heads, t=t, n_g=n_g)
        xf = _matmul_norm_resid([out_a, out_b], w_out[l].astype(BF16), xf, g_mix_post[l], tm=tm)
        q_mem = _norm_matmul(xf, g_mem_pre[l], w_mq[l].astype(BF16), ones(d_model),
                             tm=tm_n, tn=_pick(d_model, (2048, 1024, 512, 256, 128)))
        kv = _norm_matmul(memf, g_mem_kv[l], w_mkv[l].astype(BF16), ones(2 * d_model),
                          tm=_pick(batch * n_mem, (512, 256, 128)),
                          tn=_pick(2 * d_model, (2048, 1024, 512, 256, 128)))
        o_mem = _mem_attn(q_mem, kv, batch=batch, seq=seq, n_mem=n_mem,
                          tq=_pick(seq, (512, 256, 128)))
        xf = _matmul_norm_resid([o_mem], w_mo[l].astype(BF16), xf, g_mem_post[l], tm=tm)
        act = _norm_swiglu(xf, g_ffn_pre[l], w_gate_up[l].astype(BF16),
                           tm=tm_n, tn=_pick(d_ff, (512, 256, 128)))
        xf = _matmul_norm_resid([act], w_down[l].astype(BF16), xf, g_ffn_post[l], tm=tm)
    return xf.reshape(batch, seq, d_model)
```

```python
import functools
import math

import numpy as np
import jax
import jax.numpy as jnp
from jax import lax
from jax.experimental import pallas as pl
from jax.experimental.pallas import tpu as pltpu

CHUNK = 64
HEAD_DIM = 128
DIFF_QK_DIM = HEAD_DIM // 2
N_HEADS_MEM = 4
N_BUCKETS = 32
MAX_DISTANCE = 128
EPS = 1e-6
LOG2E = math.log2(math.e)
F32_UNDERFLOW_LOG2 = -160.0
F32_SIGN_BIT = 0x80000000

V7X_VMEM_BYTES = 64 * 1024 * 1024
LANES = 128
MXU_DIM = 256
NORM_ROW_CHUNK = 256
HEADS_PER_STEP = 4
SCORES_AHEAD = 4
VMEM_CAP_BYTES = V7X_VMEM_BYTES - 8 * 1024 * 1024

F32 = jnp.float32
BF16 = jnp.bfloat16


def _vmem_limit(estimate_bytes):
    return int(min(VMEM_CAP_BYTES, max(estimate_bytes, 16 * 1024 * 1024)))


def _params(n_axes, vmem_bytes):
    return pltpu.CompilerParams(
        dimension_semantics=("arbitrary",) * n_axes,
        vmem_limit_bytes=_vmem_limit(vmem_bytes))


def _rms_scale(v):
    return lax.rsqrt(jnp.mean(v * v, axis=-1, keepdims=True) + EPS)


def _dot(a, b):
    return jnp.dot(a, b, preferred_element_type=F32)


def _dot_nt(a, b):
    return lax.dot_general(a, b, (((1,), (1,)), ((), ())), preferred_element_type=F32)


def _norm_matmul_kernel(x_ref, g_ref, w_ref, cs_ref, o_ref, h_ref):
    first = pl.program_id(1) == 0

    @pl.when(first)
    def _():
        tm = x_ref.shape[0]
        rc = min(NORM_ROW_CHUNK, tm)
        for r in range(tm // rc):
            rows = slice(r * rc, (r + 1) * rc)
            x = x_ref[rows]
            h = (x * _rms_scale(x) * g_ref[...]).astype(BF16)
            h_ref[rows] = h
            o_ref[rows] = (_dot(h, w_ref[...]) * cs_ref[...]).astype(o_ref.dtype)

    @pl.when(jnp.logical_not(first))
    def _():
        acc = _dot(h_ref[...], w_ref[...])
        o_ref[...] = (acc * cs_ref[...]).astype(o_ref.dtype)


def _norm_matmul(x, gain, w, col_scale, *, tm, tn):
    m, d = x.shape
    n = w.shape[1]
    assert m % tm == 0 and n % tn == 0
    vmem = 2 * tm * d * 4 + tm * d * 2 + 2 * d * tn * 2 + 2 * tm * tn * 2 + 2 * tm * tn * 4
    return pl.pallas_call(
        _norm_matmul_kernel,
        grid=(m // tm, n // tn),
        in_specs=[
            pl.BlockSpec((tm, d), lambda i, j: (i, 0)),
            pl.BlockSpec((1, d), lambda i, j: (0, 0)),
            pl.BlockSpec((d, tn), lambda i, j: (0, j)),
            pl.BlockSpec((1, tn), lambda i, j: (0, j)),
        ],
        out_specs=pl.BlockSpec((tm, tn), lambda i, j: (i, j)),
        out_shape=jax.ShapeDtypeStruct((m, n), BF16),
        scratch_shapes=[pltpu.VMEM((tm, d), BF16)],
        compiler_params=_params(2, vmem),
        name="norm_matmul",
    )(x, gain.reshape(1, d), w, col_scale.reshape(1, n))


def _swiglu(gate, up):
    return gate * (1.0 / (1.0 + jnp.exp(-gate))) * up


def _norm_swiglu_kernel(x_ref, g_ref, wg_ref, wu_ref, o_ref, h_ref):
    first = pl.program_id(1) == 0

    @pl.when(first)
    def _():
        tm = x_ref.shape[0]
        rc = min(NORM_ROW_CHUNK, tm)
        for r in range(tm // rc):
            rows = slice(r * rc, (r + 1) * rc)
            x = x_ref[rows]
            h = (x * _rms_scale(x) * g_ref[...]).astype(BF16)
            h_ref[rows] = h
            o_ref[rows] = _swiglu(_dot(h, wg_ref[...]), _dot(h, wu_ref[...])).astype(o_ref.dtype)

    @pl.when(jnp.logical_not(first))
    def _():
        h = h_ref[...]
        o_ref[...] = _swiglu(_dot(h, wg_ref[...]), _dot(h, wu_ref[...])).astype(o_ref.dtype)


def _norm_swiglu(x, gain, w_gate_up, *, tm, tn):
    m, d = x.shape
    f = w_gate_up.shape[1] // 2
    assert m % tm == 0 and f % tn == 0
    nf = f // tn
    vmem = 2 * tm * d * 4 + tm * d * 2 + 4 * d * tn * 2 + 2 * tm * tn * 2 + 4 * tm * tn * 4
    return pl.pallas_call(
        _norm_swiglu_kernel,
        grid=(m // tm, nf),
        in_specs=[
            pl.BlockSpec((tm, d), lambda i, j: (i, 0)),
            pl.BlockSpec((1, d), lambda i, j: (0, 0)),
            pl.BlockSpec((d, tn), lambda i, j: (0, j)),
            pl.BlockSpec((d, tn), lambda i, j: (0, j + nf)),
        ],
        out_specs=pl.BlockSpec((tm, tn), lambda i, j: (i, j)),
        out_shape=jax.ShapeDtypeStruct((m, f), BF16),
        scratch_shapes=[pltpu.VMEM((tm, d), BF16)],
        compiler_params=_params(2, vmem),
        name="norm_swiglu",
    )(x, gain.reshape(1, d), w_gate_up, w_gate_up)


def _matmul_norm_resid_kernel(*refs, n_parts):
    lhs_refs, (w_ref, r_ref, g_ref, o_ref) = refs[:n_parts], refs[n_parts:]
    y, row = None, 0
    for l_ref in lhs_refs:
        part = _dot(l_ref[...], w_ref[row:row + l_ref.shape[1], :])
        y = part if y is None else y + part
        row += l_ref.shape[1]
    o_ref[...] = r_ref[...] + y * _rms_scale(y) * g_ref[...]


def _matmul_norm_resid(lhs_parts, w, resid, gain, *, tm):
    m = lhs_parts[0].shape[0]
    kdim, d = w.shape
    assert m % tm == 0 and sum(p.shape[1] for p in lhs_parts) == kdim
    vmem = 2 * tm * kdim * 2 + kdim * d * 2 + 4 * tm * d * 4 + 2 * tm * d * 4
    return pl.pallas_call(
        functools.partial(_matmul_norm_resid_kernel, n_parts=len(lhs_parts)),
        grid=(m // tm,),
        in_specs=[pl.BlockSpec((tm, p.shape[1]), lambda i: (i, 0)) for p in lhs_parts] + [
            pl.BlockSpec((kdim, d), lambda i: (0, 0), pipeline_mode=pl.Buffered(1)),
            pl.BlockSpec((tm, d), lambda i: (i, 0)),
            pl.BlockSpec((1, d), lambda i: (0, 0)),
        ],
        out_specs=pl.BlockSpec((tm, d), lambda i: (i, 0)),
        out_shape=jax.ShapeDtypeStruct((m, d), F32),
        compiler_params=_params(1, vmem),
        name="matmul_norm_resid",
    )(*lhs_parts, w, resid, gain.reshape(1, d))


def _t5_bucket(rel):
    nb = N_BUCKETS // 2
    ret = jnp.where(rel > 0, nb, 0)
    n = jnp.abs(rel)
    max_exact = nb // 2
    nf = jnp.maximum(n, 1).astype(F32)
    large = max_exact + (jnp.log(nf / max_exact) / math.log(MAX_DISTANCE / max_exact)
                         * (nb - max_exact)).astype(jnp.int32)
    large = jnp.minimum(large, nb - 1)
    return ret + jnp.where(n < max_exact, n, large)


def _far_bias_is_constant(t, seq):
    n = np.arange(t + 1, max(seq, t + 2), dtype=np.float64)
    nb, me = N_BUCKETS // 2, N_BUCKETS // 4
    b = np.minimum(me + (np.log(n / me) / math.log(MAX_DISTANCE / me) * (nb - me)).astype(np.int64), nb - 1)
    return bool(np.all(b == nb - 1)) and (t + 1) >= 2 * MAX_DISTANCE


def _diff_bias_vectors(rel_bias, t):
    j = jnp.arange(2 * t, dtype=jnp.int32)
    rel0 = jnp.where(j < t, -j, 2 * t - j)
    vecs = [rel_bias[_t5_bucket(rel0 - d * t)].astype(F32) for d in range(3)]
    out = jnp.transpose(jnp.stack(vecs, axis=0), (2, 0, 1)) * LOG2E
    return out[:, :, None, :]


def _widen(col, width):
    return jnp.concatenate([col] * (width // LANES), axis=1)


def _diff_attn_kernel(q_ref, k_ref, v_ref, bvec_ref, lamv_ref, gain_ref, o_ref,
                      bias_ref, vt_ref, qq_ref, s_ref, m_ref, l_ref, acc_ref, *, t, n_g, lam_init):
    qi = pl.program_id(2)
    hd = HEAD_DIM
    seq = k_ref.shape[0]

    @pl.when(qi == 0)
    def _():
        key = lax.broadcasted_iota(jnp.int32, (t, t), 0)
        qry = lax.broadcasted_iota(jnp.int32, (t, t), 1)
        shift = CHUNK.bit_length() - 1
        allowed = lax.shift_right_logical(key, shift) <= lax.shift_right_logical(qry, shift)
        for g in range(n_g):
            for d in range(2):
                band = jnp.broadcast_to(bvec_ref[g, d], (t, 2 * t))
                tile_b = pltpu.roll(band, 0, 1, stride=1, stride_axis=0)[:, :t]
                bias_ref[g, d] = jnp.where(allowed, tile_b, -jnp.inf) if d == 0 else tile_b
            for c in range(seq // t):
                blk = v_ref[c * t:(c + 1) * t, g * hd:(g + 1) * hd].astype(F32)
                vt_ref[g, c] = blk.T.astype(BF16)

    lane = lax.broadcasted_iota(jnp.int32, (t, hd), 1)
    first = lane < DIFF_QK_DIM
    for g in range(n_g):
        q = q_ref[:, g * hd:(g + 1) * hd].astype(F32)
        qq_ref[g, :t] = jnp.where(first, q, 0.0).astype(BF16)
        qq_ref[g, t:] = jnp.where(first, 0.0, q).astype(BF16)

    m_ref[...] = jnp.full_like(m_ref, -jnp.inf)
    l_ref[...] = jnp.zeros_like(l_ref)
    acc_ref[...] = jnp.zeros_like(acc_ref)

    def tile(ki, near):
        off = pl.multiple_of(ki * t, t)
        qc = MXU_DIM
        depth = SCORES_AHEAD
        items = [(g, c) for g in range(n_g) for c in range(2 * t // qc)]

        def n_keys(c):
            return (c * qc) % t + qc if near == 0 else t

        def scores(n, g, c):
            s_ref[n % (depth + 1), :n_keys(c)] = _dot_nt(k_ref[pl.ds(off, n_keys(c)), g * hd:(g + 1) * hd],
                                                         qq_ref[g, c * qc:(c + 1) * qc])

        def softmax(n, g, c):
            cols = slice(c * qc, (c + 1) * qc)
            s = s_ref[n % (depth + 1), :n_keys(c)]
            if near is not None:
                bc = (c * qc) % t
                s = bias_ref[g, near, :n_keys(c), bc:bc + qc] + s
            mx = jnp.max(s, axis=0, keepdims=True)
            m_prev = m_ref[g, :, cols]
            if near is None:
                far = bvec_ref[g, 2][:, :qc]
                m_new = jnp.maximum(m_prev, mx + far)
                sub = m_new - far
            else:
                m_new = jnp.maximum(m_prev, mx)
                sub = m_new
            alpha = jnp.exp2(m_prev - m_new)
            p = jnp.exp2(s - sub)
            l_ref[g, :, cols] = alpha * l_ref[g, :, cols] + jnp.sum(p, axis=0, keepdims=True)
            m_ref[g, :, cols] = m_new
            return alpha, p.astype(BF16)

        def weigh(g, c, alpha, p):
            cols = slice(c * qc, (c + 1) * qc)
            acc_ref[g, :, cols] = alpha * acc_ref[g, :, cols] + _dot(vt_ref[g, ki][:, :n_keys(c)], p)

        for n in range(min(depth, len(items))):
            scores(n, *items[n])
        for n, (g, c) in enumerate(items):
            if n + depth < len(items):
                scores(n + depth, *items[n + depth])
            weigh(g, c, *softmax(n, g, c))

    def far_body(ki, carry):
        tile(ki, None)
        return carry

    lax.fori_loop(0, jnp.maximum(qi - 1, 0), far_body, 0)

    @pl.when(qi >= 1)
    def _():
        tile(qi - 1, 1)

    tile(qi, 0)

    lv = lamv_ref[...]
    lam = (jnp.exp(jnp.sum(lv[0:1] * lv[1:2], axis=-1, keepdims=True))
           - jnp.exp(jnp.sum(lv[2:3] * lv[3:4], axis=-1, keepdims=True)) + lam_init)
    for g in range(n_g):
        o = acc_ref[g] / l_ref[g]
        out = o[:, :t] - lam * o[:, t:]
        scale = lax.rsqrt(jnp.mean(out * out, axis=0, keepdims=True) + EPS)
        y = out * scale * gain_ref[...] * (1.0 - lam_init)
        o_ref[:, g * hd:(g + 1) * hd] = y.T.astype(o_ref.dtype)


def _diff_attn(proj, bias_vecs, lam_vecs, sub_gain, *, batch, seq, n_heads, col0, t, n_g, lam_init):
    nq = seq // t
    hd = HEAD_DIM
    assert n_heads % n_g == 0 and col0 % n_g == 0
    ng_blocks = n_heads // n_g
    c0 = col0 // n_g
    w = n_g * hd
    vmem = (4 * t * w * 2 + 4 * seq * w * 2 + n_g * 2 * t * t * 4 + n_g * 2 * t * hd * 2
            + n_g * seq * hd * 2 + n_g * 2 * t * hd * 4 + n_g * 6 * 2 * t * t * 4)
    kern = functools.partial(_diff_attn_kernel, t=t, n_g=n_g, lam_init=lam_init)
    return pl.pallas_call(
        kern,
        grid=(batch, ng_blocks, nq),
        in_specs=[
            pl.BlockSpec((t, w), lambda b, h, i: (b * nq + i, c0 + h)),
            pl.BlockSpec((seq, w), lambda b, h, i: (b, c0 + ng_blocks + h)),
            pl.BlockSpec((seq, w), lambda b, h, i: (b, c0 + 2 * ng_blocks + h)),
            pl.BlockSpec((n_g, 3, 1, 2 * t), lambda b, h, i: (h, 0, 0, 0)),
            pl.BlockSpec((4, DIFF_QK_DIM), lambda b, h, i: (0, 0)),
            pl.BlockSpec((hd, 1), lambda b, h, i: (0, 0)),
        ],
        out_specs=pl.BlockSpec((t, w), lambda b, h, i: (b * nq + i, h)),
        out_shape=jax.ShapeDtypeStruct((batch * seq, n_heads * hd), BF16),
        scratch_shapes=[
            pltpu.VMEM((n_g, 2, t, t), F32),
            pltpu.VMEM((n_g, seq // t, hd, t), BF16),
            pltpu.VMEM((n_g, 2 * t, hd), BF16),
            pltpu.VMEM((SCORES_AHEAD + 1, t, MXU_DIM), F32),
            pltpu.VMEM((n_g, 1, 2 * t), F32),
            pltpu.VMEM((n_g, 1, 2 * t), F32),
            pltpu.VMEM((n_g, hd, 2 * t), F32),
        ],
        compiler_params=_params(3, vmem),
        name="diff_attn",
    )(proj, proj, proj, bias_vecs, lam_vecs, sub_gain.reshape(hd, 1))


def _sb_attn_kernel(q_ref, k_ref, v_ref, tri_ref, gain_ref, o_ref, carry_ref, acc_ref, *, t, cw, n_g):
    qi = pl.program_id(2)
    hd = HEAD_DIM
    head_cols = [slice(g * hd, (g + 1) * hd) for g in range(n_g)]

    def block(row0, nr, key_off, nk, diag):
        rows = slice(row0, row0 + nr)
        tri = tri_ref[...]
        if diag:
            strict = (lax.broadcasted_iota(jnp.int32, (nr, nk), 1)
                      < lax.broadcasted_iota(jnp.int32, (nr, nk), 0) + row0)

        def scores(g):
            return _dot_nt(q_ref[rows, head_cols[g]], k_ref[pl.ds(key_off, nk), head_cols[g]])

        def logits(g, z):
            neg_abs = lax.bitcast_convert_type(
                lax.bitcast_convert_type(z, jnp.uint32) | jnp.uint32(F32_SIGN_BIT), F32)
            neg_soft = jnp.log(1.0 + jnp.exp2(neg_abs)) * (-LOG2E)
            log_keep = neg_soft - jnp.maximum(z, 0.0)
            log_beta = log_keep + z
            if diag:
                log_keep = jnp.where(strict, log_keep, 0.0)
            run = None if diag else carry_ref[g, rows]
            after = [None] * (nk // cw)
            for c in reversed(range(nk // cw)):
                lk = log_keep[:, c * cw:(c + 1) * cw]
                cs = _dot(lk.astype(BF16), tri)
                after[c] = cs if run is None else _widen(run, cw) + cs
                row_sum = jnp.sum(lk, axis=-1, keepdims=True)
                run = jnp.broadcast_to(row_sum, (nr, LANES)) if run is None else run + row_sum
            carry_ref[g, rows] = run
            return log_beta, jnp.concatenate(after, axis=1)

        def mix(g, log_beta, after):
            w = jnp.exp2(log_beta + after)
            if diag:
                w = jnp.where(strict, w, 0.0)
            pv = _dot(w.astype(BF16), v_ref[pl.ds(key_off, nk), head_cols[g]])
            if diag:
                acc_ref[g, rows] = pv
            else:
                acc_ref[g, rows] += pv

        z_next = scores(0)
        for g in range(n_g):
            z = z_next
            if g + 1 < n_g:
                z_next = scores(g + 1)
            mix(g, *logits(g, z))

    tile_off = pl.multiple_of(qi * t, t)
    block(0, t, tile_off, t, True)

    for row0 in range(0, t, t // 2):
        rows = slice(row0, row0 + t // 2)

        def underflowed():
            largest = jnp.max(carry_ref[0, rows])
            for g in range(1, n_g):
                largest = jnp.maximum(largest, jnp.max(carry_ref[g, rows]))
            return largest < F32_UNDERFLOW_LOG2

        def not_done(state):
            j, done = state
            return jnp.logical_and(j < qi * (t // cw), jnp.logical_not(done))

        def body(state):
            j, _ = state
            block(row0, t // 2, pl.multiple_of(tile_off - (j + 1) * cw, cw), cw, False)
            return j + 1, underflowed()

        lax.while_loop(not_done, body, (jnp.int32(0), underflowed()))

    for g in range(n_g):
        out = acc_ref[g]
        o_ref[:, g * hd:(g + 1) * hd] = (out * _rms_scale(out) * gain_ref[...]).astype(o_ref.dtype)


def _sb_attn(proj, out_gain, *, batch, seq, n_heads, col0, t, n_g):
    nq = seq // t
    hd = HEAD_DIM
    assert n_heads % n_g == 0 and col0 % n_g == 0
    ng_blocks = n_heads // n_g
    c0 = col0 // n_g
    wd = n_g * hd
    cw = min(t, MXU_DIM)
    tri = jnp.asarray(np.tril(np.ones((cw, cw), np.float32), -1), BF16)
    vmem = (4 * t * wd * 2 + 4 * seq * wd * 2 + 2 * cw * cw * 2
            + n_g * (t * hd * 4 + t * LANES * 4) + n_g * 8 * t * t * 4)
    kern = functools.partial(_sb_attn_kernel, t=t, cw=cw, n_g=n_g)
    return pl.pallas_call(
        kern,
        grid=(batch, ng_blocks, nq),
        in_specs=[
            pl.BlockSpec((t, wd), lambda b, h, i: (b * nq + i, c0 + h)),
            pl.BlockSpec((seq, wd), lambda b, h, i: (b, c0 + ng_blocks + h)),
            pl.BlockSpec((seq, wd), lambda b, h, i: (b, c0 + 2 * ng_blocks + h)),
            pl.BlockSpec((cw, cw), lambda b, h, i: (0, 0)),
            pl.BlockSpec((1, hd), lambda b, h, i: (0, 0)),
        ],
        out_specs=pl.BlockSpec((t, wd), lambda b, h, i: (b * nq + i, h)),
        out_shape=jax.ShapeDtypeStruct((batch * seq, n_heads * hd), BF16),
        scratch_shapes=[
            pltpu.VMEM((n_g, t, LANES), F32),
            pltpu.VMEM((n_g, t, hd), F32),
        ],
        compiler_params=_params(3, vmem),
        name="sb_attn",
    )(proj, proj, proj, tri, out_gain.reshape(1, hd))


def _mem_attn_kernel(q_ref, k_ref, v_ref, o_ref, *, n_heads, dh):
    scale = dh ** -0.5
    head_cols = [slice(h * dh, (h + 1) * dh) for h in range(n_heads)]

    def scores(h):
        return _dot_nt(q_ref[:, head_cols[h]], k_ref[:, head_cols[h]])

    s_next = scores(0)
    for h in range(n_heads):
        s = s_next * scale
        if h + 1 < n_heads:
            s_next = scores(h + 1)
        p = jnp.exp(s - jnp.max(s, axis=-1, keepdims=True))
        denom = jnp.sum(p, axis=-1, keepdims=True)
        o = _dot(p.astype(BF16), v_ref[:, head_cols[h]]) / denom
        o_ref[:, head_cols[h]] = o.astype(o_ref.dtype)


def _mem_attn(q, kv, *, batch, seq, n_mem, tq):
    d = q.shape[1]
    nq = seq // tq
    vmem = 4 * tq * d * 2 + 4 * n_mem * d * 2 + 8 * tq * n_mem * 4 + 2 * tq * d * 4
    kern = functools.partial(_mem_attn_kernel, n_heads=N_HEADS_MEM, dh=d // N_HEADS_MEM)
    return pl.pallas_call(
        kern,
        grid=(batch, nq),
        in_specs=[
            pl.BlockSpec((tq, d), lambda b, i: (b * nq + i, 0)),
            pl.BlockSpec((n_mem, d), lambda b, i: (b, 0)),
            pl.BlockSpec((n_mem, d), lambda b, i: (b, 1)),
        ],
        out_specs=pl.BlockSpec((tq, d), lambda b, i: (b * nq + i, 0)),
        out_shape=jax.ShapeDtypeStruct((batch * seq, d), BF16),
        compiler_params=_params(2, vmem),
        name="mem_attn",
    )(q, kv, kv)


def _lambda_init(layer_idx):
    return 0.8 - 0.6 * math.exp(-0.3 * layer_idx)


def _pick(n, prefs):
    for p in prefs:
        if n % p == 0:
            return p
    return n


def kernel(x, mem, w_in, w_out, rel_bias, lambda_q1, lambda_k1, lambda_q2, lambda_k2, diff_sub_gain, sb_gain, g_mix_pre, g_mix_post, w_mq, w_mkv, w_mo, g_mem_kv, g_mem_pre, g_mem_post, w_gate_up, w_down, g_ffn_pre, g_ffn_post):
    batch, seq, d_model = x.shape
    n_mem = mem.shape[1]
    depth = w_in.shape[0]
    width = w_out.shape[1] // 2
    n_heads = width // HEAD_DIM
    d_ff = w_down.shape[1]
    t = _pick(seq, (512, 256))
    assert w_in.shape[2] == 6 * width and t % CHUNK == 0
    assert _far_bias_is_constant(t, seq)
    n_g = _pick(n_heads, (HEADS_PER_STEP,))

    rows = batch * seq
    tm = _pick(rows, (512, 256, 128))
    tm_n = _pick(rows, (1024, 512, 256, 128))
    xf = x.reshape(rows, d_model)
    memf = mem.reshape(batch * n_mem, d_model)
    ones = lambda n: jnp.ones((n,), F32)

    col_scale = jnp.concatenate([
        jnp.full((width,), DIFF_QK_DIM ** -0.5 * LOG2E, F32), ones(2 * width),
        jnp.full((width,), HEAD_DIM ** -0.5 * LOG2E, F32), ones(2 * width)])

    for l in range(depth):
        lam_init = _lambda_init(l)
        proj = _norm_matmul(xf, g_mix_pre[l], w_in[l].astype(BF16), col_scale,
                            tm=tm_n, tn=_pick(6 * width, (1536, 1024, 512, 256, 128)))
        lam_vecs = jnp.stack([lambda_q1[l], lambda_k1[l], lambda_q2[l], lambda_k2[l]]).astype(F32)
        out_a = _diff_attn(proj, _diff_bias_vectors(rel_bias, t), lam_vecs, diff_sub_gain[l],
                           batch=batch, seq=seq, n_heads=n_heads, col0=0, t=t, n_g=n_g, lam_init=lam_init)
        out_b = _sb_attn(proj, sb_gain[l], batch=batch, seq=seq, n_heads=n_heads,
                         col0=3 * n_heads, t=t, n_g=n_g)
        xf = _matmul_norm_resid([out_a, out_b], w_out[l].astype(BF16), xf, g_mix_post[l], tm=tm)
        q_mem = _norm_matmul(xf, g_mem_pre[l], w_mq[l].astype(BF16), ones(d_model),
                             tm=tm_n, tn=_pick(d_model, (2048, 1024, 512, 256, 128)))
        kv = _norm_matmul(memf, g_mem_kv[l], w_mkv[l].astype(BF16), ones(2 * d_model),
                          tm=_pick(batch * n_mem, (512, 256, 128)),
                          tn=_pick(2 * d_model, (2048, 1024, 512, 256, 128)))
        o_mem = _mem_attn(q_mem, kv, batch=batch, seq=seq, n_mem=n_mem,
                          tq=_pick(seq, (512, 256, 128)))
        xf = _matmul_norm_resid([o_mem], w_mo[l].astype(BF16), xf, g_mem_post[l], tm=tm)
        act = _norm_swiglu(xf, g_ffn_pre[l], w_gate_up[l].astype(BF16),
                           tm=tm_n, tn=_pick(d_ff, (512, 256, 128)))
        xf = _matmul_norm_resid([act], w_down[l].astype(BF16), xf, g_ffn_post[l], tm=tm)
    return xf.reshape(batch, seq, d_model)
```

```python
import functools
import math

import numpy as np
import jax
import jax.numpy as jnp
from jax import lax
from jax.experimental import pallas as pl
from jax.experimental.pallas import tpu as pltpu

CHUNK = 64
HEAD_DIM = 128
DIFF_QK_DIM = HEAD_DIM // 2
N_HEADS_MEM = 4
N_BUCKETS = 32
MAX_DISTANCE = 128
EPS = 1e-6
LOG2E = math.log2(math.e)
F32_UNDERFLOW_LOG2 = -160.0
F32_SIGN_BIT = 0x80000000

V7X_VMEM_BYTES = 64 * 1024 * 1024
LANES = 128
MXU_DIM = 256
NORM_ROW_CHUNK = 256
HEADS_PER_STEP = 4
SCORES_AHEAD = 4
VMEM_CAP_BYTES = V7X_VMEM_BYTES - 8 * 1024 * 1024

F32 = jnp.float32
BF16 = jnp.bfloat16


def _vmem_limit(estimate_bytes):
    return int(min(VMEM_CAP_BYTES, max(estimate_bytes, 16 * 1024 * 1024)))


def _params(n_axes, vmem_bytes):
    return pltpu.CompilerParams(
        dimension_semantics=("arbitrary",) * n_axes,
        vmem_limit_bytes=_vmem_limit(vmem_bytes))


def _rms_scale(v):
    return lax.rsqrt(jnp.mean(v * v, axis=-1, keepdims=True) + EPS)


def _dot(a, b):
    return jnp.dot(a, b, preferred_element_type=F32)


def _dot_nt(a, b):
    return lax.dot_general(a, b, (((1,), (1,)), ((), ())), preferred_element_type=F32)


def _norm_matmul_kernel(x_ref, g_ref, w_ref, cs_ref, o_ref, h_ref):
    first = pl.program_id(1) == 0

    @pl.when(first)
    def _():
        tm = x_ref.shape[0]
        rc = min(NORM_ROW_CHUNK, tm)
        for r in range(tm // rc):
            rows = slice(r * rc, (r + 1) * rc)
            x = x_ref[rows]
            h = (x * _rms_scale(x) * g_ref[...]).astype(BF16)
            h_ref[rows] = h
            o_ref[rows] = (_dot(h, w_ref[...]) * cs_ref[...]).astype(o_ref.dtype)

    @pl.when(jnp.logical_not(first))
    def _():
        acc = _dot(h_ref[...], w_ref[...])
        o_ref[...] = (acc * cs_ref[...]).astype(o_ref.dtype)


def _norm_matmul(x, gain, w, col_scale, *, tm, tn):
    m, d = x.shape
    n = w.shape[1]
    assert m % tm == 0 and n % tn == 0
    vmem = 2 * tm * d * 4 + tm * d * 2 + 2 * d * tn * 2 + 2 * tm * tn * 2 + 2 * tm * tn * 4
    return pl.pallas_call(
        _norm_matmul_kernel,
        grid=(m // tm, n // tn),
        in_specs=[
            pl.BlockSpec((tm, d), lambda i, j: (i, 0)),
            pl.BlockSpec((1, d), lambda i, j: (0, 0)),
            pl.BlockSpec((d, tn), lambda i, j: (0, j)),
            pl.BlockSpec((1, tn), lambda i, j: (0, j)),
        ],
        out_specs=pl.BlockSpec((tm, tn), lambda i, j: (i, j)),
        out_shape=jax.ShapeDtypeStruct((m, n), BF16),
        scratch_shapes=[pltpu.VMEM((tm, d), BF16)],
        compiler_params=_params(2, vmem),
        name="norm_matmul",
    )(x, gain.reshape(1, d), w, col_scale.reshape(1, n))


def _swiglu(gate, up):
    return gate * (1.0 / (1.0 + jnp.exp(-gate))) * up


def _norm_swiglu_kernel(x_ref, g_ref, wg_ref, wu_ref, o_ref, h_ref):
    first = pl.program_id(1) == 0

    @pl.when(first)
    def _():
        tm = x_ref.shape[0]
        rc = min(NORM_ROW_CHUNK, tm)
        for r in range(tm // rc):
            rows = slice(r * rc, (r + 1) * rc)
            x = x_ref[rows]
            h = (x * _rms_scale(x) * g_ref[...]).astype(BF16)
            h_ref[rows] = h
            o_ref[rows] = _swiglu(_dot(h, wg_ref[...]), _dot(h, wu_ref[...])).astype(o_ref.dtype)

    @pl.when(jnp.logical_not(first))
    def _():
        h = h_ref[...]
        o_ref[...] = _swiglu(_dot(h, wg_ref[...]), _dot(h, wu_ref[...])).astype(o_ref.dtype)


def _norm_swiglu(x, gain, w_gate_up, *, tm, tn):
    m, d = x.shape
    f = w_gate_up.shape[1] // 2
    assert m % tm == 0 and f % tn == 0
    nf = f // tn
    vmem = 2 * tm * d * 4 + tm * d * 2 + 4 * d * tn * 2 + 2 * tm * tn * 2 + 4 * tm * tn * 4
    return pl.pallas_call(
        _norm_swiglu_kernel,
        grid=(m // tm, nf),
        in_specs=[
            pl.BlockSpec((tm, d), lambda i, j: (i, 0)),
            pl.BlockSpec((1, d), lambda i, j: (0, 0)),
            pl.BlockSpec((d, tn), lambda i, j: (0, j)),
            pl.BlockSpec((d, tn), lambda i, j: (0, j + nf)),
        ],
        out_specs=pl.BlockSpec((tm, tn), lambda i, j: (i, j)),
        out_shape=jax.ShapeDtypeStruct((m, f), BF16),
        scratch_shapes=[pltpu.VMEM((tm, d), BF16)],
        compiler_params=_params(2, vmem),
        name="norm_swiglu",
    )(x, gain.reshape(1, d), w_gate_up, w_gate_up)


def _matmul_norm_resid_kernel(*refs, n_parts):
    lhs_refs, (w_ref, r_ref, g_ref, o_ref) = refs[:n_parts], refs[n_parts:]
    y, row = None, 0
    for l_ref in lhs_refs:
        part = _dot(l_ref[...], w_ref[row:row + l_ref.shape[1], :])
        y = part if y is None else y + part
        row += l_ref.shape[1]
    o_ref[...] = r_ref[...] + y * _rms_scale(y) * g_ref[...]


def _matmul_norm_resid(lhs_parts, w, resid, gain, *, tm):
    m = lhs_parts[0].shape[0]
    kdim, d = w.shape
    assert m % tm == 0 and sum(p.shape[1] for p in lhs_parts) == kdim
    vmem = 2 * tm * kdim * 2 + kdim * d * 2 + 4 * tm * d * 4 + 2 * tm * d * 4
    return pl.pallas_call(
        functools.partial(_matmul_norm_resid_kernel, n_parts=len(lhs_parts)),
        grid=(m // tm,),
        in_specs=[pl.BlockSpec((tm, p.shape[1]), lambda i: (i, 0)) for p in lhs_parts] + [
            pl.BlockSpec((kdim, d), lambda i: (0, 0), pipeline_mode=pl.Buffered(1)),
            pl.BlockSpec((tm, d), lambda i: (i, 0)),
            pl.BlockSpec((1, d), lambda i: (0, 0)),
        ],
        out_specs=pl.BlockSpec((tm, d), lambda i: (i, 0)),
        out_shape=jax.ShapeDtypeStruct((m, d), F32),
        compiler_params=_params(1, vmem),
        name="matmul_norm_resid",
    )(*lhs_parts, w, resid, gain.reshape(1, d))


def _t5_bucket(rel):
    nb = N_BUCKETS // 2
    ret = jnp.where(rel > 0, nb, 0)
    n = jnp.abs(rel)
    max_exact = nb // 2
    nf = jnp.maximum(n, 1).astype(F32)
    large = max_exact + (jnp.log(nf / max_exact) / math.log(MAX_DISTANCE / max_exact)
                         * (nb - max_exact)).astype(jnp.int32)
    large = jnp.minimum(large, nb - 1)
    return ret + jnp.where(n < max_exact, n, large)


def _far_bias_is_constant(t, seq):
    n = np.arange(t + 1, max(seq, t + 2), dtype=np.float64)
    nb, me = N_BUCKETS // 2, N_BUCKETS // 4
    b = np.minimum(me + (np.log(n / me) / math.log(MAX_DISTANCE / me) * (nb - me)).astype(np.int64), nb - 1)
    return bool(np.all(b == nb - 1)) and (t + 1) >= 2 * MAX_DISTANCE


def _diff_bias_vectors(rel_bias, t):
    j = jnp.arange(2 * t, dtype=jnp.int32)
    rel0 = jnp.where(j < t, -j, 2 * t - j)
    vecs = [rel_bias[_t5_bucket(rel0 - d * t)].astype(F32) for d in range(3)]
    out = jnp.transpose(jnp.stack(vecs, axis=0), (2, 0, 1)) * LOG2E
    return out[:, :, None, :]


def _widen(col, width):
    return jnp.concatenate([col] * (width // LANES), axis=1)


def _diff_attn_kernel(q_ref, k_ref, v_ref, bvec_ref, lamv_ref, gain_ref, o_ref,
                      bias_ref, vt_ref, qq_ref, s_ref, m_ref, l_ref, acc_ref, *, t, n_g, lam_init):
    qi = pl.program_id(2)
    hd = HEAD_DIM
    seq = k_ref.shape[0]

    @pl.when(qi == 0)
    def _():
        key = lax.broadcasted_iota(jnp.int32, (t, t), 0)
        qry = lax.broadcasted_iota(jnp.int32, (t, t), 1)
        shift = CHUNK.bit_length() - 1
        allowed = lax.shift_right_logical(key, shift) <= lax.shift_right_logical(qry, shift)
        for g in range(n_g):
            for d in range(2):
                band = jnp.broadcast_to(bvec_ref[g, d], (t, 2 * t))
                tile_b = pltpu.roll(band, 0, 1, stride=1, stride_axis=0)[:, :t]
                bias_ref[g, d] = jnp.where(allowed, tile_b, -jnp.inf) if d == 0 else tile_b
            for c in range(seq // t):
                blk = v_ref[c * t:(c + 1) * t, g * hd:(g + 1) * hd].astype(F32)
                vt_ref[g, c] = blk.T.astype(BF16)

    lane = lax.broadcasted_iota(jnp.int32, (t, hd), 1)
    first = lane < DIFF_QK_DIM
    for g in range(n_g):
        q = q_ref[:, g * hd:(g + 1) * hd].astype(F32)
        qq_ref[g, :t] = jnp.where(first, q, 0.0).astype(BF16)
        qq_ref[g, t:] = jnp.where(first, 0.0, q).astype(BF16)

    m_ref[...] = jnp.full_like(m_ref, -jnp.inf)
    l_ref[...] = jnp.zeros_like(l_ref)
    acc_ref[...] = jnp.zeros_like(acc_ref)

    def tiles(kis, nears):
        offs = [pl.multiple_of(ki * t, t) for ki in kis]
        qc = MXU_DIM
        depth = SCORES_AHEAD
        items = [(i, g, c) for i in range(len(kis)) for g in range(n_g) for c in range(2 * t // qc)]

        def n_keys(i, c):
            return (c * qc) % t + qc if nears[i] == 0 else t

        def scores(n, i, g, c):
            nk = n_keys(i, c)
            s_ref[n % (depth + 1), :nk] = _dot_nt(k_ref[pl.ds(offs[i], nk), g * hd:(g + 1) * hd],
                                                  qq_ref[g, c * qc:(c + 1) * qc])

        def softmax(n, i, g, c):
            cols = slice(c * qc, (c + 1) * qc)
            nk = n_keys(i, c)
            s = s_ref[n % (depth + 1), :nk]
            if nears[i] is not None:
                bc = (c * qc) % t
                s = bias_ref[g, nears[i], :nk, bc:bc + qc] + s
            mx = jnp.max(s, axis=0, keepdims=True)
            m_prev = m_ref[g, :, cols]
            if nears[i] is None:
                far = bvec_ref[g, 2][:, :qc]
                m_new = jnp.maximum(m_prev, mx + far)
                sub = m_new - far
            else:
                m_new = jnp.maximum(m_prev, mx)
                sub = m_new
            alpha = jnp.exp2(m_prev - m_new)
            p = jnp.exp2(s - sub)
            l_ref[g, :, cols] = alpha * l_ref[g, :, cols] + jnp.sum(p, axis=0, keepdims=True)
            m_ref[g, :, cols] = m_new
            return alpha, p.astype(BF16)

        def weigh(i, g, c, alpha, p):
            cols = slice(c * qc, (c + 1) * qc)
            acc_ref[g, :, cols] = alpha * acc_ref[g, :, cols] + _dot(vt_ref[g, kis[i]][:, :n_keys(i, c)], p)

        for n in range(min(depth, len(items))):
            scores(n, *items[n])
        for n, (i, g, c) in enumerate(items):
            if n + depth < len(items):
                scores(n + depth, *items[n + depth])
            weigh(i, g, c, *softmax(n, i, g, c))

    n_far = jnp.maximum(qi - 1, 0)

    def far_pair(j, carry):
        tiles([2 * j, 2 * j + 1], [None, None])
        return carry

    lax.fori_loop(0, n_far // 2, far_pair, 0)

    @pl.when(n_far % 2 == 1)
    def _():
        tiles([n_far - 1], [None])

    @pl.when(qi >= 1)
    def _():
        tiles([qi - 1, qi], [1, 0])

    @pl.when(qi == 0)
    def _():
        tiles([qi], [0])

    lv = lamv_ref[...]
    lam = (jnp.exp(jnp.sum(lv[0:1] * lv[1:2], axis=-1, keepdims=True))
           - jnp.exp(jnp.sum(lv[2:3] * lv[3:4], axis=-1, keepdims=True)) + lam_init)
    for g in range(n_g):
        o = acc_ref[g] / l_ref[g]
        out = o[:, :t] - lam * o[:, t:]
        scale = lax.rsqrt(jnp.mean(out * out, axis=0, keepdims=True) + EPS)
        y = out * scale * gain_ref[...] * (1.0 - lam_init)
        o_ref[:, g * hd:(g + 1) * hd] = y.T.astype(o_ref.dtype)


def _diff_attn(proj, bias_vecs, lam_vecs, sub_gain, *, batch, seq, n_heads, col0, t, n_g, lam_init):
    nq = seq // t
    hd = HEAD_DIM
    assert n_heads % n_g == 0 and col0 % n_g == 0
    ng_blocks = n_heads // n_g
    c0 = col0 // n_g
    w = n_g * hd
    vmem = (4 * t * w * 2 + 4 * seq * w * 2 + n_g * 2 * t * t * 4 + n_g * 2 * t * hd * 2
            + n_g * seq * hd * 2 + n_g * 2 * t * hd * 4 + n_g * 6 * 2 * t * t * 4)
    kern = functools.partial(_diff_attn_kernel, t=t, n_g=n_g, lam_init=lam_init)
    return pl.pallas_call(
        kern,
        grid=(batch, ng_blocks, nq),
        in_specs=[
            pl.BlockSpec((t, w), lambda b, h, i: (b * nq + i, c0 + h)),
            pl.BlockSpec((seq, w), lambda b, h, i: (b, c0 + ng_blocks + h)),
            pl.BlockSpec((seq, w), lambda b, h, i: (b, c0 + 2 * ng_blocks + h)),
            pl.BlockSpec((n_g, 3, 1, 2 * t), lambda b, h, i: (h, 0, 0, 0)),
            pl.BlockSpec((4, DIFF_QK_DIM), lambda b, h, i: (0, 0)),
            pl.BlockSpec((hd, 1), lambda b, h, i: (0, 0)),
        ],
        out_specs=pl.BlockSpec((t, w), lambda b, h, i: (b * nq + i, h)),
        out_shape=jax.ShapeDtypeStruct((batch * seq, n_heads * hd), BF16),
        scratch_shapes=[
            pltpu.VMEM((n_g, 2, t, t), F32),
            pltpu.VMEM((n_g, seq // t, hd, t), BF16),
            pltpu.VMEM((n_g, 2 * t, hd), BF16),
            pltpu.VMEM((SCORES_AHEAD + 1, t, MXU_DIM), F32),
            pltpu.VMEM((n_g, 1, 2 * t), F32),
            pltpu.VMEM((n_g, 1, 2 * t), F32),
            pltpu.VMEM((n_g, hd, 2 * t), F32),
        ],
        compiler_params=_params(3, vmem),
        name="diff_attn",
    )(proj, proj, proj, bias_vecs, lam_vecs, sub_gain.reshape(hd, 1))


def _sb_attn_kernel(q_ref, k_ref, v_ref, tri_ref, gain_ref, o_ref, carry_ref, acc_ref, *, t, cw, n_g):
    qi = pl.program_id(2)
    hd = HEAD_DIM
    head_cols = [slice(g * hd, (g + 1) * hd) for g in range(n_g)]

    def block(row0, nr, key_off, nk, diag):
        rows = slice(row0, row0 + nr)
        tri = tri_ref[...]
        if diag:
            strict = (lax.broadcasted_iota(jnp.int32, (nr, nk), 1)
                      < lax.broadcasted_iota(jnp.int32, (nr, nk), 0) + row0)

        def scores(g):
            return _dot_nt(q_ref[rows, head_cols[g]], k_ref[pl.ds(key_off, nk), head_cols[g]])

        def logits(g, z):
            neg_abs = lax.bitcast_convert_type(
                lax.bitcast_convert_type(z, jnp.uint32) | jnp.uint32(F32_SIGN_BIT), F32)
            neg_soft = jnp.log(1.0 + jnp.exp2(neg_abs)) * (-LOG2E)
            log_keep = neg_soft - jnp.maximum(z, 0.0)
            log_beta = log_keep + z
            if diag:
                log_keep = jnp.where(strict, log_keep, 0.0)
            run = None if diag else carry_ref[g, rows]
            after = [None] * (nk // cw)
            for c in reversed(range(nk // cw)):
                lk = log_keep[:, c * cw:(c + 1) * cw]
                cs = _dot(lk.astype(BF16), tri)
                after[c] = cs if run is None else _widen(run, cw) + cs
                row_sum = jnp.sum(lk, axis=-1, keepdims=True)
                run = jnp.broadcast_to(row_sum, (nr, LANES)) if run is None else run + row_sum
            carry_ref[g, rows] = run
            return log_beta, jnp.concatenate(after, axis=1)

        def mix(g, log_beta, after):
            w = jnp.exp2(log_beta + after)
            if diag:
                w = jnp.where(strict, w, 0.0)
            pv = _dot(w.astype(BF16), v_ref[pl.ds(key_off, nk), head_cols[g]])
            if diag:
                acc_ref[g, rows] = pv
            else:
                acc_ref[g, rows] += pv

        z_next = scores(0)
        for g in range(n_g):
            z = z_next
            if g + 1 < n_g:
                z_next = scores(g + 1)
            mix(g, *logits(g, z))

    tile_off = pl.multiple_of(qi * t, t)
    block(0, t, tile_off, t, True)

    for row0 in range(0, t, t // 2):
        rows = slice(row0, row0 + t // 2)

        def underflowed():
            largest = jnp.max(carry_ref[0, rows])
            for g in range(1, n_g):
                largest = jnp.maximum(largest, jnp.max(carry_ref[g, rows]))
            return largest < F32_UNDERFLOW_LOG2

        def not_done(state):
            j, done = state
            return jnp.logical_and(j < qi * (t // cw), jnp.logical_not(done))

        def body(state):
            j, _ = state
            block(row0, t // 2, pl.multiple_of(tile_off - (j + 1) * cw, cw), cw, False)
            return j + 1, underflowed()

        lax.while_loop(not_done, body, (jnp.int32(0), underflowed()))

    for g in range(n_g):
        out = acc_ref[g]
        o_ref[:, g * hd:(g + 1) * hd] = (out * _rms_scale(out) * gain_ref[...]).astype(o_ref.dtype)


def _sb_attn(proj, out_gain, *, batch, seq, n_heads, col0, t, n_g):
    nq = seq // t
    hd = HEAD_DIM
    assert n_heads % n_g == 0 and col0 % n_g == 0
    ng_blocks = n_heads // n_g
    c0 = col0 // n_g
    wd = n_g * hd
    cw = min(t, MXU_DIM)
    tri = jnp.asarray(np.tril(np.ones((cw, cw), np.float32), -1), BF16)
    vmem = (4 * t * wd * 2 + 4 * seq * wd * 2 + 2 * cw * cw * 2
            + n_g * (t * hd * 4 + t * LANES * 4) + n_g * 8 * t * t * 4)
    kern = functools.partial(_sb_attn_kernel, t=t, cw=cw, n_g=n_g)
    return pl.pallas_call(
        kern,
        grid=(batch, ng_blocks, nq),
        in_specs=[
            pl.BlockSpec((t, wd), lambda b, h, i: (b * nq + i, c0 + h)),
            pl.BlockSpec((seq, wd), lambda b, h, i: (b, c0 + ng_blocks + h)),
            pl.BlockSpec((seq, wd), lambda b, h, i: (b, c0 + 2 * ng_blocks + h)),
            pl.BlockSpec((cw, cw), lambda b, h, i: (0, 0)),
            pl.BlockSpec((1, hd), lambda b, h, i: (0, 0)),
        ],
        out_specs=pl.BlockSpec((t, wd), lambda b, h, i: (b * nq + i, h)),
        out_shape=jax.ShapeDtypeStruct((batch * seq, n_heads * hd), BF16),
        scratch_shapes=[
            pltpu.VMEM((n_g, t, LANES), F32),
            pltpu.VMEM((n_g, t, hd), F32),
        ],
        compiler_params=_params(3, vmem),
        name="sb_attn",
    )(proj, proj, proj, tri, out_gain.reshape(1, hd))


def _mem_attn_kernel(q_ref, k_ref, v_ref, o_ref, *, n_heads, dh):
    scale = dh ** -0.5
    head_cols = [slice(h * dh, (h + 1) * dh) for h in range(n_heads)]

    def scores(h):
        return _dot_nt(q_ref[:, head_cols[h]], k_ref[:, head_cols[h]])

    s_next = scores(0)
    for h in range(n_heads):
        s = s_next * scale
        if h + 1 < n_heads:
            s_next = scores(h + 1)
        p = jnp.exp(s - jnp.max(s, axis=-1, keepdims=True))
        denom = jnp.sum(p, axis=-1, keepdims=True)
        o = _dot(p.astype(BF16), v_ref[:, head_cols[h]]) / denom
        o_ref[:, head_cols[h]] = o.astype(o_ref.dtype)


def _mem_attn(q, kv, *, batch, seq, n_mem, tq):
    d = q.shape[1]
    nq = seq // tq
    vmem = 4 * tq * d * 2 + 4 * n_mem * d * 2 + 8 * tq * n_mem * 4 + 2 * tq * d * 4
    kern = functools.partial(_mem_attn_kernel, n_heads=N_HEADS_MEM, dh=d // N_HEADS_MEM)
    return pl.pallas_call(
        kern,
        grid=(batch, nq),
        in_specs=[
            pl.BlockSpec((tq, d), lambda b, i: (b * nq + i, 0)),
            pl.BlockSpec((n_mem, d), lambda b, i: (b, 0)),
            pl.BlockSpec((n_mem, d), lambda b, i: (b, 1)),
        ],
        out_specs=pl.BlockSpec((tq, d), lambda b, i: (b * nq + i, 0)),
        out_shape=jax.ShapeDtypeStruct((batch * seq, d), BF16),
        compiler_params=_params(2, vmem),
        name="mem_attn",
    )(q, kv, kv)


def _lambda_init(layer_idx):
    return 0.8 - 0.6 * math.exp(-0.3 * layer_idx)


def _pick(n, prefs):
    for p in prefs:
        if n % p == 0:
            return p
    return n


def kernel(x, mem, w_in, w_out, rel_bias, lambda_q1, lambda_k1, lambda_q2, lambda_k2, diff_sub_gain, sb_gain, g_mix_pre, g_mix_post, w_mq, w_mkv, w_mo, g_mem_kv, g_mem_pre, g_mem_post, w_gate_up, w_down, g_ffn_pre, g_ffn_post):
    batch, seq, d_model = x.shape
    n_mem = mem.shape[1]
    depth = w_in.shape[0]
    width = w_out.shape[1] // 2
    n_heads = width // HEAD_DIM
    d_ff = w_down.shape[1]
    t = _pick(seq, (512, 256))
    assert w_in.shape[2] == 6 * width and t % CHUNK == 0
    assert _far_bias_is_constant(t, seq)
    n_g = _pick(n_heads, (HEADS_PER_STEP,))

    rows = batch * seq
    tm = _pick(rows, (512, 256, 128))
    tm_n = _pick(rows, (1024, 512, 256, 128))
    xf = x.reshape(rows, d_model)
    memf = mem.reshape(batch * n_mem, d_model)
    ones = lambda n: jnp.ones((n,), F32)

    col_scale = jnp.concatenate([
        jnp.full((width,), DIFF_QK_DIM ** -0.5 * LOG2E, F32), ones(2 * width),
        jnp.full((width,), HEAD_DIM ** -0.5 * LOG2E, F32), ones(2 * width)])

    for l in range(depth):
        lam_init = _lambda_init(l)
        proj = _norm_matmul(xf, g_mix_pre[l], w_in[l].astype(BF16), col_scale,
                            tm=tm_n, tn=_pick(6 * width, (1536, 1024, 512, 256, 128)))
        lam_vecs = jnp.stack([lambda_q1[l], lambda_k1[l], lambda_q2[l], lambda_k2[l]]).astype(F32)
        out_a = _diff_attn(proj, _diff_bias_vectors(rel_bias, t), lam_vecs, diff_sub_gain[l],
                           batch=batch, seq=seq, n_heads=n_heads, col0=0, t=t, n_g=n_g, lam_init=lam_init)
        out_b = _sb_attn(proj, sb_gain[l], batch=batch, seq=seq, n_heads=n_heads,
                         col0=3 * n_heads, t=t, n_g=n_g)
        xf = _matmul_norm_resid([out_a, out_b], w_out[l].astype(BF16), xf, g_mix_post[l], tm=tm)
        q_mem = _norm_matmul(xf, g_mem_pre[l], w_mq[l].astype(BF16), ones(d_model),
                             tm=tm_n, tn=_pick(d_model, (2048, 1024, 512, 256, 128)))
        kv = _norm_matmul(memf, g_mem_kv[l], w_mkv[l].astype(BF16), ones(2 * d_model),
                          tm=_pick(batch * n_mem, (512, 256, 128)),
                          tn=_pick(2 * d_model, (2048, 1024, 512, 256, 128)))
        o_mem = _mem_attn(q_mem, kv, batch=batch, seq=seq, n_mem=n_mem,
                          tq=_pick(seq, (512, 256, 128)))
        xf = _matmul_norm_resid([o_mem], w_mo[l].astype(BF16), xf, g_mem_post[l], tm=tm)
        act = _norm_swiglu(xf, g_ffn_pre[l], w_gate_up[l].astype(BF16),
                           tm=tm_n, tn=_pick(d_ff, (512, 256, 128)))
        xf = _matmul_norm_resid([act], w_down[l].astype(BF16), xf, g_ffn_post[l], tm=tm)
    return xf.reshape(batch, seq, d_model)
```

```python
import functools
import math

import numpy as np
import jax
import jax.numpy as jnp
from jax import lax
from jax.experimental import pallas as pl
from jax.experimental.pallas import tpu as pltpu

CHUNK = 64
HEAD_DIM = 128
DIFF_QK_DIM = HEAD_DIM // 2
N_HEADS_MEM = 4
N_BUCKETS = 32
MAX_DISTANCE = 128
EPS = 1e-6
LOG2E = math.log2(math.e)
F32_UNDERFLOW_LOG2 = -160.0
F32_SIGN_BIT = 0x80000000

V7X_VMEM_BYTES = 64 * 1024 * 1024
LANES = 128
MXU_DIM = 256
NORM_ROW_CHUNK = 256
HEADS_PER_STEP = 4
SCORES_AHEAD = 4
VMEM_CAP_BYTES = V7X_VMEM_BYTES - 8 * 1024 * 1024

F32 = jnp.float32
BF16 = jnp.bfloat16


def _vmem_limit(estimate_bytes):
    return int(min(VMEM_CAP_BYTES, max(estimate_bytes, 16 * 1024 * 1024)))


def _params(n_axes, vmem_bytes):
    return pltpu.CompilerParams(
        dimension_semantics=("arbitrary",) * n_axes,
        vmem_limit_bytes=_vmem_limit(vmem_bytes))


def _rms_scale(v):
    return lax.rsqrt(jnp.mean(v * v, axis=-1, keepdims=True) + EPS)


def _dot(a, b):
    return jnp.dot(a, b, preferred_element_type=F32)


def _dot_nt(a, b):
    return lax.dot_general(a, b, (((1,), (1,)), ((), ())), preferred_element_type=F32)


def _row_chunk_specs(m, tm, d):
    rc = min(NORM_ROW_CHUNK, tm)
    n_chunks, n_blocks = tm // rc, m // tm

    def chunk_map(r):
        return lambda i, j: (jnp.minimum(i + (j > r).astype(jnp.int32), n_blocks - 1) * n_chunks + r, 0)

    return [pl.BlockSpec((rc, d), chunk_map(r)) for r in range(n_chunks)]


def _norm_rows(x_refs, g_ref, h_ref, finish):
    rc = x_refs[0].shape[0]
    for r, x_ref in enumerate(x_refs):
        rows = slice(r * rc, (r + 1) * rc)
        x = x_ref[...]
        h = (x * _rms_scale(x) * g_ref[...]).astype(BF16)
        h_ref[rows] = h
        finish(rows, h)


def _norm_matmul_kernel(*refs, n_chunks):
    x_refs, (g_ref, w_ref, cs_ref, o_ref, h_ref) = refs[:n_chunks], refs[n_chunks:]
    first = pl.program_id(1) == 0

    def finish(rows, h):
        o_ref[rows] = (_dot(h, w_ref[...]) * cs_ref[...]).astype(o_ref.dtype)

    @pl.when(first)
    def _():
        _norm_rows(x_refs, g_ref, h_ref, finish)

    @pl.when(jnp.logical_not(first))
    def _():
        finish(slice(None), h_ref[...])


def _norm_matmul(x, gain, w, col_scale, *, tm, tn):
    m, d = x.shape
    n = w.shape[1]
    assert m % tm == 0 and n % tn == 0
    x_specs = _row_chunk_specs(m, tm, d)
    vmem = 2 * tm * d * 4 + tm * d * 2 + 2 * d * tn * 2 + 2 * tm * tn * 2 + 2 * tm * tn * 4
    return pl.pallas_call(
        functools.partial(_norm_matmul_kernel, n_chunks=len(x_specs)),
        grid=(m // tm, n // tn),
        in_specs=x_specs + [
            pl.BlockSpec((1, d), lambda i, j: (0, 0)),
            pl.BlockSpec((d, tn), lambda i, j: (0, j)),
            pl.BlockSpec((1, tn), lambda i, j: (0, j)),
        ],
        out_specs=pl.BlockSpec((tm, tn), lambda i, j: (i, j)),
        out_shape=jax.ShapeDtypeStruct((m, n), BF16),
        scratch_shapes=[pltpu.VMEM((tm, d), BF16)],
        compiler_params=_params(2, vmem),
        name="norm_matmul",
    )(*[x] * len(x_specs), gain.reshape(1, d), w, col_scale.reshape(1, n))


def _swiglu(gate, up):
    return gate * (1.0 / (1.0 + jnp.exp(-gate))) * up


def _norm_swiglu_kernel(*refs, n_chunks):
    x_refs, (g_ref, wg_ref, wu_ref, o_ref, h_ref) = refs[:n_chunks], refs[n_chunks:]
    first = pl.program_id(1) == 0

    def finish(rows, h):
        o_ref[rows] = _swiglu(_dot(h, wg_ref[...]), _dot(h, wu_ref[...])).astype(o_ref.dtype)

    @pl.when(first)
    def _():
        _norm_rows(x_refs, g_ref, h_ref, finish)

    @pl.when(jnp.logical_not(first))
    def _():
        finish(slice(None), h_ref[...])


def _norm_swiglu(x, gain, w_gate_up, *, tm, tn):
    m, d = x.shape
    f = w_gate_up.shape[1] // 2
    assert m % tm == 0 and f % tn == 0
    nf = f // tn
    x_specs = _row_chunk_specs(m, tm, d)
    vmem = 2 * tm * d * 4 + tm * d * 2 + 4 * d * tn * 2 + 2 * tm * tn * 2 + 4 * tm * tn * 4
    return pl.pallas_call(
        functools.partial(_norm_swiglu_kernel, n_chunks=len(x_specs)),
        grid=(m // tm, nf),
        in_specs=x_specs + [
            pl.BlockSpec((1, d), lambda i, j: (0, 0)),
            pl.BlockSpec((d, tn), lambda i, j: (0, j)),
            pl.BlockSpec((d, tn), lambda i, j: (0, j + nf)),
        ],
        out_specs=pl.BlockSpec((tm, tn), lambda i, j: (i, j)),
        out_shape=jax.ShapeDtypeStruct((m, f), BF16),
        scratch_shapes=[pltpu.VMEM((tm, d), BF16)],
        compiler_params=_params(2, vmem),
        name="norm_swiglu",
    )(*[x] * len(x_specs), gain.reshape(1, d), w_gate_up, w_gate_up)


def _matmul_norm_resid_kernel(*refs, n_parts):
    lhs_refs, (w_ref, r_ref, g_ref, o_ref) = refs[:n_parts], refs[n_parts:]
    y, row = None, 0
    for l_ref in lhs_refs:
        part = _dot(l_ref[...], w_ref[row:row + l_ref.shape[1], :])
        y = part if y is None else y + part
        row += l_ref.shape[1]
    o_ref[...] = r_ref[...] + y * _rms_scale(y) * g_ref[...]


def _matmul_norm_resid(lhs_parts, w, resid, gain, *, tm):
    m = lhs_parts[0].shape[0]
    kdim, d = w.shape
    assert m % tm == 0 and sum(p.shape[1] for p in lhs_parts) == kdim
    vmem = 2 * tm * kdim * 2 + kdim * d * 2 + 4 * tm * d * 4 + 2 * tm * d * 4
    return pl.pallas_call(
        functools.partial(_matmul_norm_resid_kernel, n_parts=len(lhs_parts)),
        grid=(m // tm,),
        in_specs=[pl.BlockSpec((tm, p.shape[1]), lambda i: (i, 0)) for p in lhs_parts] + [
            pl.BlockSpec((kdim, d), lambda i: (0, 0), pipeline_mode=pl.Buffered(1)),
            pl.BlockSpec((tm, d), lambda i: (i, 0)),
            pl.BlockSpec((1, d), lambda i: (0, 0)),
        ],
        out_specs=pl.BlockSpec((tm, d), lambda i: (i, 0)),
        out_shape=jax.ShapeDtypeStruct((m, d), F32),
        compiler_params=_params(1, vmem),
        name="matmul_norm_resid",
    )(*lhs_parts, w, resid, gain.reshape(1, d))


def _t5_bucket(rel):
    nb = N_BUCKETS // 2
    ret = jnp.where(rel > 0, nb, 0)
    n = jnp.abs(rel)
    max_exact = nb // 2
    nf = jnp.maximum(n, 1).astype(F32)
    large = max_exact + (jnp.log(nf / max_exact) / math.log(MAX_DISTANCE / max_exact)
                         * (nb - max_exact)).astype(jnp.int32)
    large = jnp.minimum(large, nb - 1)
    return ret + jnp.where(n < max_exact, n, large)


def _far_bias_is_constant(t, seq):
    n = np.arange(t + 1, max(seq, t + 2), dtype=np.float64)
    nb, me = N_BUCKETS // 2, N_BUCKETS // 4
    b = np.minimum(me + (np.log(n / me) / math.log(MAX_DISTANCE / me) * (nb - me)).astype(np.int64), nb - 1)
    return bool(np.all(b == nb - 1)) and (t + 1) >= 2 * MAX_DISTANCE


def _diff_bias_vectors(rel_bias, t):
    j = jnp.arange(2 * t, dtype=jnp.int32)
    rel0 = jnp.where(j < t, -j, 2 * t - j)
    vecs = [rel_bias[_t5_bucket(rel0 - d * t)].astype(F32) for d in range(3)]
    out = jnp.transpose(jnp.stack(vecs, axis=0), (2, 0, 1)) * LOG2E
    return out[:, :, None, :]


def _widen(col, width):
    return jnp.concatenate([col] * (width // LANES), axis=1)


def _diff_attn_kernel(q_ref, k_ref, v_ref, bvec_ref, lamv_ref, gain_ref, o_ref,
                      bias_ref, vt_ref, qq_ref, s_ref, m_ref, l_ref, acc_ref, *, t, n_g, lam_init):
    qi = pl.program_id(2)
    hd = HEAD_DIM
    seq = k_ref.shape[0]

    @pl.when(qi == 0)
    def _():
        key = lax.broadcasted_iota(jnp.int32, (t, t), 0)
        qry = lax.broadcasted_iota(jnp.int32, (t, t), 1)
        shift = CHUNK.bit_length() - 1
        allowed = lax.shift_right_logical(key, shift) <= lax.shift_right_logical(qry, shift)
        for g in range(n_g):
            for d in range(2):
                band = jnp.broadcast_to(bvec_ref[g, d], (t, 2 * t))
                tile_b = pltpu.roll(band, 0, 1, stride=1, stride_axis=0)[:, :t]
                bias_ref[g, d] = jnp.where(allowed, tile_b, -jnp.inf) if d == 0 else tile_b
            for c in range(seq // t):
                blk = v_ref[c * t:(c + 1) * t, g * hd:(g + 1) * hd].astype(F32)
                vt_ref[g, c] = blk.T.astype(BF16)

    lane = lax.broadcasted_iota(jnp.int32, (t, hd), 1)
    first = lane < DIFF_QK_DIM
    for g in range(n_g):
        q = q_ref[:, g * hd:(g + 1) * hd].astype(F32)
        qq_ref[g, :t] = jnp.where(first, q, 0.0).astype(BF16)
        qq_ref[g, t:] = jnp.where(first, 0.0, q).astype(BF16)

    m_ref[...] = jnp.full_like(m_ref, -jnp.inf)
    l_ref[...] = jnp.zeros_like(l_ref)
    acc_ref[...] = jnp.zeros_like(acc_ref)

    def tiles(kis, nears):
        offs = [pl.multiple_of(ki * t, t) for ki in kis]
        qc = MXU_DIM
        depth = SCORES_AHEAD
        items = [(i, g, c) for i in range(len(kis)) for g in range(n_g) for c in range(2 * t // qc)]

        def n_keys(i, c):
            return (c * qc) % t + qc if nears[i] == 0 else t

        def scores(n, i, g, c):
            nk = n_keys(i, c)
            s_ref[n % (depth + 1), :nk] = _dot_nt(k_ref[pl.ds(offs[i], nk), g * hd:(g + 1) * hd],
                                                  qq_ref[g, c * qc:(c + 1) * qc])

        def softmax(n, i, g, c):
            cols = slice(c * qc, (c + 1) * qc)
            nk = n_keys(i, c)
            s = s_ref[n % (depth + 1), :nk]
            if nears[i] is not None:
                bc = (c * qc) % t
                s = bias_ref[g, nears[i], :nk, bc:bc + qc] + s
            mx = jnp.max(s, axis=0, keepdims=True)
            m_prev = m_ref[g, :, cols]
            if nears[i] is None:
                far = bvec_ref[g, 2][:, :qc]
                m_new = jnp.maximum(m_prev, mx + far)
                sub = m_new - far
            else:
                m_new = jnp.maximum(m_prev, mx)
                sub = m_new
            alpha = jnp.exp2(m_prev - m_new)
            p = jnp.exp2(s - sub)
            l_ref[g, :, cols] = alpha * l_ref[g, :, cols] + jnp.sum(p, axis=0, keepdims=True)
            m_ref[g, :, cols] = m_new
            return alpha, p.astype(BF16)

        def weigh(i, g, c, alpha, p):
            cols = slice(c * qc, (c + 1) * qc)
            acc_ref[g, :, cols] = alpha * acc_ref[g, :, cols] + _dot(vt_ref[g, kis[i]][:, :n_keys(i, c)], p)

        for n in range(min(depth, len(items))):
            scores(n, *items[n])
        for n, (i, g, c) in enumerate(items):
            if n + depth < len(items):
                scores(n + depth, *items[n + depth])
            weigh(i, g, c, *softmax(n, i, g, c))

    n_far = jnp.maximum(qi - 1, 0)

    def far_pair(j, carry):
        tiles([2 * j, 2 * j + 1], [None, None])
        return carry

    lax.fori_loop(0, n_far // 2, far_pair, 0)

    @pl.when(n_far % 2 == 1)
    def _():
        tiles([n_far - 1], [None])

    @pl.when(qi >= 1)
    def _():
        tiles([qi - 1, qi], [1, 0])

    @pl.when(qi == 0)
    def _():
        tiles([qi], [0])

    lv = lamv_ref[...]
    lam = (jnp.exp(jnp.sum(lv[0:1] * lv[1:2], axis=-1, keepdims=True))
           - jnp.exp(jnp.sum(lv[2:3] * lv[3:4], axis=-1, keepdims=True)) + lam_init)
    for g in range(n_g):
        o = acc_ref[g] / l_ref[g]
        out = o[:, :t] - lam * o[:, t:]
        scale = lax.rsqrt(jnp.mean(out * out, axis=0, keepdims=True) + EPS)
        y = out * scale * gain_ref[...] * (1.0 - lam_init)
        o_ref[:, g * hd:(g + 1) * hd] = y.T.astype(o_ref.dtype)


def _diff_attn(proj, bias_vecs, lam_vecs, sub_gain, *, batch, seq, n_heads, col0, t, n_g, lam_init):
    nq = seq // t
    hd = HEAD_DIM
    assert n_heads % n_g == 0 and col0 % n_g == 0
    ng_blocks = n_heads // n_g
    c0 = col0 // n_g
    w = n_g * hd
    vmem = (4 * t * w * 2 + 4 * seq * w * 2 + n_g * 2 * t * t * 4 + n_g * 2 * t * hd * 2
            + n_g * seq * hd * 2 + n_g * 2 * t * hd * 4 + n_g * 6 * 2 * t * t * 4)
    kern = functools.partial(_diff_attn_kernel, t=t, n_g=n_g, lam_init=lam_init)
    return pl.pallas_call(
        kern,
        grid=(batch, ng_blocks, nq),
        in_specs=[
            pl.BlockSpec((t, w), lambda b, h, i: (b * nq + i, c0 + h)),
            pl.BlockSpec((seq, w), lambda b, h, i: (b, c0 + ng_blocks + h)),
            pl.BlockSpec((seq, w), lambda b, h, i: (b, c0 + 2 * ng_blocks + h)),
            pl.BlockSpec((n_g, 3, 1, 2 * t), lambda b, h, i: (h, 0, 0, 0)),
            pl.BlockSpec((4, DIFF_QK_DIM), lambda b, h, i: (0, 0)),
            pl.BlockSpec((hd, 1), lambda b, h, i: (0, 0)),
        ],
        out_specs=pl.BlockSpec((t, w), lambda b, h, i: (b * nq + i, h)),
        out_shape=jax.ShapeDtypeStruct((batch * seq, n_heads * hd), BF16),
        scratch_shapes=[
            pltpu.VMEM((n_g, 2, t, t), F32),
            pltpu.VMEM((n_g, seq // t, hd, t), BF16),
            pltpu.VMEM((n_g, 2 * t, hd), BF16),
            pltpu.VMEM((SCORES_AHEAD + 1, t, MXU_DIM), F32),
            pltpu.VMEM((n_g, 1, 2 * t), F32),
            pltpu.VMEM((n_g, 1, 2 * t), F32),
            pltpu.VMEM((n_g, hd, 2 * t), F32),
        ],
        compiler_params=_params(3, vmem),
        name="diff_attn",
    )(proj, proj, proj, bias_vecs, lam_vecs, sub_gain.reshape(hd, 1))


def _sb_attn_kernel(q_ref, k_ref, v_ref, tri_ref, gain_ref, o_ref, carry_ref, acc_ref, *, t, cw, n_g):
    qi = pl.program_id(2)
    hd = HEAD_DIM
    head_cols = [slice(g * hd, (g + 1) * hd) for g in range(n_g)]

    def block(row0, nr, key_off, nk, diag):
        rows = slice(row0, row0 + nr)
        tri = tri_ref[...]
        if diag:
            strict = (lax.broadcasted_iota(jnp.int32, (nr, nk), 1)
                      < lax.broadcasted_iota(jnp.int32, (nr, nk), 0) + row0)

        def scores(g):
            return _dot_nt(q_ref[rows, head_cols[g]], k_ref[pl.ds(key_off, nk), head_cols[g]])

        def logits(g, z):
            neg_abs = lax.bitcast_convert_type(
                lax.bitcast_convert_type(z, jnp.uint32) | jnp.uint32(F32_SIGN_BIT), F32)
            neg_soft = jnp.log(1.0 + jnp.exp2(neg_abs)) * (-LOG2E)
            log_keep = neg_soft - jnp.maximum(z, 0.0)
            log_beta = log_keep + z
            if diag:
                log_keep = jnp.where(strict, log_keep, 0.0)
            run = None if diag else carry_ref[g, rows]
            after = [None] * (nk // cw)
            for c in reversed(range(nk // cw)):
                lk = log_keep[:, c * cw:(c + 1) * cw]
                cs = _dot(lk.astype(BF16), tri)
                after[c] = cs if run is None else _widen(run, cw) + cs
                row_sum = jnp.sum(lk, axis=-1, keepdims=True)
                run = jnp.broadcast_to(row_sum, (nr, LANES)) if run is None else run + row_sum
            carry_ref[g, rows] = run
            return log_beta, jnp.concatenate(after, axis=1)

        def mix(g, log_beta, after):
            w = jnp.exp2(log_beta + after)
            if diag:
                w = jnp.where(strict, w, 0.0)
            pv = _dot(w.astype(BF16), v_ref[pl.ds(key_off, nk), head_cols[g]])
            if diag:
                acc_ref[g, rows] = pv
            else:
                acc_ref[g, rows] += pv

        z_next = scores(0)
        for g in range(n_g):
            z = z_next
            if g + 1 < n_g:
                z_next = scores(g + 1)
            mix(g, *logits(g, z))

    tile_off = pl.multiple_of(qi * t, t)
    block(0, t, tile_off, t, True)

    for row0 in range(0, t, t // 2):
        rows = slice(row0, row0 + t // 2)

        def underflowed():
            largest = jnp.max(carry_ref[0, rows])
            for g in range(1, n_g):
                largest = jnp.maximum(largest, jnp.max(carry_ref[g, rows]))
            return largest < F32_UNDERFLOW_LOG2

        def not_done(state):
            j, done = state
            return jnp.logical_and(j < qi * (t // cw), jnp.logical_not(done))

        def body(state):
            j, _ = state
            block(row0, t // 2, pl.multiple_of(tile_off - (j + 1) * cw, cw), cw, False)
            return j + 1, underflowed()

        lax.while_loop(not_done, body, (jnp.int32(0), underflowed()))

    for g in range(n_g):
        out = acc_ref[g]
        o_ref[:, g * hd:(g + 1) * hd] = (out * _rms_scale(out) * gain_ref[...]).astype(o_ref.dtype)


def _sb_attn(proj, out_gain, *, batch, seq, n_heads, col0, t, n_g):
    nq = seq // t
    hd = HEAD_DIM
    assert n_heads % n_g == 0 and col0 % n_g == 0
    ng_blocks = n_heads // n_g
    c0 = col0 // n_g
    wd = n_g * hd
    cw = min(t, MXU_DIM)
    tri = jnp.asarray(np.tril(np.ones((cw, cw), np.float32), -1), BF16)
    vmem = (4 * t * wd * 2 + 4 * seq * wd * 2 + 2 * cw * cw * 2
            + n_g * (t * hd * 4 + t * LANES * 4) + n_g * 8 * t * t * 4)
    kern = functools.partial(_sb_attn_kernel, t=t, cw=cw, n_g=n_g)
    return pl.pallas_call(
        kern,
        grid=(batch, ng_blocks, nq),
        in_specs=[
            pl.BlockSpec((t, wd), lambda b, h, i: (b * nq + i, c0 + h)),
            pl.BlockSpec((seq, wd), lambda b, h, i: (b, c0 + ng_blocks + h)),
            pl.BlockSpec((seq, wd), lambda b, h, i: (b, c0 + 2 * ng_blocks + h)),
            pl.BlockSpec((cw, cw), lambda b, h, i: (0, 0)),
            pl.BlockSpec((1, hd), lambda b, h, i: (0, 0)),
        ],
        out_specs=pl.BlockSpec((t, wd), lambda b, h, i: (b * nq + i, h)),
        out_shape=jax.ShapeDtypeStruct((batch * seq, n_heads * hd), BF16),
        scratch_shapes=[
            pltpu.VMEM((n_g, t, LANES), F32),
            pltpu.VMEM((n_g, t, hd), F32),
        ],
        compiler_params=_params(3, vmem),
        name="sb_attn",
    )(proj, proj, proj, tri, out_gain.reshape(1, hd))


def _mem_attn_kernel(q_ref, k_ref, v_ref, o_ref, *, n_heads, dh):
    scale = dh ** -0.5
    head_cols = [slice(h * dh, (h + 1) * dh) for h in range(n_heads)]

    def scores(h):
        return _dot_nt(q_ref[:, head_cols[h]], k_ref[:, head_cols[h]])

    s_next = scores(0)
    for h in range(n_heads):
        s = s_next * scale
        if h + 1 < n_heads:
            s_next = scores(h + 1)
        p = jnp.exp(s - jnp.max(s, axis=-1, keepdims=True))
        denom = jnp.sum(p, axis=-1, keepdims=True)
        o = _dot(p.astype(BF16), v_ref[:, head_cols[h]]) / denom
        o_ref[:, head_cols[h]] = o.astype(o_ref.dtype)


def _mem_attn(q, kv, *, batch, seq, n_mem, tq):
    d = q.shape[1]
    nq = seq // tq
    vmem = 4 * tq * d * 2 + 4 * n_mem * d * 2 + 8 * tq * n_mem * 4 + 2 * tq * d * 4
    kern = functools.partial(_mem_attn_kernel, n_heads=N_HEADS_MEM, dh=d // N_HEADS_MEM)
    return pl.pallas_call(
        kern,
        grid=(batch, nq),
        in_specs=[
            pl.BlockSpec((tq, d), lambda b, i: (b * nq + i, 0)),
            pl.BlockSpec((n_mem, d), lambda b, i: (b, 0)),
            pl.BlockSpec((n_mem, d), lambda b, i: (b, 1)),
        ],
        out_specs=pl.BlockSpec((tq, d), lambda b, i: (b * nq + i, 0)),
        out_shape=jax.ShapeDtypeStruct((batch * seq, d), BF16),
        compiler_params=_params(2, vmem),
        name="mem_attn",
    )(q, kv, kv)


def _lambda_init(layer_idx):
    return 0.8 - 0.6 * math.exp(-0.3 * layer_idx)


def _pick(n, prefs):
    for p in prefs:
        if n % p == 0:
            return p
    return n


def kernel(x, mem, w_in, w_out, rel_bias, lambda_q1, lambda_k1, lambda_q2, lambda_k2, diff_sub_gain, sb_gain, g_mix_pre, g_mix_post, w_mq, w_mkv, w_mo, g_mem_kv, g_mem_pre, g_mem_post, w_gate_up, w_down, g_ffn_pre, g_ffn_post):
    batch, seq, d_model = x.shape
    n_mem = mem.shape[1]
    depth = w_in.shape[0]
    width = w_out.shape[1] // 2
    n_heads = width // HEAD_DIM
    d_ff = w_down.shape[1]
    t = _pick(seq, (512, 256))
    assert w_in.shape[2] == 6 * width and t % CHUNK == 0
    assert _far_bias_is_constant(t, seq)
    n_g = _pick(n_heads, (HEADS_PER_STEP,))

    rows = batch * seq
    tm = _pick(rows, (512, 256, 128))
    tm_n = _pick(rows, (1024, 512, 256, 128))
    xf = x.reshape(rows, d_model)
    memf = mem.reshape(batch * n_mem, d_model)
    ones = lambda n: jnp.ones((n,), F32)

    col_scale = jnp.concatenate([
        jnp.full((width,), DIFF_QK_DIM ** -0.5 * LOG2E, F32), ones(2 * width),
        jnp.full((width,), HEAD_DIM ** -0.5 * LOG2E, F32), ones(2 * width)])

    for l in range(depth):
        lam_init = _lambda_init(l)
        proj = _norm_matmul(xf, g_mix_pre[l], w_in[l].astype(BF16), col_scale,
                            tm=tm_n, tn=_pick(6 * width, (1536, 1024, 512, 256, 128)))
        lam_vecs = jnp.stack([lambda_q1[l], lambda_k1[l], lambda_q2[l], lambda_k2[l]]).astype(F32)
        out_a = _diff_attn(proj, _diff_bias_vectors(rel_bias, t), lam_vecs, diff_sub_gain[l],
                           batch=batch, seq=seq, n_heads=n_heads, col0=0, t=t, n_g=n_g, lam_init=lam_init)
        out_b = _sb_attn(proj, sb_gain[l], batch=batch, seq=seq, n_heads=n_heads,
                         col0=3 * n_heads, t=t, n_g=n_g)
        xf = _matmul_norm_resid([out_a, out_b], w_out[l].astype(BF16), xf, g_mix_post[l], tm=tm)
        q_mem = _norm_matmul(xf, g_mem_pre[l], w_mq[l].astype(BF16), ones(d_model),
                             tm=tm_n, tn=_pick(d_model, (2048, 1024, 512, 256, 128)))
        kv = _norm_matmul(memf, g_mem_kv[l], w_mkv[l].astype(BF16), ones(2 * d_model),
                          tm=_pick(batch * n_mem, (512, 256, 128)),
                          tn=_pick(2 * d_model, (2048, 1024, 512, 256, 128)))
        o_mem = _mem_attn(q_mem, kv, batch=batch, seq=seq, n_mem=n_mem,
                          tq=_pick(seq, (512, 256, 128)))
        xf = _matmul_norm_resid([o_mem], w_mo[l].astype(BF16), xf, g_mem_post[l], tm=tm)
        act = _norm_swiglu(xf, g_ffn_pre[l], w_gate_up[l].astype(BF16),
                           tm=tm_n, tn=_pick(d_ff, (512, 256, 128)))
        xf = _matmul_norm_resid([act], w_down[l].astype(BF16), xf, g_ffn_post[l], tm=tm)
    return xf.reshape(batch, seq, d_model)
```

```python
import functools
import math

import numpy as np
import jax
import jax.numpy as jnp
from jax import lax
from jax.experimental import pallas as pl
from jax.experimental.pallas import tpu as pltpu

CHUNK = 64
HEAD_DIM = 128
DIFF_QK_DIM = HEAD_DIM // 2
N_HEADS_MEM = 4
N_BUCKETS = 32
MAX_DISTANCE = 128
EPS = 1e-6
LOG2E = math.log2(math.e)
F32_UNDERFLOW_LOG2 = -160.0
F32_SIGN_BIT = 0x80000000

V7X_VMEM_BYTES = 64 * 1024 * 1024
LANES = 128
MXU_DIM = 256
NORM_ROW_CHUNK = 256
HEADS_PER_STEP = 4
SCORES_AHEAD = 4
VMEM_CAP_BYTES = V7X_VMEM_BYTES - 8 * 1024 * 1024

F32 = jnp.float32
BF16 = jnp.bfloat16


def _vmem_limit(estimate_bytes):
    return int(min(VMEM_CAP_BYTES, max(estimate_bytes, 16 * 1024 * 1024)))


def _params(n_axes, vmem_bytes):
    return pltpu.CompilerParams(
        dimension_semantics=("arbitrary",) * n_axes,
        vmem_limit_bytes=_vmem_limit(vmem_bytes))


def _rms_scale(v):
    return lax.rsqrt(jnp.mean(v * v, axis=-1, keepdims=True) + EPS)


def _dot(a, b):
    return jnp.dot(a, b, preferred_element_type=F32)


def _dot_nt(a, b):
    return lax.dot_general(a, b, (((1,), (1,)), ((), ())), preferred_element_type=F32)


def _norm_matmul_kernel(x_ref, g_ref, w_ref, cs_ref, o_ref, h_ref):
    first = pl.program_id(1) == 0

    @pl.when(first)
    def _():
        tm = x_ref.shape[0]
        rc = min(NORM_ROW_CHUNK, tm)
        for r in range(tm // rc):
            rows = slice(r * rc, (r + 1) * rc)
            x = x_ref[rows]
            h = (x * _rms_scale(x) * g_ref[...]).astype(BF16)
            h_ref[rows] = h
            o_ref[rows] = (_dot(h, w_ref[...]) * cs_ref[...]).astype(o_ref.dtype)

    @pl.when(jnp.logical_not(first))
    def _():
        acc = _dot(h_ref[...], w_ref[...])
        o_ref[...] = (acc * cs_ref[...]).astype(o_ref.dtype)


def _norm_matmul(x, gain, w, col_scale, *, tm, tn):
    m, d = x.shape
    n = w.shape[1]
    assert m % tm == 0 and n % tn == 0
    vmem = 2 * tm * d * 4 + tm * d * 2 + 2 * d * tn * 2 + 2 * tm * tn * 2 + 2 * tm * tn * 4
    return pl.pallas_call(
        _norm_matmul_kernel,
        grid=(m // tm, n // tn),
        in_specs=[
            pl.BlockSpec((tm, d), lambda i, j: (i, 0)),
            pl.BlockSpec((1, d), lambda i, j: (0, 0)),
            pl.BlockSpec((d, tn), lambda i, j: (0, j)),
            pl.BlockSpec((1, tn), lambda i, j: (0, j)),
        ],
        out_specs=pl.BlockSpec((tm, tn), lambda i, j: (i, j)),
        out_shape=jax.ShapeDtypeStruct((m, n), BF16),
        scratch_shapes=[pltpu.VMEM((tm, d), BF16)],
        compiler_params=_params(2, vmem),
        name="norm_matmul",
    )(x, gain.reshape(1, d), w, col_scale.reshape(1, n))


def _swiglu(gate, up):
    return gate * (1.0 / (1.0 + jnp.exp(-gate))) * up


def _norm_swiglu_kernel(x_ref, g_ref, w_ref, o_ref, h_ref):
    first = pl.program_id(1) == 0
    tn = o_ref.shape[1]

    def gated(h):
        w = w_ref[0]
        return _swiglu(_dot(h, w[:, :tn]), _dot(h, w[:, tn:])).astype(o_ref.dtype)

    @pl.when(first)
    def _():
        tm = x_ref.shape[0]
        rc = min(NORM_ROW_CHUNK, tm)
        for r in range(tm // rc):
            rows = slice(r * rc, (r + 1) * rc)
            x = x_ref[rows]
            h = (x * _rms_scale(x) * g_ref[...]).astype(BF16)
            h_ref[rows] = h
            o_ref[rows] = gated(h)

    @pl.when(jnp.logical_not(first))
    def _():
        o_ref[...] = gated(h_ref[...])


def _norm_swiglu(x, gain, w_gate_up, *, tm, tn):
    m, d = x.shape
    f = w_gate_up.shape[1] // 2
    assert m % tm == 0 and f % tn == 0
    nf = f // tn
    w_steps = w_gate_up.reshape(d, 2, nf, tn).transpose(2, 0, 1, 3).reshape(nf, d, 2 * tn)
    vmem = 2 * tm * d * 4 + tm * d * 2 + 4 * d * tn * 2 + 2 * tm * tn * 2 + 4 * tm * tn * 4
    return pl.pallas_call(
        _norm_swiglu_kernel,
        grid=(m // tm, nf),
        in_specs=[
            pl.BlockSpec((tm, d), lambda i, j: (i, 0)),
            pl.BlockSpec((1, d), lambda i, j: (0, 0)),
            pl.BlockSpec((1, d, 2 * tn), lambda i, j: (j, 0, 0)),
        ],
        out_specs=pl.BlockSpec((tm, tn), lambda i, j: (i, j)),
        out_shape=jax.ShapeDtypeStruct((m, f), BF16),
        scratch_shapes=[pltpu.VMEM((tm, d), BF16)],
        compiler_params=_params(2, vmem),
        name="norm_swiglu",
    )(x, gain.reshape(1, d), w_steps)


def _matmul_norm_resid_kernel(*refs, n_parts):
    lhs_refs, (w_ref, r_ref, g_ref, o_ref) = refs[:n_parts], refs[n_parts:]
    y, row = None, 0
    for l_ref in lhs_refs:
        part = _dot(l_ref[...], w_ref[row:row + l_ref.shape[1], :])
        y = part if y is None else y + part
        row += l_ref.shape[1]
    o_ref[...] = r_ref[...] + y * _rms_scale(y) * g_ref[...]


def _matmul_norm_resid(lhs_parts, w, resid, gain, *, tm):
    m = lhs_parts[0].shape[0]
    kdim, d = w.shape
    assert m % tm == 0 and sum(p.shape[1] for p in lhs_parts) == kdim
    vmem = 2 * tm * kdim * 2 + kdim * d * 2 + 4 * tm * d * 4 + 2 * tm * d * 4
    return pl.pallas_call(
        functools.partial(_matmul_norm_resid_kernel, n_parts=len(lhs_parts)),
        grid=(m // tm,),
        in_specs=[pl.BlockSpec((tm, p.shape[1]), lambda i: (i, 0)) for p in lhs_parts] + [
            pl.BlockSpec((kdim, d), lambda i: (0, 0), pipeline_mode=pl.Buffered(1)),
            pl.BlockSpec((tm, d), lambda i: (i, 0)),
            pl.BlockSpec((1, d), lambda i: (0, 0)),
        ],
        out_specs=pl.BlockSpec((tm, d), lambda i: (i, 0)),
        out_shape=jax.ShapeDtypeStruct((m, d), F32),
        compiler_params=_params(1, vmem),
        name="matmul_norm_resid",
    )(*lhs_parts, w, resid, gain.reshape(1, d))


def _t5_bucket(rel):
    nb = N_BUCKETS // 2
    ret = jnp.where(rel > 0, nb, 0)
    n = jnp.abs(rel)
    max_exact = nb // 2
    nf = jnp.maximum(n, 1).astype(F32)
    large = max_exact + (jnp.log(nf / max_exact) / math.log(MAX_DISTANCE / max_exact)
                         * (nb - max_exact)).astype(jnp.int32)
    large = jnp.minimum(large, nb - 1)
    return ret + jnp.where(n < max_exact, n, large)


def _far_bias_is_constant(t, seq):
    n = np.arange(t + 1, max(seq, t + 2), dtype=np.float64)
    nb, me = N_BUCKETS // 2, N_BUCKETS // 4
    b = np.minimum(me + (np.log(n / me) / math.log(MAX_DISTANCE / me) * (nb - me)).astype(np.int64), nb - 1)
    return bool(np.all(b == nb - 1)) and (t + 1) >= 2 * MAX_DISTANCE


def _diff_bias_vectors(rel_bias, t):
    j = jnp.arange(2 * t, dtype=jnp.int32)
    rel0 = jnp.where(j < t, -j, 2 * t - j)
    vecs = [rel_bias[_t5_bucket(rel0 - d * t)].astype(F32) for d in range(3)]
    out = jnp.transpose(jnp.stack(vecs, axis=0), (2, 0, 1)) * LOG2E
    return out[:, :, None, :]


def _widen(col, width):
    return jnp.concatenate([col] * (width // LANES), axis=1)


def _diff_attn_kernel(q_ref, k_ref, v_ref, bvec_ref, lamv_ref, gain_ref, o_ref,
                      bias_ref, vt_ref, qq_ref, s_ref, m_ref, l_ref, acc_ref, *, t, n_g, lam_init):
    qi = pl.program_id(2)
    hd = HEAD_DIM
    seq = k_ref.shape[0]

    @pl.when(qi == 0)
    def _():
        key = lax.broadcasted_iota(jnp.int32, (t, t), 0)
        qry = lax.broadcasted_iota(jnp.int32, (t, t), 1)
        shift = CHUNK.bit_length() - 1
        allowed = lax.shift_right_logical(key, shift) <= lax.shift_right_logical(qry, shift)
        for g in range(n_g):
            for d in range(2):
                band = jnp.broadcast_to(bvec_ref[g, d], (t, 2 * t))
                tile_b = pltpu.roll(band, 0, 1, stride=1, stride_axis=0)[:, :t]
                bias_ref[g, d] = jnp.where(allowed, tile_b, -jnp.inf) if d == 0 else tile_b
            for c in range(seq // t):
                blk = v_ref[c * t:(c + 1) * t, g * hd:(g + 1) * hd].astype(F32)
                vt_ref[g, c] = blk.T.astype(BF16)

    lane = lax.broadcasted_iota(jnp.int32, (t, hd), 1)
    first = lane < DIFF_QK_DIM
    for g in range(n_g):
        q = q_ref[:, g * hd:(g + 1) * hd].astype(F32)
        qq_ref[g, :t] = jnp.where(first, q, 0.0).astype(BF16)
        qq_ref[g, t:] = jnp.where(first, 0.0, q).astype(BF16)

    m_ref[...] = jnp.full_like(m_ref, -jnp.inf)
    l_ref[...] = jnp.zeros_like(l_ref)
    acc_ref[...] = jnp.zeros_like(acc_ref)

    def tiles(kis, nears):
        offs = [pl.multiple_of(ki * t, t) for ki in kis]
        qc = MXU_DIM
        depth = SCORES_AHEAD
        items = [(i, g, c) for i in range(len(kis)) for g in range(n_g) for c in range(2 * t // qc)]

        def n_keys(i, c):
            return (c * qc) % t + qc if nears[i] == 0 else t

        def scores(n, i, g, c):
            nk = n_keys(i, c)
            s_ref[n % (depth + 1), :nk] = _dot_nt(k_ref[pl.ds(offs[i], nk), g * hd:(g + 1) * hd],
                                                  qq_ref[g, c * qc:(c + 1) * qc])

        def softmax(n, i, g, c):
            cols = slice(c * qc, (c + 1) * qc)
            nk = n_keys(i, c)
            s = s_ref[n % (depth + 1), :nk]
            if nears[i] is not None:
                bc = (c * qc) % t
                s = bias_ref[g, nears[i], :nk, bc:bc + qc] + s
            mx = jnp.max(s, axis=0, keepdims=True)
            m_prev = m_ref[g, :, cols]
            if nears[i] is None:
                far = bvec_ref[g, 2][:, :qc]
                m_new = jnp.maximum(m_prev, mx + far)
                sub = m_new - far
            else:
                m_new = jnp.maximum(m_prev, mx)
                sub = m_new
            alpha = jnp.exp2(m_prev - m_new)
            p = jnp.exp2(s - sub)
            l_ref[g, :, cols] = alpha * l_ref[g, :, cols] + jnp.sum(p, axis=0, keepdims=True)
            m_ref[g, :, cols] = m_new
            return alpha, p.astype(BF16)

        def weigh(i, g, c, alpha, p):
            cols = slice(c * qc, (c + 1) * qc)
            acc_ref[g, :, cols] = alpha * acc_ref[g, :, cols] + _dot(vt_ref[g, kis[i]][:, :n_keys(i, c)], p)

        for n in range(min(depth, len(items))):
            scores(n, *items[n])
        for n, (i, g, c) in enumerate(items):
            if n + depth < len(items):
                scores(n + depth, *items[n + depth])
            weigh(i, g, c, *softmax(n, i, g, c))

    n_far = jnp.maximum(qi - 1, 0)

    def far_pair(j, carry):
        tiles([2 * j, 2 * j + 1], [None, None])
        return carry

    lax.fori_loop(0, n_far // 2, far_pair, 0)

    @pl.when(n_far % 2 == 1)
    def _():
        tiles([n_far - 1], [None])

    @pl.when(qi >= 1)
    def _():
        tiles([qi - 1, qi], [1, 0])

    @pl.when(qi == 0)
    def _():
        tiles([qi], [0])

    lv = lamv_ref[...]
    lam = (jnp.exp(jnp.sum(lv[0:1] * lv[1:2], axis=-1, keepdims=True))
           - jnp.exp(jnp.sum(lv[2:3] * lv[3:4], axis=-1, keepdims=True)) + lam_init)
    for g in range(n_g):
        o = acc_ref[g] / l_ref[g]
        out = o[:, :t] - lam * o[:, t:]
        scale = lax.rsqrt(jnp.mean(out * out, axis=0, keepdims=True) + EPS)
        y = out * scale * gain_ref[...] * (1.0 - lam_init)
        o_ref[:, g * hd:(g + 1) * hd] = y.T.astype(o_ref.dtype)


def _diff_attn(proj, bias_vecs, lam_vecs, sub_gain, *, batch, seq, n_heads, col0, t, n_g, lam_init):
    nq = seq // t
    hd = HEAD_DIM
    assert n_heads % n_g == 0 and col0 % n_g == 0
    ng_blocks = n_heads // n_g
    c0 = col0 // n_g
    w = n_g * hd
    vmem = (4 * t * w * 2 + 4 * seq * w * 2 + n_g * 2 * t * t * 4 + n_g * 2 * t * hd * 2
            + n_g * seq * hd * 2 + n_g * 2 * t * hd * 4 + n_g * 6 * 2 * t * t * 4)
    kern = functools.partial(_diff_attn_kernel, t=t, n_g=n_g, lam_init=lam_init)
    return pl.pallas_call(
        kern,
        grid=(batch, ng_blocks, nq),
        in_specs=[
            pl.BlockSpec((t, w), lambda b, h, i: (b * nq + i, c0 + h)),
            pl.BlockSpec((seq, w), lambda b, h, i: (b, c0 + ng_blocks + h)),
            pl.BlockSpec((seq, w), lambda b, h, i: (b, c0 + 2 * ng_blocks + h)),
            pl.BlockSpec((n_g, 3, 1, 2 * t), lambda b, h, i: (h, 0, 0, 0)),
            pl.BlockSpec((4, DIFF_QK_DIM), lambda b, h, i: (0, 0)),
            pl.BlockSpec((hd, 1), lambda b, h, i: (0, 0)),
        ],
        out_specs=pl.BlockSpec((t, w), lambda b, h, i: (b * nq + i, h)),
        out_shape=jax.ShapeDtypeStruct((batch * seq, n_heads * hd), BF16),
        scratch_shapes=[
            pltpu.VMEM((n_g, 2, t, t), F32),
            pltpu.VMEM((n_g, seq // t, hd, t), BF16),
            pltpu.VMEM((n_g, 2 * t, hd), BF16),
            pltpu.VMEM((SCORES_AHEAD + 1, t, MXU_DIM), F32),
            pltpu.VMEM((n_g, 1, 2 * t), F32),
            pltpu.VMEM((n_g, 1, 2 * t), F32),
            pltpu.VMEM((n_g, hd, 2 * t), F32),
        ],
        compiler_params=_params(3, vmem),
        name="diff_attn",
    )(proj, proj, proj, bias_vecs, lam_vecs, sub_gain.reshape(hd, 1))


def _sb_attn_kernel(q_ref, k_ref, v_ref, tri_ref, gain_ref, o_ref, carry_ref, acc_ref, *, t, cw, n_g):
    qi = pl.program_id(2)
    hd = HEAD_DIM
    head_cols = [slice(g * hd, (g + 1) * hd) for g in range(n_g)]

    def block(row0, nr, key_off, nk, diag):
        rows = slice(row0, row0 + nr)
        tri = tri_ref[...]
        if diag:
            strict = (lax.broadcasted_iota(jnp.int32, (nr, nk), 1)
                      < lax.broadcasted_iota(jnp.int32, (nr, nk), 0) + row0)

        def scores(g):
            return _dot_nt(q_ref[rows, head_cols[g]], k_ref[pl.ds(key_off, nk), head_cols[g]])

        def logits(g, z):
            neg_abs = lax.bitcast_convert_type(
                lax.bitcast_convert_type(z, jnp.uint32) | jnp.uint32(F32_SIGN_BIT), F32)
            neg_soft = jnp.log(1.0 + jnp.exp2(neg_abs)) * (-LOG2E)
            log_keep = neg_soft - jnp.maximum(z, 0.0)
            log_beta = log_keep + z
            if diag:
                log_keep = jnp.where(strict, log_keep, 0.0)
            run = None if diag else carry_ref[g, rows]
            after = [None] * (nk // cw)
            for c in reversed(range(nk // cw)):
                lk = log_keep[:, c * cw:(c + 1) * cw]
                cs = _dot(lk.astype(BF16), tri)
                after[c] = cs if run is None else _widen(run, cw) + cs
                row_sum = jnp.sum(lk, axis=-1, keepdims=True)
                run = jnp.broadcast_to(row_sum, (nr, LANES)) if run is None else run + row_sum
            carry_ref[g, rows] = run
            return log_beta, jnp.concatenate(after, axis=1)

        def mix(g, log_beta, after):
            w = jnp.exp2(log_beta + after)
            if diag:
                w = jnp.where(strict, w, 0.0)
            pv = _dot(w.astype(BF16), v_ref[pl.ds(key_off, nk), head_cols[g]])
            if diag:
                acc_ref[g, rows] = pv
            else:
                acc_ref[g, rows] += pv

        z_next = scores(0)
        for g in range(n_g):
            z = z_next
            if g + 1 < n_g:
                z_next = scores(g + 1)
            mix(g, *logits(g, z))

    tile_off = pl.multiple_of(qi * t, t)
    block(0, t, tile_off, t, True)

    for row0 in range(0, t, t // 2):
        rows = slice(row0, row0 + t // 2)

        def underflowed():
            largest = jnp.max(carry_ref[0, rows])
            for g in range(1, n_g):
                largest = jnp.maximum(largest, jnp.max(carry_ref[g, rows]))
            return largest < F32_UNDERFLOW_LOG2

        def not_done(state):
            j, done = state
            return jnp.logical_and(j < qi * (t // cw), jnp.logical_not(done))

        def body(state):
            j, _ = state
            block(row0, t // 2, pl.multiple_of(tile_off - (j + 1) * cw, cw), cw, False)
            return j + 1, underflowed()

        lax.while_loop(not_done, body, (jnp.int32(0), underflowed()))

    for g in range(n_g):
        out = acc_ref[g]
        o_ref[:, g * hd:(g + 1) * hd] = (out * _rms_scale(out) * gain_ref[...]).astype(o_ref.dtype)


def _sb_attn(proj, out_gain, *, batch, seq, n_heads, col0, t, n_g):
    nq = seq // t
    hd = HEAD_DIM
    assert n_heads % n_g == 0 and col0 % n_g == 0
    ng_blocks = n_heads // n_g
    c0 = col0 // n_g
    wd = n_g * hd
    cw = min(t, MXU_DIM)
    tri = jnp.asarray(np.tril(np.ones((cw, cw), np.float32), -1), BF16)
    vmem = (4 * t * wd * 2 + 4 * seq * wd * 2 + 2 * cw * cw * 2
            + n_g * (t * hd * 4 + t * LANES * 4) + n_g * 8 * t * t * 4)
    kern = functools.partial(_sb_attn_kernel, t=t, cw=cw, n_g=n_g)
    return pl.pallas_call(
        kern,
        grid=(batch, ng_blocks, nq),
        in_specs=[
            pl.BlockSpec((t, wd), lambda b, h, i: (b * nq + i, c0 + h)),
            pl.BlockSpec((seq, wd), lambda b, h, i: (b, c0 + ng_blocks + h)),
            pl.BlockSpec((seq, wd), lambda b, h, i: (b, c0 + 2 * ng_blocks + h)),
            pl.BlockSpec((cw, cw), lambda b, h, i: (0, 0)),
            pl.BlockSpec((1, hd), lambda b, h, i: (0, 0)),
        ],
        out_specs=pl.BlockSpec((t, wd), lambda b, h, i: (b * nq + i, h)),
        out_shape=jax.ShapeDtypeStruct((batch * seq, n_heads * hd), BF16),
        scratch_shapes=[
            pltpu.VMEM((n_g, t, LANES), F32),
            pltpu.VMEM((n_g, t, hd), F32),
        ],
        compiler_params=_params(3, vmem),
        name="sb_attn",
    )(proj, proj, proj, tri, out_gain.reshape(1, hd))


def _mem_attn_kernel(q_ref, k_ref, v_ref, o_ref, *, n_heads, dh):
    scale = dh ** -0.5
    head_cols = [slice(h * dh, (h + 1) * dh) for h in range(n_heads)]

    def scores(h):
        return _dot_nt(q_ref[:, head_cols[h]], k_ref[:, head_cols[h]])

    s_next = scores(0)
    for h in range(n_heads):
        s = s_next * scale
        if h + 1 < n_heads:
            s_next = scores(h + 1)
        p = jnp.exp(s - jnp.max(s, axis=-1, keepdims=True))
        denom = jnp.sum(p, axis=-1, keepdims=True)
        o = _dot(p.astype(BF16), v_ref[:, head_cols[h]]) / denom
        o_ref[:, head_cols[h]] = o.astype(o_ref.dtype)


def _mem_attn(q, kv, *, batch, seq, n_mem, tq):
    d = q.shape[1]
    nq = seq // tq
    vmem = 4 * tq * d * 2 + 4 * n_mem * d * 2 + 8 * tq * n_mem * 4 + 2 * tq * d * 4
    kern = functools.partial(_mem_attn_kernel, n_heads=N_HEADS_MEM, dh=d // N_HEADS_MEM)
    return pl.pallas_call(
        kern,
        grid=(batch, nq),
        in_specs=[
            pl.BlockSpec((tq, d), lambda b, i: (b * nq + i, 0)),
            pl.BlockSpec((n_mem, d), lambda b, i: (b, 0)),
            pl.BlockSpec((n_mem, d), lambda b, i: (b, 1)),
        ],
        out_specs=pl.BlockSpec((tq, d), lambda b, i: (b * nq + i, 0)),
        out_shape=jax.ShapeDtypeStruct((batch * seq, d), BF16),
        compiler_params=_params(2, vmem),
        name="mem_attn",
    )(q, kv, kv)


def _lambda_init(layer_idx):
    return 0.8 - 0.6 * math.exp(-0.3 * layer_idx)


def _pick(n, prefs):
    for p in prefs:
        if n % p == 0:
            return p
    return n


def kernel(x, mem, w_in, w_out, rel_bias, lambda_q1, lambda_k1, lambda_q2, lambda_k2, diff_sub_gain, sb_gain, g_mix_pre, g_mix_post, w_mq, w_mkv, w_mo, g_mem_kv, g_mem_pre, g_mem_post, w_gate_up, w_down, g_ffn_pre, g_ffn_post):
    batch, seq, d_model = x.shape
    n_mem = mem.shape[1]
    depth = w_in.shape[0]
    width = w_out.shape[1] // 2
    n_heads = width // HEAD_DIM
    d_ff = w_down.shape[1]
    t = _pick(seq, (512, 256))
    assert w_in.shape[2] == 6 * width and t % CHUNK == 0
    assert _far_bias_is_constant(t, seq)
    n_g = _pick(n_heads, (HEADS_PER_STEP,))

    rows = batch * seq
    tm = _pick(rows, (512, 256, 128))
    tm_n = _pick(rows, (1024, 512, 256, 128))
    xf = x.reshape(rows, d_model)
    memf = mem.reshape(batch * n_mem, d_model)
    ones = lambda n: jnp.ones((n,), F32)

    col_scale = jnp.concatenate([
        jnp.full((width,), DIFF_QK_DIM ** -0.5 * LOG2E, F32), ones(2 * width),
        jnp.full((width,), HEAD_DIM ** -0.5 * LOG2E, F32), ones(2 * width)])

    for l in range(depth):
        lam_init = _lambda_init(l)
        proj = _norm_matmul(xf, g_mix_pre[l], w_in[l].astype(BF16), col_scale,
                            tm=tm_n, tn=_pick(6 * width, (1536, 1024, 512, 256, 128)))
        lam_vecs = jnp.stack([lambda_q1[l], lambda_k1[l], lambda_q2[l], lambda_k2[l]]).astype(F32)
        out_a = _diff_attn(proj, _diff_bias_vectors(rel_bias, t), lam_vecs, diff_sub_gain[l],
                           batch=batch, seq=seq, n_heads=n_heads, col0=0, t=t, n_g=n_g, lam_init=lam_init)
        out_b = _sb_attn(proj, sb_gain[l], batch=batch, seq=seq, n_heads=n_heads,
                         col0=3 * n_heads, t=t, n_g=n_g)
        xf = _matmul_norm_resid([out_a, out_b], w_out[l].astype(BF16), xf, g_mix_post[l], tm=tm)
        q_mem = _norm_matmul(xf, g_mem_pre[l], w_mq[l].astype(BF16), ones(d_model),
                             tm=tm_n, tn=_pick(d_model, (2048, 1024, 512, 256, 128)))
        kv = _norm_matmul(memf, g_mem_kv[l], w_mkv[l].astype(BF16), ones(2 * d_model),
                          tm=_pick(batch * n_mem, (512, 256, 128)),
                          tn=_pick(2 * d_model, (2048, 1024, 512, 256, 128)))
        o_mem = _mem_attn(q_mem, kv, batch=batch, seq=seq, n_mem=n_mem,
                          tq=_pick(seq, (512, 256, 128)))
        xf = _matmul_norm_resid([o_mem], w_mo[l].astype(BF16), xf, g_mem_post[l], tm=tm)
        act = _norm_swiglu(xf, g_ffn_pre[l], w_gate_up[l].astype(BF16),
                           tm=tm_n, tn=_pick(d_ff, (512, 256, 128)))
        xf = _matmul_norm_resid([act], w_down[l].astype(BF16), xf, g_ffn_post[l], tm=tm)
    return xf.reshape(batch, seq, d_model)
```

```python
import functools
import math

import numpy as np
import jax
import jax.numpy as jnp
from jax import lax
from jax.experimental import pallas as pl
from jax.experimental.pallas import tpu as pltpu

CHUNK = 64
HEAD_DIM = 128
DIFF_QK_DIM = HEAD_DIM // 2
N_HEADS_MEM = 4
N_BUCKETS = 32
MAX_DISTANCE = 128
EPS = 1e-6
LOG2E = math.log2(math.e)
F32_UNDERFLOW_LOG2 = -160.0
F32_SIGN_BIT = 0x80000000

V7X_VMEM_BYTES = 64 * 1024 * 1024
LANES = 128
MXU_DIM = 256
NORM_ROW_CHUNK = 256
HEADS_PER_STEP = 4
SCORES_AHEAD = 3
VMEM_CAP_BYTES = V7X_VMEM_BYTES - 8 * 1024 * 1024

F32 = jnp.float32
BF16 = jnp.bfloat16


def _vmem_limit(estimate_bytes):
    return int(min(VMEM_CAP_BYTES, max(estimate_bytes, 16 * 1024 * 1024)))


def _params(n_axes, vmem_bytes):
    return pltpu.CompilerParams(
        dimension_semantics=("arbitrary",) * n_axes,
        vmem_limit_bytes=_vmem_limit(vmem_bytes))


def _rms_scale(v):
    return lax.rsqrt(jnp.mean(v * v, axis=-1, keepdims=True) + EPS)


def _dot(a, b):
    return jnp.dot(a, b, preferred_element_type=F32)


def _dot_nt(a, b):
    return lax.dot_general(a, b, (((1,), (1,)), ((), ())), preferred_element_type=F32)


def _norm_matmul_kernel(x_ref, g_ref, w_ref, cs_ref, o_ref, h_ref):
    first = pl.program_id(1) == 0

    @pl.when(first)
    def _():
        tm = x_ref.shape[0]
        rc = min(NORM_ROW_CHUNK, tm)
        for r in range(tm // rc):
            rows = slice(r * rc, (r + 1) * rc)
            x = x_ref[rows]
            h = (x * _rms_scale(x) * g_ref[...]).astype(BF16)
            h_ref[rows] = h
            o_ref[rows] = (_dot(h, w_ref[...]) * cs_ref[...]).astype(o_ref.dtype)

    @pl.when(jnp.logical_not(first))
    def _():
        acc = _dot(h_ref[...], w_ref[...])
        o_ref[...] = (acc * cs_ref[...]).astype(o_ref.dtype)


def _norm_matmul(x, gain, w, col_scale, *, tm, tn):
    m, d = x.shape
    n = w.shape[1]
    assert m % tm == 0 and n % tn == 0
    vmem = 2 * tm * d * 4 + tm * d * 2 + 2 * d * tn * 2 + 2 * tm * tn * 2 + 2 * tm * tn * 4
    return pl.pallas_call(
        _norm_matmul_kernel,
        grid=(m // tm, n // tn),
        in_specs=[
            pl.BlockSpec((tm, d), lambda i, j: (i, 0)),
            pl.BlockSpec((1, d), lambda i, j: (0, 0)),
            pl.BlockSpec((d, tn), lambda i, j: (0, j)),
            pl.BlockSpec((1, tn), lambda i, j: (0, j)),
        ],
        out_specs=pl.BlockSpec((tm, tn), lambda i, j: (i, j)),
        out_shape=jax.ShapeDtypeStruct((m, n), BF16),
        scratch_shapes=[pltpu.VMEM((tm, d), BF16)],
        compiler_params=_params(2, vmem),
        name="norm_matmul",
    )(x, gain.reshape(1, d), w, col_scale.reshape(1, n))


def _swiglu(gate, up):
    return gate * (1.0 / (1.0 + jnp.exp(-gate))) * up


def _norm_swiglu_kernel(x_ref, g_ref, wg_ref, wu_ref, o_ref, h_ref):
    first = pl.program_id(1) == 0

    @pl.when(first)
    def _():
        tm = x_ref.shape[0]
        rc = min(NORM_ROW_CHUNK, tm)
        for r in range(tm // rc):
            rows = slice(r * rc, (r + 1) * rc)
            x = x_ref[rows]
            h = (x * _rms_scale(x) * g_ref[...]).astype(BF16)
            h_ref[rows] = h
            o_ref[rows] = _swiglu(_dot(h, wg_ref[...]), _dot(h, wu_ref[...])).astype(o_ref.dtype)

    @pl.when(jnp.logical_not(first))
    def _():
        h = h_ref[...]
        o_ref[...] = _swiglu(_dot(h, wg_ref[...]), _dot(h, wu_ref[...])).astype(o_ref.dtype)


def _norm_swiglu(x, gain, w_gate_up, *, tm, tn):
    m, d = x.shape
    f = w_gate_up.shape[1] // 2
    assert m % tm == 0 and f % tn == 0
    nf = f // tn
    vmem = 2 * tm * d * 4 + tm * d * 2 + 4 * d * tn * 2 + 2 * tm * tn * 2 + 4 * tm * tn * 4
    return pl.pallas_call(
        _norm_swiglu_kernel,
        grid=(m // tm, nf),
        in_specs=[
            pl.BlockSpec((tm, d), lambda i, j: (i, 0)),
            pl.BlockSpec((1, d), lambda i, j: (0, 0)),
            pl.BlockSpec((d, tn), lambda i, j: (0, j)),
            pl.BlockSpec((d, tn), lambda i, j: (0, j + nf)),
        ],
        out_specs=pl.BlockSpec((tm, tn), lambda i, j: (i, j)),
        out_shape=jax.ShapeDtypeStruct((m, f), BF16),
        scratch_shapes=[pltpu.VMEM((tm, d), BF16)],
        compiler_params=_params(2, vmem),
        name="norm_swiglu",
    )(x, gain.reshape(1, d), w_gate_up, w_gate_up)


def _matmul_norm_resid_kernel(*refs, n_parts):
    lhs_refs, (w_ref, r_ref, g_ref, o_ref) = refs[:n_parts], refs[n_parts:]
    y, row = None, 0
    for l_ref in lhs_refs:
        part = _dot(l_ref[...], w_ref[row:row + l_ref.shape[1], :])
        y = part if y is None else y + part
        row += l_ref.shape[1]
    o_ref[...] = r_ref[...] + y * _rms_scale(y) * g_ref[...]


def _matmul_norm_resid(lhs_parts, w, resid, gain, *, tm):
    m = lhs_parts[0].shape[0]
    kdim, d = w.shape
    assert m % tm == 0 and sum(p.shape[1] for p in lhs_parts) == kdim
    vmem = 2 * tm * kdim * 2 + kdim * d * 2 + 4 * tm * d * 4 + 2 * tm * d * 4
    return pl.pallas_call(
        functools.partial(_matmul_norm_resid_kernel, n_parts=len(lhs_parts)),
        grid=(m // tm,),
        in_specs=[pl.BlockSpec((tm, p.shape[1]), lambda i: (i, 0)) for p in lhs_parts] + [
            pl.BlockSpec((kdim, d), lambda i: (0, 0), pipeline_mode=pl.Buffered(1)),
            pl.BlockSpec((tm, d), lambda i: (i, 0)),
            pl.BlockSpec((1, d), lambda i: (0, 0)),
        ],
        out_specs=pl.BlockSpec((tm, d), lambda i: (i, 0)),
        out_shape=jax.ShapeDtypeStruct((m, d), F32),
        compiler_params=_params(1, vmem),
        name="matmul_norm_resid",
    )(*lhs_parts, w, resid, gain.reshape(1, d))


def _t5_bucket(rel):
    nb = N_BUCKETS // 2
    ret = jnp.where(rel > 0, nb, 0)
    n = jnp.abs(rel)
    max_exact = nb // 2
    nf = jnp.maximum(n, 1).astype(F32)
    large = max_exact + (jnp.log(nf / max_exact) / math.log(MAX_DISTANCE / max_exact)
                         * (nb - max_exact)).astype(jnp.int32)
    large = jnp.minimum(large, nb - 1)
    return ret + jnp.where(n < max_exact, n, large)


def _far_bias_is_constant(t, seq):
    n = np.arange(t + 1, max(seq, t + 2), dtype=np.float64)
    nb, me = N_BUCKETS // 2, N_BUCKETS // 4
    b = np.minimum(me + (np.log(n / me) / math.log(MAX_DISTANCE / me) * (nb - me)).astype(np.int64), nb - 1)
    return bool(np.all(b == nb - 1)) and (t + 1) >= 2 * MAX_DISTANCE


def _diff_bias_vectors(rel_bias, t):
    j = jnp.arange(2 * t, dtype=jnp.int32)
    rel0 = jnp.where(j < t, -j, 2 * t - j)
    vecs = [rel_bias[_t5_bucket(rel0 - d * t)].astype(F32) for d in range(3)]
    out = jnp.transpose(jnp.stack(vecs, axis=0), (2, 0, 1)) * LOG2E
    return out[:, :, None, :]


def _widen(col, width):
    return jnp.concatenate([col] * (width // LANES), axis=1)


def _diff_attn_kernel(q_ref, k_ref, v_ref, bvec_ref, lamv_ref, gain_ref, o_ref,
                      bias_ref, vt_ref, qq_ref, s_ref, m_ref, l_ref, acc_ref, *, t, n_g, lam_init):
    qi = pl.program_id(2)
    hd = HEAD_DIM
    seq = k_ref.shape[0]

    @pl.when(qi == 0)
    def _():
        key = lax.broadcasted_iota(jnp.int32, (t, t), 0)
        qry = lax.broadcasted_iota(jnp.int32, (t, t), 1)
        shift = CHUNK.bit_length() - 1
        allowed = lax.shift_right_logical(key, shift) <= lax.shift_right_logical(qry, shift)
        for g in range(n_g):
            for d in range(2):
                band = jnp.broadcast_to(bvec_ref[g, d], (t, 2 * t))
                tile_b = pltpu.roll(band, 0, 1, stride=1, stride_axis=0)[:, :t]
                bias_ref[g, d] = jnp.where(allowed, tile_b, -jnp.inf) if d == 0 else tile_b
            for c in range(seq // t):
                blk = v_ref[c * t:(c + 1) * t, g * hd:(g + 1) * hd].astype(F32)
                vt_ref[g, c] = blk.T.astype(BF16)

    lane = lax.broadcasted_iota(jnp.int32, (t, hd), 1)
    first = lane < DIFF_QK_DIM
    for g in range(n_g):
        q = q_ref[:, g * hd:(g + 1) * hd].astype(F32)
        qq_ref[g, :t] = jnp.where(first, q, 0.0).astype(BF16)
        qq_ref[g, t:] = jnp.where(first, 0.0, q).astype(BF16)

    m_ref[...] = jnp.full_like(m_ref, -jnp.inf)
    l_ref[...] = jnp.zeros_like(l_ref)
    acc_ref[...] = jnp.zeros_like(acc_ref)

    def tiles(kis, nears):
        offs = [pl.multiple_of(ki * t, t) for ki in kis]
        qc = MXU_DIM
        depth = SCORES_AHEAD
        items = [(i, g, c) for i in range(len(kis)) for g in range(n_g) for c in range(2 * t // qc)]

        def n_keys(i, c):
            return (c * qc) % t + qc if nears[i] == 0 else t

        def scores(n, i, g, c):
            nk = n_keys(i, c)
            s_ref[n % (depth + 1), :nk] = _dot_nt(k_ref[pl.ds(offs[i], nk), g * hd:(g + 1) * hd],
                                                  qq_ref[g, c * qc:(c + 1) * qc])

        def softmax(n, i, g, c):
            cols = slice(c * qc, (c + 1) * qc)
            nk = n_keys(i, c)
            s = s_ref[n % (depth + 1), :nk]
            if nears[i] is not None:
                bc = (c * qc) % t
                s = bias_ref[g, nears[i], :nk, bc:bc + qc] + s
            mx = jnp.max(s, axis=0, keepdims=True)
            m_prev = m_ref[g, :, cols]
            if nears[i] is None:
                far = bvec_ref[g, 2][:, :qc]
                m_new = jnp.maximum(m_prev, mx + far)
                sub = m_new - far
            else:
                m_new = jnp.maximum(m_prev, mx)
                sub = m_new
            alpha = jnp.exp2(m_prev - m_new)
            p = jnp.exp2(s - sub)
            l_ref[g, :, cols] = alpha * l_ref[g, :, cols] + jnp.sum(p, axis=0, keepdims=True)
            m_ref[g, :, cols] = m_new
            return alpha, p.astype(BF16)

        def weigh(i, g, c, alpha, p):
            cols = slice(c * qc, (c + 1) * qc)
            acc_ref[g, :, cols] = alpha * acc_ref[g, :, cols] + _dot(vt_ref[g, kis[i]][:, :n_keys(i, c)], p)

        for n in range(min(depth, len(items))):
            scores(n, *items[n])
        for n, (i, g, c) in enumerate(items):
            if n + depth < len(items):
                scores(n + depth, *items[n + depth])
            weigh(i, g, c, *softmax(n, i, g, c))

    n_far = jnp.maximum(qi - 1, 0)

    def far_pair(j, carry):
        tiles([2 * j, 2 * j + 1], [None, None])
        return carry

    lax.fori_loop(0, n_far // 2, far_pair, 0)

    @pl.when(n_far % 2 == 1)
    def _():
        tiles([n_far - 1], [None])

    @pl.when(qi >= 1)
    def _():
        tiles([qi - 1, qi], [1, 0])

    @pl.when(qi == 0)
    def _():
        tiles([qi], [0])

    lv = lamv_ref[...]
    lam = (jnp.exp(jnp.sum(lv[0:1] * lv[1:2], axis=-1, keepdims=True))
           - jnp.exp(jnp.sum(lv[2:3] * lv[3:4], axis=-1, keepdims=True)) + lam_init)
    for g in range(n_g):
        o = acc_ref[g] / l_ref[g]
        out = o[:, :t] - lam * o[:, t:]
        scale = lax.rsqrt(jnp.mean(out * out, axis=0, keepdims=True) + EPS)
        y = out * scale * gain_ref[...] * (1.0 - lam_init)
        o_ref[:, g * hd:(g + 1) * hd] = y.T.astype(o_ref.dtype)


def _diff_attn(proj, bias_vecs, lam_vecs, sub_gain, *, batch, seq, n_heads, col0, t, n_g, lam_init):
    nq = seq // t
    hd = HEAD_DIM
    assert n_heads % n_g == 0 and col0 % n_g == 0
    ng_blocks = n_heads // n_g
    c0 = col0 // n_g
    w = n_g * hd
    vmem = (4 * t * w * 2 + 4 * seq * w * 2 + n_g * 2 * t * t * 4 + n_g * 2 * t * hd * 2
            + n_g * seq * hd * 2 + n_g * 2 * t * hd * 4 + n_g * 6 * 2 * t * t * 4)
    kern = functools.partial(_diff_attn_kernel, t=t, n_g=n_g, lam_init=lam_init)
    return pl.pallas_call(
        kern,
        grid=(batch, ng_blocks, nq),
        in_specs=[
            pl.BlockSpec((t, w), lambda b, h, i: (b * nq + i, c0 + h)),
            pl.BlockSpec((seq, w), lambda b, h, i: (b, c0 + ng_blocks + h)),
            pl.BlockSpec((seq, w), lambda b, h, i: (b, c0 + 2 * ng_blocks + h)),
            pl.BlockSpec((n_g, 3, 1, 2 * t), lambda b, h, i: (h, 0, 0, 0)),
            pl.BlockSpec((4, DIFF_QK_DIM), lambda b, h, i: (0, 0)),
            pl.BlockSpec((hd, 1), lambda b, h, i: (0, 0)),
        ],
        out_specs=pl.BlockSpec((t, w), lambda b, h, i: (b * nq + i, h)),
        out_shape=jax.ShapeDtypeStruct((batch * seq, n_heads * hd), BF16),
        scratch_shapes=[
            pltpu.VMEM((n_g, 2, t, t), F32),
            pltpu.VMEM((n_g, seq // t, hd, t), BF16),
            pltpu.VMEM((n_g, 2 * t, hd), BF16),
            pltpu.VMEM((SCORES_AHEAD + 1, t, MXU_DIM), F32),
            pltpu.VMEM((n_g, 1, 2 * t), F32),
            pltpu.VMEM((n_g, 1, 2 * t), F32),
            pltpu.VMEM((n_g, hd, 2 * t), F32),
        ],
        compiler_params=_params(3, vmem),
        name="diff_attn",
    )(proj, proj, proj, bias_vecs, lam_vecs, sub_gain.reshape(hd, 1))


def _sb_attn_kernel(q_ref, k_ref, v_ref, tri_ref, gain_ref, o_ref, carry_ref, acc_ref, *, t, cw, n_g):
    qi = pl.program_id(2)
    hd = HEAD_DIM
    head_cols = [slice(g * hd, (g + 1) * hd) for g in range(n_g)]

    def block(row0, nr, key_off, nk, diag):
        rows = slice(row0, row0 + nr)
        tri = tri_ref[...]
        if diag:
            strict = (lax.broadcasted_iota(jnp.int32, (nr, nk), 1)
                      < lax.broadcasted_iota(jnp.int32, (nr, nk), 0) + row0)

        def scores(g):
            return _dot_nt(q_ref[rows, head_cols[g]], k_ref[pl.ds(key_off, nk), head_cols[g]])

        def logits(g, z):
            neg_abs = lax.bitcast_convert_type(
                lax.bitcast_convert_type(z, jnp.uint32) | jnp.uint32(F32_SIGN_BIT), F32)
            neg_soft = jnp.log(1.0 + jnp.exp2(neg_abs)) * (-LOG2E)
            log_keep = neg_soft - jnp.maximum(z, 0.0)
            log_beta = log_keep + z
            if diag:
                log_keep = jnp.where(strict, log_keep, 0.0)
            run = None if diag else carry_ref[g, rows]
            after = [None] * (nk // cw)
            for c in reversed(range(nk // cw)):
                lk = log_keep[:, c * cw:(c + 1) * cw]
                cs = _dot(lk.astype(BF16), tri)
                after[c] = cs if run is None else _widen(run, cw) + cs
                row_sum = jnp.sum(lk, axis=-1, keepdims=True)
                run = jnp.broadcast_to(row_sum, (nr, LANES)) if run is None else run + row_sum
            carry_ref[g, rows] = run
            return log_beta, jnp.concatenate(after, axis=1)

        def mix(g, log_beta, after):
            w = jnp.exp2(log_beta + after)
            if diag:
                w = jnp.where(strict, w, 0.0)
            pv = _dot(w.astype(BF16), v_ref[pl.ds(key_off, nk), head_cols[g]])
            if diag:
                acc_ref[g, rows] = pv
            else:
                acc_ref[g, rows] += pv

        z_next = scores(0)
        for g in range(n_g):
            z = z_next
            if g + 1 < n_g:
                z_next = scores(g + 1)
            mix(g, *logits(g, z))

    tile_off = pl.multiple_of(qi * t, t)
    block(0, t, tile_off, t, True)

    for row0 in range(0, t, t // 2):
        rows = slice(row0, row0 + t // 2)

        def underflowed():
            largest = jnp.max(carry_ref[0, rows])
            for g in range(1, n_g):
                largest = jnp.maximum(largest, jnp.max(carry_ref[g, rows]))
            return largest < F32_UNDERFLOW_LOG2

        def not_done(state):
            j, done = state
            return jnp.logical_and(j < qi * (t // cw), jnp.logical_not(done))

        def body(state):
            j, _ = state
            block(row0, t // 2, pl.multiple_of(tile_off - (j + 1) * cw, cw), cw, False)
            return j + 1, underflowed()

        lax.while_loop(not_done, body, (jnp.int32(0), underflowed()))

    for g in range(n_g):
        out = acc_ref[g]
        o_ref[:, g * hd:(g + 1) * hd] = (out * _rms_scale(out) * gain_ref[...]).astype(o_ref.dtype)


def _sb_attn(proj, out_gain, *, batch, seq, n_heads, col0, t, n_g):
    nq = seq // t
    hd = HEAD_DIM
    assert n_heads % n_g == 0 and col0 % n_g == 0
    ng_blocks = n_heads // n_g
    c0 = col0 // n_g
    wd = n_g * hd
    cw = min(t, MXU_DIM)
    tri = jnp.asarray(np.tril(np.ones((cw, cw), np.float32), -1), BF16)
    vmem = (4 * t * wd * 2 + 4 * seq * wd * 2 + 2 * cw * cw * 2
            + n_g * (t * hd * 4 + t * LANES * 4) + n_g * 8 * t * t * 4)
    kern = functools.partial(_sb_attn_kernel, t=t, cw=cw, n_g=n_g)
    return pl.pallas_call(
        kern,
        grid=(batch, ng_blocks, nq),
        in_specs=[
            pl.BlockSpec((t, wd), lambda b, h, i: (b * nq + i, c0 + h)),
            pl.BlockSpec((seq, wd), lambda b, h, i: (b, c0 + ng_blocks + h)),
            pl.BlockSpec((seq, wd), lambda b, h, i: (b, c0 + 2 * ng_blocks + h)),
            pl.BlockSpec((cw, cw), lambda b, h, i: (0, 0)),
            pl.BlockSpec((1, hd), lambda b, h, i: (0, 0)),
        ],
        out_specs=pl.BlockSpec((t, wd), lambda b, h, i: (b * nq + i, h)),
        out_shape=jax.ShapeDtypeStruct((batch * seq, n_heads * hd), BF16),
        scratch_shapes=[
            pltpu.VMEM((n_g, t, LANES), F32),
            pltpu.VMEM((n_g, t, hd), F32),
        ],
        compiler_params=_params(3, vmem),
        name="sb_attn",
    )(proj, proj, proj, tri, out_gain.reshape(1, hd))


def _mem_attn_kernel(q_ref, k_ref, v_ref, o_ref, *, n_heads, dh):
    scale = dh ** -0.5
    head_cols = [slice(h * dh, (h + 1) * dh) for h in range(n_heads)]

    def scores(h):
        return _dot_nt(q_ref[:, head_cols[h]], k_ref[:, head_cols[h]])

    s_next = scores(0)
    for h in range(n_heads):
        s = s_next * scale
        if h + 1 < n_heads:
            s_next = scores(h + 1)
        p = jnp.exp(s - jnp.max(s, axis=-1, keepdims=True))
        denom = jnp.sum(p, axis=-1, keepdims=True)
        o = _dot(p.astype(BF16), v_ref[:, head_cols[h]]) / denom
        o_ref[:, head_cols[h]] = o.astype(o_ref.dtype)


def _mem_attn(q, kv, *, batch, seq, n_mem, tq):
    d = q.shape[1]
    nq = seq // tq
    vmem = 4 * tq * d * 2 + 4 * n_mem * d * 2 + 8 * tq * n_mem * 4 + 2 * tq * d * 4
    kern = functools.partial(_mem_attn_kernel, n_heads=N_HEADS_MEM, dh=d // N_HEADS_MEM)
    return pl.pallas_call(
        kern,
        grid=(batch, nq),
        in_specs=[
            pl.BlockSpec((tq, d), lambda b, i: (b * nq + i, 0)),
            pl.BlockSpec((n_mem, d), lambda b, i: (b, 0)),
            pl.BlockSpec((n_mem, d), lambda b, i: (b, 1)),
        ],
        out_specs=pl.BlockSpec((tq, d), lambda b, i: (b * nq + i, 0)),
        out_shape=jax.ShapeDtypeStruct((batch * seq, d), BF16),
        compiler_params=_params(2, vmem),
        name="mem_attn",
    )(q, kv, kv)


def _lambda_init(layer_idx):
    return 0.8 - 0.6 * math.exp(-0.3 * layer_idx)


def _pick(n, prefs):
    for p in prefs:
        if n % p == 0:
            return p
    return n


def kernel(x, mem, w_in, w_out, rel_bias, lambda_q1, lambda_k1, lambda_q2, lambda_k2, diff_sub_gain, sb_gain, g_mix_pre, g_mix_post, w_mq, w_mkv, w_mo, g_mem_kv, g_mem_pre, g_mem_post, w_gate_up, w_down, g_ffn_pre, g_ffn_post):
    batch, seq, d_model = x.shape
    n_mem = mem.shape[1]
    depth = w_in.shape[0]
    width = w_out.shape[1] // 2
    n_heads = width // HEAD_DIM
    d_ff = w_down.shape[1]
    t = _pick(seq, (512, 256))
    assert w_in.shape[2] == 6 * width and t % CHUNK == 0
    assert _far_bias_is_constant(t, seq)
    n_g = _pick(n_heads, (HEADS_PER_STEP,))

    rows = batch * seq
    tm = _pick(rows, (512, 256, 128))
    tm_n = _pick(rows, (1024, 512, 256, 128))
    xf = x.reshape(rows, d_model)
    memf = mem.reshape(batch * n_mem, d_model)
    ones = lambda n: jnp.ones((n,), F32)

    col_scale = jnp.concatenate([
        jnp.full((width,), DIFF_QK_DIM ** -0.5 * LOG2E, F32), ones(2 * width),
        jnp.full((width,), HEAD_DIM ** -0.5 * LOG2E, F32), ones(2 * width)])

    for l in range(depth):
        lam_init = _lambda_init(l)
        proj = _norm_matmul(xf, g_mix_pre[l], w_in[l].astype(BF16), col_scale,
                            tm=tm_n, tn=_pick(6 * width, (1536, 1024, 512, 256, 128)))
        lam_vecs = jnp.stack([lambda_q1[l], lambda_k1[l], lambda_q2[l], lambda_k2[l]]).astype(F32)
        out_a = _diff_attn(proj, _diff_bias_vectors(rel_bias, t), lam_vecs, diff_sub_gain[l],
                           batch=batch, seq=seq, n_heads=n_heads, col0=0, t=t, n_g=n_g, lam_init=lam_init)
        out_b = _sb_attn(proj, sb_gain[l], batch=batch, seq=seq, n_heads=n_heads,
                         col0=3 * n_heads, t=t, n_g=n_g)
        xf = _matmul_norm_resid([out_a, out_b], w_out[l].astype(BF16), xf, g_mix_post[l], tm=tm)
        q_mem = _norm_matmul(xf, g_mem_pre[l], w_mq[l].astype(BF16), ones(d_model),
                             tm=tm_n, tn=_pick(d_model, (2048, 1024, 512, 256, 128)))
        kv = _norm_matmul(memf, g_mem_kv[l], w_mkv[l].astype(BF16), ones(2 * d_model),
                          tm=_pick(batch * n_mem, (512, 256, 128)),
                          tn=_pick(2 * d_model, (2048, 1024, 512, 256, 128)))
        o_mem = _mem_attn(q_mem, kv, batch=batch, seq=seq, n_mem=n_mem,
                          tq=_pick(seq, (512, 256, 128)))
        xf = _matmul_norm_resid([o_mem], w_mo[l].astype(BF16), xf, g_mem_post[l], tm=tm)
        act = _norm_swiglu(xf, g_ffn_pre[l], w_gate_up[l].astype(BF16),
                           tm=tm_n, tn=_pick(d_ff, (512, 256, 128)))
        xf = _matmul_norm_resid([act], w_down[l].astype(BF16), xf, g_ffn_post[l], tm=tm)
    return xf.reshape(batch, seq, d_model)
```

```python
import functools
import math

import numpy as np
import jax
import jax.numpy as jnp
from jax import lax
from jax.experimental import pallas as pl
from jax.experimental.pallas import tpu as pltpu

CHUNK = 64
HEAD_DIM = 128
DIFF_QK_DIM = HEAD_DIM // 2
N_HEADS_MEM = 4
N_BUCKETS = 32
MAX_DISTANCE = 128
EPS = 1e-6
LOG2E = math.log2(math.e)
F32_UNDERFLOW_LOG2 = -160.0
F32_SIGN_BIT = 0x80000000

V7X_VMEM_BYTES = 64 * 1024 * 1024
LANES = 128
MXU_DIM = 256
NORM_ROW_CHUNK = 256
HEADS_PER_STEP = 4
SCORES_AHEAD = 3
VMEM_CAP_BYTES = V7X_VMEM_BYTES - 8 * 1024 * 1024

F32 = jnp.float32
BF16 = jnp.bfloat16


def _vmem_limit(estimate_bytes):
    return int(min(VMEM_CAP_BYTES, max(estimate_bytes, 16 * 1024 * 1024)))


def _params(n_axes, vmem_bytes):
    return pltpu.CompilerParams(
        dimension_semantics=("arbitrary",) * n_axes,
        vmem_limit_bytes=_vmem_limit(vmem_bytes))


def _rms_scale(v):
    return lax.rsqrt(jnp.mean(v * v, axis=-1, keepdims=True) + EPS)


def _dot(a, b):
    return jnp.dot(a, b, preferred_element_type=F32)


def _dot_nt(a, b):
    return lax.dot_general(a, b, (((1,), (1,)), ((), ())), preferred_element_type=F32)


def _norm_matmul_kernel(x_ref, g_ref, w_ref, cs_ref, o_ref, h_ref):
    first = pl.program_id(1) == 0

    @pl.when(first)
    def _():
        tm = x_ref.shape[0]
        rc = min(NORM_ROW_CHUNK, tm)
        for r in range(tm // rc):
            rows = slice(r * rc, (r + 1) * rc)
            x = x_ref[rows]
            h = (x * _rms_scale(x) * g_ref[...]).astype(BF16)
            h_ref[rows] = h
            o_ref[rows] = (_dot(h, w_ref[...]) * cs_ref[...]).astype(o_ref.dtype)

    @pl.when(jnp.logical_not(first))
    def _():
        acc = _dot(h_ref[...], w_ref[...])
        o_ref[...] = (acc * cs_ref[...]).astype(o_ref.dtype)


def _norm_matmul(x, gain, w, col_scale, *, tm, tn):
    m, d = x.shape
    n = w.shape[1]
    assert m % tm == 0 and n % tn == 0
    vmem = 2 * tm * d * 4 + tm * d * 2 + 2 * d * tn * 2 + 2 * tm * tn * 2 + 2 * tm * tn * 4
    return pl.pallas_call(
        _norm_matmul_kernel,
        grid=(m // tm, n // tn),
        in_specs=[
            pl.BlockSpec((tm, d), lambda i, j: (i, 0)),
            pl.BlockSpec((1, d), lambda i, j: (0, 0)),
            pl.BlockSpec((d, tn), lambda i, j: (0, j)),
            pl.BlockSpec((1, tn), lambda i, j: (0, j)),
        ],
        out_specs=pl.BlockSpec((tm, tn), lambda i, j: (i, j)),
        out_shape=jax.ShapeDtypeStruct((m, n), BF16),
        scratch_shapes=[pltpu.VMEM((tm, d), BF16)],
        compiler_params=_params(2, vmem),
        name="norm_matmul",
    )(x, gain.reshape(1, d), w, col_scale.reshape(1, n))


def _swiglu(gate, up):
    return gate * (1.0 / (1.0 + jnp.exp(-gate))) * up


def _norm_swiglu_kernel(x_ref, g_ref, wg_ref, wu_ref, o_ref, h_ref):
    first = pl.program_id(1) == 0

    @pl.when(first)
    def _():
        tm = x_ref.shape[0]
        rc = min(NORM_ROW_CHUNK, tm)
        for r in range(tm // rc):
            rows = slice(r * rc, (r + 1) * rc)
            x = x_ref[rows]
            h = (x * _rms_scale(x) * g_ref[...]).astype(BF16)
            h_ref[rows] = h
            o_ref[rows] = _swiglu(_dot(h, wg_ref[...]), _dot(h, wu_ref[...])).astype(o_ref.dtype)

    @pl.when(jnp.logical_not(first))
    def _():
        h = h_ref[...]
        o_ref[...] = _swiglu(_dot(h, wg_ref[...]), _dot(h, wu_ref[...])).astype(o_ref.dtype)


def _norm_swiglu(x, gain, w_gate_up, *, tm, tn):
    m, d = x.shape
    f = w_gate_up.shape[1] // 2
    assert m % tm == 0 and f % tn == 0
    nf = f // tn
    vmem = 2 * tm * d * 4 + tm * d * 2 + 4 * d * tn * 2 + 2 * tm * tn * 2 + 4 * tm * tn * 4
    return pl.pallas_call(
        _norm_swiglu_kernel,
        grid=(m // tm, nf),
        in_specs=[
            pl.BlockSpec((tm, d), lambda i, j: (i, 0)),
            pl.BlockSpec((1, d), lambda i, j: (0, 0)),
            pl.BlockSpec((d, tn), lambda i, j: (0, j)),
            pl.BlockSpec((d, tn), lambda i, j: (0, j + nf)),
        ],
        out_specs=pl.BlockSpec((tm, tn), lambda i, j: (i, j)),
        out_shape=jax.ShapeDtypeStruct((m, f), BF16),
        scratch_shapes=[pltpu.VMEM((tm, d), BF16)],
        compiler_params=_params(2, vmem),
        name="norm_swiglu",
    )(x, gain.reshape(1, d), w_gate_up, w_gate_up)


def _matmul_norm_resid_kernel(*refs, n_parts):
    lhs_refs, (w_ref, r_ref, g_ref, o_ref) = refs[:n_parts], refs[n_parts:]
    y, row = None, 0
    for l_ref in lhs_refs:
        part = _dot(l_ref[...], w_ref[row:row + l_ref.shape[1], :])
        y = part if y is None else y + part
        row += l_ref.shape[1]
    o_ref[...] = r_ref[...] + y * _rms_scale(y) * g_ref[...]


def _matmul_norm_resid(lhs_parts, w, resid, gain, *, tm):
    m = lhs_parts[0].shape[0]
    kdim, d = w.shape
    assert m % tm == 0 and sum(p.shape[1] for p in lhs_parts) == kdim
    vmem = 2 * tm * kdim * 2 + kdim * d * 2 + 4 * tm * d * 4 + 2 * tm * d * 4
    return pl.pallas_call(
        functools.partial(_matmul_norm_resid_kernel, n_parts=len(lhs_parts)),
        grid=(m // tm,),
        in_specs=[pl.BlockSpec((tm, p.shape[1]), lambda i: (i, 0)) for p in lhs_parts] + [
            pl.BlockSpec((kdim, d), lambda i: (0, 0), pipeline_mode=pl.Buffered(1)),
            pl.BlockSpec((tm, d), lambda i: (i, 0)),
            pl.BlockSpec((1, d), lambda i: (0, 0)),
        ],
        out_specs=pl.BlockSpec((tm, d), lambda i: (i, 0)),
        out_shape=jax.ShapeDtypeStruct((m, d), F32),
        compiler_params=_params(1, vmem),
        name="matmul_norm_resid",
    )(*lhs_parts, w, resid, gain.reshape(1, d))


def _t5_bucket(rel):
    nb = N_BUCKETS // 2
    ret = jnp.where(rel > 0, nb, 0)
    n = jnp.abs(rel)
    max_exact = nb // 2
    nf = jnp.maximum(n, 1).astype(F32)
    large = max_exact + (jnp.log(nf / max_exact) / math.log(MAX_DISTANCE / max_exact)
                         * (nb - max_exact)).astype(jnp.int32)
    large = jnp.minimum(large, nb - 1)
    return ret + jnp.where(n < max_exact, n, large)


def _far_bias_is_constant(t, seq):
    n = np.arange(t + 1, max(seq, t + 2), dtype=np.float64)
    nb, me = N_BUCKETS // 2, N_BUCKETS // 4
    b = np.minimum(me + (np.log(n / me) / math.log(MAX_DISTANCE / me) * (nb - me)).astype(np.int64), nb - 1)
    return bool(np.all(b == nb - 1)) and (t + 1) >= 2 * MAX_DISTANCE


def _diff_bias_vectors(rel_bias, t):
    j = jnp.arange(2 * t, dtype=jnp.int32)
    rel0 = jnp.where(j < t, -j, 2 * t - j)
    vecs = [rel_bias[_t5_bucket(rel0 - d * t)].astype(F32) for d in range(3)]
    out = jnp.transpose(jnp.stack(vecs, axis=0), (2, 0, 1)) * LOG2E
    return out[:, :, None, :]


def _widen(col, width):
    return jnp.concatenate([col] * (width // LANES), axis=1)


def _diff_attn_kernel(q_ref, k_ref, v_ref, bvec_ref, lamv_ref, gain_ref, o_ref,
                      bias_ref, vt_ref, qq_ref, s_ref, m_ref, l_ref, acc_ref, *, t, n_g, lam_init):
    qi = pl.program_id(2)
    hd = HEAD_DIM
    seq = k_ref.shape[0]

    @pl.when(qi == 0)
    def _():
        key = lax.broadcasted_iota(jnp.int32, (t, t), 0)
        qry = lax.broadcasted_iota(jnp.int32, (t, t), 1)
        shift = CHUNK.bit_length() - 1
        allowed = lax.shift_right_logical(key, shift) <= lax.shift_right_logical(qry, shift)
        for g in range(n_g):
            for d in range(2):
                band = jnp.broadcast_to(bvec_ref[g, d], (t, 2 * t))
                tile_b = pltpu.roll(band, 0, 1, stride=1, stride_axis=0)[:, :t]
                bias_ref[g, d] = jnp.where(allowed, tile_b, -jnp.inf) if d == 0 else tile_b
            for c in range(seq // t):
                blk = v_ref[c * t:(c + 1) * t, g * hd:(g + 1) * hd].astype(F32)
                vt_ref[g, c] = blk.T.astype(BF16)

    lane = lax.broadcasted_iota(jnp.int32, (t, hd), 1)
    first = lane < DIFF_QK_DIM
    for g in range(n_g):
        q = q_ref[:, g * hd:(g + 1) * hd].astype(F32)
        qq_ref[g, :t] = jnp.where(first, q, 0.0).astype(BF16)
        qq_ref[g, t:] = jnp.where(first, 0.0, q).astype(BF16)

    m_ref[...] = jnp.full_like(m_ref, -jnp.inf)
    l_ref[...] = jnp.zeros_like(l_ref)
    acc_ref[...] = jnp.zeros_like(acc_ref)

    def tiles(kis, nears):
        offs = [pl.multiple_of(ki * t, t) for ki in kis]
        qc = MXU_DIM
        depth = SCORES_AHEAD
        items = [(i, g, c) for i in range(len(kis)) for g in range(n_g) for c in range(2 * t // qc)]

        def n_keys(i, c):
            return (c * qc) % t + qc if nears[i] == 0 else t

        def scores(n, i, g, c):
            nk = n_keys(i, c)
            s_ref[n % (depth + 1), :nk] = _dot_nt(k_ref[pl.ds(offs[i], nk), g * hd:(g + 1) * hd],
                                                  qq_ref[g, c * qc:(c + 1) * qc])

        def softmax(n, i, g, c):
            cols = slice(c * qc, (c + 1) * qc)
            nk = n_keys(i, c)
            s = s_ref[n % (depth + 1), :nk]
            if nears[i] is not None:
                bc = (c * qc) % t
                s = bias_ref[g, nears[i], :nk, bc:bc + qc] + s
            mx = jnp.max(s, axis=0, keepdims=True)
            m_prev = m_ref[g, :, cols]
            if nears[i] is None:
                far = bvec_ref[g, 2][:, :qc]
                m_new = jnp.maximum(m_prev, mx + far)
                sub = m_new - far
            else:
                m_new = jnp.maximum(m_prev, mx)
                sub = m_new
            alpha = jnp.exp2(m_prev - m_new)
            p = jnp.exp2(s - sub)
            l_ref[g, :, cols] = alpha * l_ref[g, :, cols] + jnp.sum(p, axis=0, keepdims=True)
            m_ref[g, :, cols] = m_new
            return alpha, p.astype(BF16)

        def weigh(i, g, c, alpha, p):
            cols = slice(c * qc, (c + 1) * qc)
            acc_ref[g, :, cols] = alpha * acc_ref[g, :, cols] + _dot(vt_ref[g, kis[i]][:, :n_keys(i, c)], p)

        for n in range(min(depth, len(items))):
            scores(n, *items[n])
        for n, (i, g, c) in enumerate(items):
            if n + depth < len(items):
                scores(n + depth, *items[n + depth])
            weigh(i, g, c, *softmax(n, i, g, c))

    n_far = jnp.maximum(qi - 1, 0)

    def far_pair(j, carry):
        tiles([2 * j, 2 * j + 1], [None, None])
        return carry

    lax.fori_loop(0, n_far // 2, far_pair, 0)

    @pl.when(n_far % 2 == 1)
    def _():
        tiles([n_far - 1], [None])

    @pl.when(qi >= 1)
    def _():
        tiles([qi - 1, qi], [1, 0])

    @pl.when(qi == 0)
    def _():
        tiles([qi], [0])

    lv = lamv_ref[...]
    lam = (jnp.exp(jnp.sum(lv[0:1] * lv[1:2], axis=-1, keepdims=True))
           - jnp.exp(jnp.sum(lv[2:3] * lv[3:4], axis=-1, keepdims=True)) + lam_init)
    for g in range(n_g):
        o = acc_ref[g] / l_ref[g]
        out = o[:, :t] - lam * o[:, t:]
        scale = lax.rsqrt(jnp.mean(out * out, axis=0, keepdims=True) + EPS)
        y = out * scale * gain_ref[...] * (1.0 - lam_init)
        o_ref[:, g * hd:(g + 1) * hd] = y.T.astype(o_ref.dtype)


def _diff_attn(proj, bias_vecs, lam_vecs, sub_gain, *, batch, seq, n_heads, col0, t, n_g, lam_init):
    nq = seq // t
    hd = HEAD_DIM
    assert n_heads % n_g == 0 and col0 % n_g == 0
    ng_blocks = n_heads // n_g
    c0 = col0 // n_g
    w = n_g * hd
    vmem = (4 * t * w * 2 + 4 * seq * w * 2 + n_g * 2 * t * t * 4 + n_g * 2 * t * hd * 2
            + n_g * seq * hd * 2 + n_g * 2 * t * hd * 4 + n_g * 6 * 2 * t * t * 4)
    kern = functools.partial(_diff_attn_kernel, t=t, n_g=n_g, lam_init=lam_init)
    return pl.pallas_call(
        kern,
        grid=(batch, ng_blocks, nq),
        in_specs=[
            pl.BlockSpec((t, w), lambda b, h, i: (b * nq + i, c0 + h)),
            pl.BlockSpec((seq, w), lambda b, h, i: (b, c0 + ng_blocks + h)),
            pl.BlockSpec((seq, w), lambda b, h, i: (b, c0 + 2 * ng_blocks + h)),
            pl.BlockSpec((n_g, 3, 1, 2 * t), lambda b, h, i: (h, 0, 0, 0)),
            pl.BlockSpec((4, DIFF_QK_DIM), lambda b, h, i: (0, 0)),
            pl.BlockSpec((hd, 1), lambda b, h, i: (0, 0)),
        ],
        out_specs=pl.BlockSpec((t, w), lambda b, h, i: (b * nq + i, h)),
        out_shape=jax.ShapeDtypeStruct((batch * seq, n_heads * hd), BF16),
        scratch_shapes=[
            pltpu.VMEM((n_g, 2, t, t), F32),
            pltpu.VMEM((n_g, seq // t, hd, t), BF16),
            pltpu.VMEM((n_g, 2 * t, hd), BF16),
            pltpu.VMEM((SCORES_AHEAD + 1, t, MXU_DIM), F32),
            pltpu.VMEM((n_g, 1, 2 * t), F32),
            pltpu.VMEM((n_g, 1, 2 * t), F32),
            pltpu.VMEM((n_g, hd, 2 * t), F32),
        ],
        compiler_params=_params(3, vmem),
        name="diff_attn",
    )(proj, proj, proj, bias_vecs, lam_vecs, sub_gain.reshape(hd, 1))


def _sb_attn_kernel(q_ref, k_ref, v_ref, tri_ref, gain_ref, o_ref, carry_ref, acc_ref, *, t, cw, n_g):
    qi = pl.program_id(2)
    hd = HEAD_DIM
    head_cols = [slice(g * hd, (g + 1) * hd) for g in range(n_g)]

    def block(row0, nr, key_off, nk, diag):
        rows = slice(row0, row0 + nr)
        tri = tri_ref[...]
        if diag:
            strict = (lax.broadcasted_iota(jnp.int32, (nr, nk), 1)
                      < lax.broadcasted_iota(jnp.int32, (nr, nk), 0) + row0)

        def scores(g):
            return _dot_nt(q_ref[rows, head_cols[g]], k_ref[pl.ds(key_off, nk), head_cols[g]])

        def logits(g, z):
            neg_abs = lax.bitcast_convert_type(
                lax.bitcast_convert_type(z, jnp.uint32) | jnp.uint32(F32_SIGN_BIT), F32)
            neg_soft = jnp.log(1.0 + jnp.exp2(neg_abs)) * (-LOG2E)
            log_keep = neg_soft - jnp.maximum(z, 0.0)
            log_beta = log_keep + z
            if diag:
                log_keep = jnp.where(strict, log_keep, 0.0)
            run = None if diag else carry_ref[g, rows]
            after = [None] * (nk // cw)
            for c in reversed(range(nk // cw)):
                lk = log_keep[:, c * cw:(c + 1) * cw]
                cs = _dot(lk.astype(BF16), tri)
                after[c] = cs if run is None else _widen(run, cw) + cs
                row_sum = jnp.sum(lk, axis=-1, keepdims=True)
                run = jnp.broadcast_to(row_sum, (nr, LANES)) if run is None else run + row_sum
            carry_ref[g, rows] = run
            return log_beta, jnp.concatenate(after, axis=1)

        def mix(g, log_beta, after):
            w = jnp.exp2(log_beta + after)
            if diag:
                w = jnp.where(strict, w, 0.0)
            pv = _dot(w.astype(BF16), v_ref[pl.ds(key_off, nk), head_cols[g]])
            if diag:
                acc_ref[g, rows] = pv
            else:
                acc_ref[g, rows] += pv

        z_next = scores(0)
        for g in range(n_g):
            z = z_next
            if g + 1 < n_g:
                z_next = scores(g + 1)
            mix(g, *logits(g, z))

    tile_off = pl.multiple_of(qi * t, t)
    block(0, t, tile_off, t, True)

    for row0 in range(0, t, t // 2):
        rows = slice(row0, row0 + t // 2)

        def underflowed():
            largest = jnp.max(carry_ref[0, rows])
            for g in range(1, n_g):
                largest = jnp.maximum(largest, jnp.max(carry_ref[g, rows]))
            return largest < F32_UNDERFLOW_LOG2

        def not_done(state):
            j, done = state
            return jnp.logical_and(j < qi * (t // cw), jnp.logical_not(done))

        def body(state):
            j, _ = state
            block(row0, t // 2, pl.multiple_of(tile_off - (j + 1) * cw, cw), cw, False)
            return j + 1, underflowed()

        lax.while_loop(not_done, body, (jnp.int32(0), underflowed()))

    for g in range(n_g):
        out = acc_ref[g]
        o_ref[:, g * hd:(g + 1) * hd] = (out * _rms_scale(out) * gain_ref[...]).astype(o_ref.dtype)


def _sb_attn(proj, out_gain, *, batch, seq, n_heads, col0, t, n_g):
    nq = seq // t
    hd = HEAD_DIM
    assert n_heads % n_g == 0 and col0 % n_g == 0
    ng_blocks = n_heads // n_g
    c0 = col0 // n_g
    wd = n_g * hd
    cw = min(t, MXU_DIM)
    tri = jnp.asarray(np.tril(np.ones((cw, cw), np.float32), -1), BF16)
    vmem = (4 * t * wd * 2 + 4 * seq * wd * 2 + 2 * cw * cw * 2
            + n_g * (t * hd * 4 + t * LANES * 4) + n_g * 8 * t * t * 4)
    kern = functools.partial(_sb_attn_kernel, t=t, cw=cw, n_g=n_g)
    return pl.pallas_call(
        kern,
        grid=(batch, ng_blocks, nq),
        in_specs=[
            pl.BlockSpec((t, wd), lambda b, h, i: (b * nq + i, c0 + h)),
            pl.BlockSpec((seq, wd), lambda b, h, i: (b, c0 + ng_blocks + h)),
            pl.BlockSpec((seq, wd), lambda b, h, i: (b, c0 + 2 * ng_blocks + h)),
            pl.BlockSpec((cw, cw), lambda b, h, i: (0, 0)),
            pl.BlockSpec((1, hd), lambda b, h, i: (0, 0)),
        ],
        out_specs=pl.BlockSpec((t, wd), lambda b, h, i: (b * nq + i, h)),
        out_shape=jax.ShapeDtypeStruct((batch * seq, n_heads * hd), BF16),
        scratch_shapes=[
            pltpu.VMEM((n_g, t, LANES), F32),
            pltpu.VMEM((n_g, t, hd), F32),
        ],
        compiler_params=_params(3, vmem),
        name="sb_attn",
    )(proj, proj, proj, tri, out_gain.reshape(1, hd))


def _mem_attn_out_kernel(q_ref, k_ref, v_ref, w_ref, r_ref, g_ref, o_ref, *, n_heads, dh):
    scale = dh ** -0.5
    head_cols = [slice(h * dh, (h + 1) * dh) for h in range(n_heads)]

    def scores(h):
        return _dot_nt(q_ref[:, head_cols[h]], k_ref[:, head_cols[h]])

    y = None
    s_next = scores(0)
    for h in range(n_heads):
        s = s_next * scale
        if h + 1 < n_heads:
            s_next = scores(h + 1)
        p = jnp.exp(s - jnp.max(s, axis=-1, keepdims=True))
        denom = jnp.sum(p, axis=-1, keepdims=True)
        o = (_dot(p.astype(BF16), v_ref[:, head_cols[h]]) / denom).astype(BF16)
        part = _dot(o, w_ref[head_cols[h], :])
        y = part if y is None else y + part
    o_ref[...] = r_ref[...] + y * _rms_scale(y) * g_ref[...]


def _mem_attn_out(q, kv, w, resid, gain, *, batch, seq, n_mem, tq):
    d = q.shape[1]
    nq = seq // tq
    vmem = 4 * tq * d * 2 + 4 * n_mem * d * 2 + d * d * 2 + 4 * tq * d * 4 + 8 * tq * n_mem * 4 + 2 * tq * d * 4
    kern = functools.partial(_mem_attn_out_kernel, n_heads=N_HEADS_MEM, dh=d // N_HEADS_MEM)
    return pl.pallas_call(
        kern,
        grid=(batch, nq),
        in_specs=[
            pl.BlockSpec((tq, d), lambda b, i: (b * nq + i, 0)),
            pl.BlockSpec((n_mem, d), lambda b, i: (b, 0)),
            pl.BlockSpec((n_mem, d), lambda b, i: (b, 1)),
            pl.BlockSpec((d, d), lambda b, i: (0, 0), pipeline_mode=pl.Buffered(1)),
            pl.BlockSpec((tq, d), lambda b, i: (b * nq + i, 0)),
            pl.BlockSpec((1, d), lambda b, i: (0, 0)),
        ],
        out_specs=pl.BlockSpec((tq, d), lambda b, i: (b * nq + i, 0)),
        out_shape=jax.ShapeDtypeStruct((batch * seq, d), F32),
        compiler_params=_params(2, vmem),
        name="mem_attn_out",
    )(q, kv, kv, w, resid, gain.reshape(1, d))


def _lambda_init(layer_idx):
    return 0.8 - 0.6 * math.exp(-0.3 * layer_idx)


def _pick(n, prefs):
    for p in prefs:
        if n % p == 0:
            return p
    return n


def kernel(x, mem, w_in, w_out, rel_bias, lambda_q1, lambda_k1, lambda_q2, lambda_k2, diff_sub_gain, sb_gain, g_mix_pre, g_mix_post, w_mq, w_mkv, w_mo, g_mem_kv, g_mem_pre, g_mem_post, w_gate_up, w_down, g_ffn_pre, g_ffn_post):
    batch, seq, d_model = x.shape
    n_mem = mem.shape[1]
    depth = w_in.shape[0]
    width = w_out.shape[1] // 2
    n_heads = width // HEAD_DIM
    d_ff = w_down.shape[1]
    t = _pick(seq, (512, 256))
    assert w_in.shape[2] == 6 * width and t % CHUNK == 0
    assert _far_bias_is_constant(t, seq)
    n_g = _pick(n_heads, (HEADS_PER_STEP,))

    rows = batch * seq
    tm = _pick(rows, (512, 256, 128))
    tm_n = _pick(rows, (1024, 512, 256, 128))
    xf = x.reshape(rows, d_model)
    memf = mem.reshape(batch * n_mem, d_model)
    ones = lambda n: jnp.ones((n,), F32)

    col_scale = jnp.concatenate([
        jnp.full((width,), DIFF_QK_DIM ** -0.5 * LOG2E, F32), ones(2 * width),
        jnp.full((width,), HEAD_DIM ** -0.5 * LOG2E, F32), ones(2 * width)])

    for l in range(depth):
        lam_init = _lambda_init(l)
        proj = _norm_matmul(xf, g_mix_pre[l], w_in[l].astype(BF16), col_scale,
                            tm=tm_n, tn=_pick(6 * width, (1536, 1024, 512, 256, 128)))
        lam_vecs = jnp.stack([lambda_q1[l], lambda_k1[l], lambda_q2[l], lambda_k2[l]]).astype(F32)
        out_a = _diff_attn(proj, _diff_bias_vectors(rel_bias, t), lam_vecs, diff_sub_gain[l],
                           batch=batch, seq=seq, n_heads=n_heads, col0=0, t=t, n_g=n_g, lam_init=lam_init)
        out_b = _sb_attn(proj, sb_gain[l], batch=batch, seq=seq, n_heads=n_heads,
                         col0=3 * n_heads, t=t, n_g=n_g)
        xf = _matmul_norm_resid([out_a, out_b], w_out[l].astype(BF16), xf, g_mix_post[l], tm=tm)
        q_mem = _norm_matmul(xf, g_mem_pre[l], w_mq[l].astype(BF16), ones(d_model),
                             tm=tm_n, tn=_pick(d_model, (2048, 1024, 512, 256, 128)))
        kv = _norm_matmul(memf, g_mem_kv[l], w_mkv[l].astype(BF16), ones(2 * d_model),
                          tm=_pick(batch * n_mem, (512, 256, 128)),
                          tn=_pick(2 * d_model, (2048, 1024, 512, 256, 128)))
        xf = _mem_attn_out(q_mem, kv, w_mo[l].astype(BF16), xf, g_mem_post[l],
                           batch=batch, seq=seq, n_mem=n_mem, tq=_pick(seq, (512, 256, 128)))
        act = _norm_swiglu(xf, g_ffn_pre[l], w_gate_up[l].astype(BF16),
                           tm=tm_n, tn=_pick(d_ff, (512, 256, 128)))
        xf = _matmul_norm_resid([act], w_down[l].astype(BF16), xf, g_ffn_post[l], tm=tm)
    return xf.reshape(batch, seq, d_model)
```
